```python
import jax, jax.numpy as jnp
from jax import lax
import numpy as np

D_MODEL = 1024
BATCH = 8
SEQ = 4096
DEPTH = 1
DEC_BATCH = 32
DEC_SEQ = 64
PAST_LEN = 1024

CHUNK = 64
N_HEADS = 8
HEAD_DIM = 64
W_ATT = N_HEADS * HEAD_DIM
W_REC = D_MODEL - W_ATT
W_MIX = W_ATT + W_REC
N_REC_BLOCKS = 8
REC_BLOCK = W_REC // N_REC_BLOCKS
CONV_W = 4
RG_C = 8.0
IN_COLS = 3 * W_ATT + N_HEADS + 2 * W_REC
N_GROUPS = 4
EXPERTS_PER_GROUP = 8
N_EXPERTS = N_GROUPS * EXPERTS_PER_GROUP
TOP_K = 2
D_EXPERT = 256
Q_BLOCK = 128
MOE_BLOCK = 128
EPS = 1e-6

kernel_name = "fox_rglru_hiermoe_adaln_stream"


def rmsnorm(x, g):
    xf = x.astype(jnp.float32)
    y = xf * lax.rsqrt(jnp.mean(xf * xf, axis=-1, keepdims=True) + EPS)
    return (y * g.astype(jnp.float32)).astype(x.dtype)


def forgetting_attention(q, k_all, v_all, logf_all, past_len):
    B, T = q.shape[0], q.shape[1]
    K = k_all.shape[1]
    F = jnp.cumsum(logf_all, axis=1)
    Fk = F.transpose(0, 2, 1)[:, :, None, :]
    Fq = F[:, past_len:]
    qb = Q_BLOCK if T % Q_BLOCK == 0 else T
    nb = T // qb
    kpos = jnp.arange(K)
    q_blocks = q.reshape(B, nb, qb, N_HEADS, HEAD_DIM).transpose(1, 0, 2, 3, 4)
    f_blocks = Fq.reshape(B, nb, qb, N_HEADS).transpose(1, 0, 2, 3)
    pos_blocks = (past_len + jnp.arange(T)).reshape(nb, qb)
    scale = HEAD_DIM ** -0.5

    def one_block(args):
        qi, fi, pi = args
        s = jnp.einsum('bqhd,bkhd->bhqk', qi, k_all).astype(jnp.float32) * scale
        s = s + fi.transpose(0, 2, 1)[..., None] - Fk
        s = jnp.where(kpos[None, None, None, :] <= pi[None, None, :, None], s, -jnp.inf)
        p = jax.nn.softmax(s, axis=-1)
        return jnp.einsum('bhqk,bkhd->bqhd', p.astype(v_all.dtype), v_all)

    out = lax.map(one_block, (q_blocks, f_blocks, pos_blocks))
    return out.transpose(1, 0, 2, 3, 4).reshape(B, T, W_ATT)


def rglru_branch(xb, gb, conv_past, h0, conv_w, conv_b, w_a, b_a, w_x, b_x, lam):
    B, T = xb.shape[0], xb.shape[1]
    xp = jnp.concatenate([conv_past.astype(xb.dtype), xb], axis=1)
    xc = conv_b + sum(xp[:, j:j + T] * conv_w[j] for j in range(CONV_W))
    conv_tail = xp[:, xp.shape[1] - (CONV_W - 1):]
    xcb = xc.reshape(B, T, N_REC_BLOCKS, REC_BLOCK)
    r = jax.nn.sigmoid(jnp.einsum('btnk,nkj->btnj', xcb, w_a).reshape(B, T, W_REC) + b_a)
    i = jax.nn.sigmoid(jnp.einsum('btnk,nkj->btnj', xcb, w_x).reshape(B, T, W_REC) + b_x)
    log_a = -RG_C * jax.nn.softplus(-lam.astype(jnp.float32)) * r.astype(jnp.float32)
    a = jnp.exp(log_a)
    u = jnp.sqrt(-jnp.expm1(2.0 * log_a)) * (i * xc).astype(jnp.float32)

    def step(h, inp):
        a_t, u_t = inp
        h = a_t * h + u_t
        return h, h

    hT, hs = lax.scan(step, h0.astype(jnp.float32), (a.transpose(1, 0, 2), u.transpose(1, 0, 2)))
    y = jax.nn.gelu(gb) * hs.transpose(1, 0, 2).astype(xb.dtype)
    return y, hT, conv_tail


def hier_moe(x2d, w_rg, b_rg, w_re, b_re, w_eg, w_eu, w_ed):
    N = x2d.shape[0]
    lg = (x2d @ w_rg + b_rg).astype(jnp.float32)
    pg = jax.nn.softmax(lg, axis=-1)
    g = jnp.argmax(lg, axis=-1).astype(jnp.int32)
    p_top = jnp.take_along_axis(pg, g[:, None], axis=1)[:, 0]
    le = (x2d @ w_re + b_re).astype(jnp.float32).reshape(N, N_GROUPS, EXPERTS_PER_GROUP)
    le_g = jnp.take_along_axis(le, g[:, None, None], axis=1)[:, 0]
    vals, idx = lax.top_k(le_g, TOP_K)
    wts = jax.nn.softmax(vals, axis=-1) * p_top[:, None]
    eid = (g[:, None] * EXPERTS_PER_GROUP + idx).reshape(-1).astype(jnp.int32)
    tok = jnp.repeat(jnp.arange(N, dtype=jnp.int32), TOP_K)
    gate = wts.reshape(-1)
    S = N * TOP_K
    order = jnp.argsort(eid)
    e_sorted = eid[order]
    counts = jnp.zeros((N_EXPERTS,), jnp.int32).at[eid].add(1)
    padded = ((counts + MOE_BLOCK - 1) // MOE_BLOCK) * MOE_BLOCK
    start = jnp.cumsum(counts) - counts
    pend = jnp.cumsum(padded)
    pstart = pend - padded
    dest = pstart[e_sorted] + (jnp.arange(S, dtype=jnp.int32) - start[e_sorted])
    nblk = -(-S // MOE_BLOCK) + N_EXPERTS
    P = nblk * MOE_BLOCK
    row_tok = jnp.zeros((P,), jnp.int32).at[dest].set(tok[order])
    row_gate = jnp.zeros((P,), jnp.float32).at[dest].set(gate[order])
    blk_e = jnp.minimum(jnp.searchsorted(pend, jnp.arange(nblk) * MOE_BLOCK, side='right'),
                        N_EXPERTS - 1).astype(jnp.int32)

    def expert_block(args):
        toks, e = args
        xb = x2d[toks]
        h = jax.nn.silu(xb @ w_eg[e]) * (xb @ w_eu[e])
        return h @ w_ed[e]

    out = lax.map(expert_block, (row_tok.reshape(nblk, MOE_BLOCK), blk_e))
    out = out.reshape(P, D_MODEL) * row_gate[:, None].astype(out.dtype)
    return jnp.zeros_like(x2d).at[row_tok].add(out)


def hybrid_layer(x, c, k_past, v_past, logf_past, h0, conv_past,
                 n1, n2, w_ada, b_ada, w_in, b_f, conv_w, conv_b, w_a, b_a, w_x, b_x, lam,
                 g_att, g_rec, w_out, w_rg, b_rg, w_re, b_re, w_eg, w_eu, w_ed):
    B, T = x.shape[0], x.shape[1]
    ada = jax.nn.silu(c) @ w_ada + b_ada
    sh1, sc1, ga1, sh2, sc2, ga2 = jnp.split(ada, 6, axis=-1)
    u = rmsnorm(x, n1) * (1 + sc1[:, None]) + sh1[:, None]
    z = u @ w_in
    o1 = W_ATT; o2 = 2 * W_ATT; o3 = 3 * W_ATT; o4 = o3 + N_HEADS; o5 = o4 + W_REC
    q = z[..., :o1].reshape(B, T, N_HEADS, HEAD_DIM)
    k = z[..., o1:o2].reshape(B, T, N_HEADS, HEAD_DIM)
    v = z[..., o2:o3].reshape(B, T, N_HEADS, HEAD_DIM)
    logf = jax.nn.log_sigmoid(z[..., o3:o4].astype(jnp.float32) + b_f.astype(jnp.float32))
    xb = z[..., o4:o5]
    gb = z[..., o5:]
    past = k_past.shape[1]
    k_all = jnp.concatenate([k_past.astype(k.dtype), k], axis=1)
    v_all = jnp.concatenate([v_past.astype(v.dtype), v], axis=1)
    logf_all = jnp.concatenate([logf_past.astype(jnp.float32), logf], axis=1)
    attn = forgetting_attention(q, k_all, v_all, logf_all, past)
    rec, hT, conv_tail = rglru_branch(xb, gb, conv_past, h0, conv_w, conv_b, w_a, b_a, w_x, b_x, lam)
    mix = jnp.concatenate([rmsnorm(attn, g_att), rmsnorm(rec, g_rec)], axis=-1) @ w_out
    x = x + ga1[:, None] * mix
    u2 = rmsnorm(x, n2) * (1 + sc2[:, None]) + sh2[:, None]
    y = hier_moe(u2.reshape(B * T, D_MODEL), w_rg, b_rg, w_re, b_re, w_eg, w_eu, w_ed).reshape(B, T, D_MODEL)
    x = x + ga2[:, None] * y
    return x, k, v, logf, hT, conv_tail


def setup_inputs(seed: int = 0) -> dict:
    key = jax.random.key(seed)
    ks = jax.random.split(key, 40)
    f32 = jnp.float32

    def nrm(k, shape, scale):
        return jax.random.normal(k, shape, f32) * scale

    lam_u = jax.random.uniform(ks[20], (DEPTH, W_REC), f32, minval=0.9, maxval=0.999)
    lam_s = lam_u ** (1.0 / RG_C)
    return {
        "x_prompt": nrm(ks[0], (BATCH, SEQ, D_MODEL), 1.0),
        "x_sample": nrm(ks[1], (DEC_BATCH, DEC_SEQ, D_MODEL), 1.0),
        "c_prompt": nrm(ks[2], (BATCH, D_MODEL), 1.0),
        "c_sample": nrm(ks[3], (DEC_BATCH, D_MODEL), 1.0),
        "cache_k": nrm(ks[4], (DEPTH, DEC_BATCH, PAST_LEN, N_HEADS, HEAD_DIM), 1.0),
        "cache_v": nrm(ks[5], (DEPTH, DEC_BATCH, PAST_LEN, N_HEADS, HEAD_DIM), 1.0),
        "cache_logf": jax.nn.log_sigmoid(3.0 + nrm(ks[6], (DEPTH, DEC_BATCH, PAST_LEN, N_HEADS), 1.0)),
        "state_h": nrm(ks[7], (DEPTH, DEC_BATCH, W_REC), 0.5),
        "state_conv": nrm(ks[8], (DEPTH, DEC_BATCH, CONV_W - 1, W_REC), 1.0),
        "norm1_g": 1.0 + nrm(ks[9], (DEPTH, D_MODEL), 0.05),
        "norm2_g": 1.0 + nrm(ks[10], (DEPTH, D_MODEL), 0.05),
        "w_ada": nrm(ks[11], (DEPTH, D_MODEL, 6 * D_MODEL), 0.5 * D_MODEL ** -0.5),
        "b_ada": nrm(ks[12], (DEPTH, 6 * D_MODEL), 0.02),
        "w_in": nrm(ks[13], (DEPTH, D_MODEL, IN_COLS), D_MODEL ** -0.5),
        "b_f": jnp.linspace(1.0, 5.0, N_HEADS, dtype=f32)[None, :] + nrm(ks[14], (DEPTH, N_HEADS), 0.1),
        "conv_w": nrm(ks[15], (DEPTH, CONV_W, W_REC), CONV_W ** -0.5),
        "conv_b": nrm(ks[16], (DEPTH, W_REC), 0.02),
        "w_rg_a": nrm(ks[17], (DEPTH, N_REC_BLOCKS, REC_BLOCK, REC_BLOCK), REC_BLOCK ** -0.5),
        "b_rg_a": nrm(ks[18], (DEPTH, W_REC), 0.02),
        "w_rg_x": nrm(ks[19], (DEPTH, N_REC_BLOCKS, REC_BLOCK, REC_BLOCK), REC_BLOCK ** -0.5),
        "b_rg_x": nrm(ks[21], (DEPTH, W_REC), 0.02),
        "rg_lambda": jnp.log(lam_s) - jnp.log1p(-lam_s),
        "out_g_att": 1.0 + nrm(ks[22], (DEPTH, W_ATT), 0.05),
        "out_g_rec": 1.0 + nrm(ks[23], (DEPTH, W_REC), 0.05),
        "w_out": nrm(ks[24], (DEPTH, W_MIX, D_MODEL), W_MIX ** -0.5),
        "w_route_group": nrm(ks[25], (DEPTH, D_MODEL, N_GROUPS), D_MODEL ** -0.5),
        "b_route_group": nrm(ks[26], (DEPTH, N_GROUPS), 0.01),
        "w_route_expert": nrm(ks[27], (DEPTH, D_MODEL, N_EXPERTS), D_MODEL ** -0.5),
        "b_route_expert": nrm(ks[28], (DEPTH, N_EXPERTS), 0.01),
        "w_exp_gate": nrm(ks[29], (DEPTH, N_EXPERTS, D_MODEL, D_EXPERT), D_MODEL ** -0.5),
        "w_exp_up": nrm(ks[30], (DEPTH, N_EXPERTS, D_MODEL, D_EXPERT), D_MODEL ** -0.5),
        "w_exp_down": nrm(ks[31], (DEPTH, N_EXPERTS, D_EXPERT, D_MODEL), D_EXPERT ** -0.5),
        "final_g": 1.0 + nrm(ks[32], (D_MODEL,), 0.05),
    }


def reference(x_prompt, x_sample, c_prompt, c_sample, cache_k, cache_v, cache_logf, state_h, state_conv,
              norm1_g, norm2_g, w_ada, b_ada, w_in, b_f, conv_w, conv_b, w_rg_a, b_rg_a, w_rg_x, b_rg_x,
              rg_lambda, out_g_att, out_g_rec, w_out, w_route_group, b_route_group, w_route_expert,
              b_route_expert, w_exp_gate, w_exp_up, w_exp_down, final_g):
    B = x_prompt.shape[0]
    xp, xs = x_prompt, x_sample
    kp_l, vp_l, fp_l, hp_l, cp_l = [], [], [], [], []
    ks_l, vs_l, fs_l, hs_l, cs_l = [], [], [], [], []
    for l in range(DEPTH):
        w = (norm1_g[l], norm2_g[l], w_ada[l], b_ada[l], w_in[l], b_f[l], conv_w[l], conv_b[l],
             w_rg_a[l], b_rg_a[l], w_rg_x[l], b_rg_x[l], rg_lambda[l], out_g_att[l], out_g_rec[l],
             w_out[l], w_route_group[l], b_route_group[l], w_route_expert[l], b_route_expert[l],
             w_exp_gate[l], w_exp_up[l], w_exp_down[l])
        empty_kv = jnp.zeros((B, 0, N_HEADS, HEAD_DIM), xp.dtype)
        empty_f = jnp.zeros((B, 0, N_HEADS), jnp.float32)
        h_zero = jnp.zeros((B, W_REC), jnp.float32)
        conv_zero = jnp.zeros((B, CONV_W - 1, W_REC), xp.dtype)
        xp, kp, vp, fp, hp, cp = hybrid_layer(xp, c_prompt, empty_kv, empty_kv, empty_f, h_zero, conv_zero, *w)
        xs, ks_, vs_, fs_, hs_, cs_ = hybrid_layer(xs, c_sample, cache_k[l], cache_v[l], cache_logf[l],
                                                   state_h[l], state_conv[l], *w)
        kp_l.append(kp); vp_l.append(vp); fp_l.append(fp); hp_l.append(hp); cp_l.append(cp)
        ks_l.append(ks_); vs_l.append(vs_); fs_l.append(fs_); hs_l.append(hs_); cs_l.append(cs_)
    y_prompt = rmsnorm(xp, final_g)
    y_sample = rmsnorm(xs, final_g)
    return (y_prompt, y_sample,
            jnp.stack(kp_l), jnp.stack(vp_l), jnp.stack(fp_l), jnp.stack(hp_l), jnp.stack(cp_l),
            jnp.stack(ks_l), jnp.stack(vs_l), jnp.stack(fs_l), jnp.stack(hs_l), jnp.stack(cs_l))
```

```python
import functools
import math

import jax
import jax.numpy as jnp
from jax import lax
from jax.experimental import pallas as pl
from jax.experimental.pallas import tpu as pltpu
from jax.experimental.pallas import tpu_sc as plsc

F32 = jnp.float32
BF16 = jnp.bfloat16
U32 = jnp.uint32

D_MODEL = 1024
N_HEADS = 8
HEAD_DIM = 64
W_ATT = N_HEADS * HEAD_DIM
W_REC = D_MODEL - W_ATT
N_REC_BLOCKS = 8
REC_BLOCK = W_REC // N_REC_BLOCKS
CONV_W = 4
RG_C = 8.0
N_GROUPS = 4
EXPERTS_PER_GROUP = 8
N_EXPERTS = N_GROUPS * EXPERTS_PER_GROUP
D_EXPERT = 256
EPS = 1e-6

LANES = 128
SUBLANES = 8
LOG2E = math.log2(math.e)
Q_SCALE = LOG2E * HEAD_DIM ** -0.5
NEG_BIG = -1e30
VMEM_LIMIT = 48 * 1024 * 1024

SEQ_ROWS = 64
BATCH_ROWS = 8
CUMSUM_CHUNK = 512
ATTN_TILE = 512
ROW_TILE = 512
EXPERT_BLOCK = 256
GATHER_WINDOW = 128
GATHER_COLS = 256
ROUTE_COLS = 128
EXPERT_COL0 = 8


def _split3(x):
    hi = x.astype(BF16)
    r1 = x - hi.astype(F32)
    mid = r1.astype(BF16)
    lo = (r1 - mid.astype(F32)).astype(BF16)
    return hi, mid, lo


def _dot(a, b):
    return jnp.dot(a, b, preferred_element_type=F32)


def _rms(x):
    return x * lax.rsqrt(jnp.mean(x * x, axis=-1, keepdims=True) + EPS)


def _pack_bf16_pairs(y):
    n = y.shape[-1] // 2
    yb = y.astype(BF16).astype(F32)
    lo = pltpu.bitcast(yb[:, :n], U32)
    hi = pltpu.bitcast(yb[:, n:], U32)
    return (lo >> 16) | hi


def _unpack_bf16_pairs(p):
    lo = pltpu.bitcast(p << 16, F32)
    hi = pltpu.bitcast(p & jnp.uint32(0xFFFF0000), F32)
    return lo, hi


def _ada_kernel(c_ref, w_ref, b_ref, o_ref):
    c = c_ref[...]
    a = c * jax.nn.sigmoid(c)
    a_hi, a_mid, _ = _split3(a)
    w_hi, w_mid, _ = _split3(w_ref[...])
    o_ref[...] = _dot(a_hi, w_hi) + _dot(a_mid, w_hi) + _dot(a_hi, w_mid) + b_ref[...]


def _ada(c, w_ada, b_ada):
    rows = c.shape[0]
    n = w_ada.shape[1]
    tn = 1536
    return pl.pallas_call(
        _ada_kernel,
        grid=(n // tn,),
        in_specs=[pl.BlockSpec((rows, D_MODEL), lambda j: (0, 0)),
                  pl.BlockSpec((D_MODEL, tn), lambda j: (0, j)),
                  pl.BlockSpec((1, tn), lambda j: (0, j))],
        out_specs=pl.BlockSpec((rows, tn), lambda j: (0, j)),
        out_shape=jax.ShapeDtypeStruct((rows, n), F32),
        compiler_params=pltpu.CompilerParams(vmem_limit_bytes=VMEM_LIMIT),
        name="ada",
    )(c, w_ada, b_ada.reshape(1, n))


def _inproj_kernel(x_ref, g1_ref, sh1_ref, n1_ref, w_ref, bf_ref, cw_ref, cb_ref, wa_ref, ba_ref, wx_ref, bx_ref,
                   lam_ref, grec_ref, cpast_ref, h0_ref,
                   q_ref, k_ref, v_ref, lf_ref, rec_ref, ht_ref, ctail_ref,
                   xbuf, a_scr, u_scr, hs_scr, h_scr):
    bb, tm, _ = x_ref.shape
    rows = bb * tm

    @pl.when(pl.program_id(1) == 0)
    def _():
        xbuf[:, 0:SUBLANES, :] = cpast_ref[...]
        h_scr[...] = h0_ref[...]

    x = x_ref[...]
    u = _rms(x) * (n1_ref[...] * g1_ref[...]) + sh1_ref[...]
    z = _dot(u.reshape(rows, D_MODEL).astype(BF16), w_ref[...])

    q_ref[...] = (z[:, 0:W_ATT] * Q_SCALE).reshape(bb, tm, W_ATT).astype(BF16)
    k_ref[...] = z[:, W_ATT:2 * W_ATT].reshape(bb, tm, W_ATT)
    v_ref[...] = z[:, 2 * W_ATT:3 * W_ATT].reshape(bb, tm, W_ATT)
    o_x = 3 * W_ATT
    o_g = o_x + W_REC
    o_f = o_g + W_REC
    xb = z[:, o_x:o_g].reshape(bb, tm, W_REC)
    gb = z[:, o_g:o_f]
    zf = z[:, o_f:o_f + LANES] + bf_ref[...]
    lf = jnp.minimum(zf, 0.0) - jnp.log1p(jnp.exp(-jnp.abs(zf)))
    lf_ref[0, 0] = lf.T[0:N_HEADS, :]

    xbuf[:, SUBLANES:SUBLANES + tm, :] = xb
    cw = cw_ref[...]
    xc = cb_ref[...] + cw[3:4, :] * xb
    for j in range(CONV_W - 1):
        off = SUBLANES - (CONV_W - 1) + j
        xc = xc + cw[j:j + 1, :] * xbuf[:, off:off + tm, :]
    tail = xbuf[:, tm:tm + SUBLANES, :]
    ctail_ref[...] = tail
    xbuf[:, 0:SUBLANES, :] = tail

    xc2 = xc.reshape(rows, W_REC)
    xcb = xc2.astype(BF16)
    r = jax.nn.sigmoid(_dot(xcb, wa_ref[...]) + ba_ref[...])
    i = jax.nn.sigmoid(_dot(xcb, wx_ref[...]) + bx_ref[...])
    nlam = -lam_ref[...]
    softplus = jnp.maximum(nlam, 0.0) + jnp.log1p(jnp.exp(-jnp.abs(nlam)))
    log_a = (-RG_C * softplus) * r
    t = jnp.tanh(log_a)
    neg_expm1 = -2.0 * t / (1.0 - t)
    a = jnp.exp(log_a)
    uu = jnp.sqrt(neg_expm1) * (i * xc2)
    n_lane_tiles = W_REC // LANES
    for c in range(n_lane_tiles):
        a_scr[c] = a[:, c * LANES:(c + 1) * LANES]
        u_scr[c] = uu[:, c * LANES:(c + 1) * LANES]

    for c in range(n_lane_tiles):
        h = h_scr[:, c * LANES:(c + 1) * LANES]
        for step in range(tm):
            rows_t = pl.ds(step, bb, stride=tm)
            h = a_scr[c, rows_t, :] * h + u_scr[c, rows_t, :]
            hs_scr[c, rows_t, :] = h
        h_scr[:, c * LANES:(c + 1) * LANES] = h
    ht_ref[...] = h_scr[...]

    gelu = 0.5 * gb * (1.0 + jnp.tanh(math.sqrt(2.0 / math.pi) * (gb + 0.044715 * (gb * gb * gb))))
    y = gelu * jnp.concatenate([hs_scr[c] for c in range(n_lane_tiles)], axis=1)
    rec_ref[...] = (_rms(y) * grec_ref[...]).reshape(bb, tm, W_REC).astype(BF16)


def _inproj(x, g1, sh1, n1, w_all, b_f, conv_w, conv_b, wa_bd, b_a, wx_bd, b_x, lam, g_rec, conv_past, h0):
    b, t, _ = x.shape
    bb, tm = BATCH_ROWS, SEQ_ROWS
    nb, nt = b // bb, t // tm
    rows = bb * tm
    wcols = w_all.shape[1]
    const2 = lambda shape: pl.BlockSpec(shape, lambda i, j: (0, 0))
    per_b = lambda last: pl.BlockSpec((bb, 1, last), lambda i, j: (i, 0, 0))
    seq = lambda last: pl.BlockSpec((bb, tm, last), lambda i, j: (i, j, 0))
    return pl.pallas_call(
        _inproj_kernel,
        grid=(nb, nt),
        in_specs=[seq(D_MODEL), per_b(D_MODEL), per_b(D_MODEL), const2((1, D_MODEL)),
                  const2((D_MODEL, wcols)), const2((1, LANES)),
                  const2((CONV_W, W_REC)), const2((1, W_REC)),
                  const2((W_REC, W_REC)), const2((1, W_REC)), const2((W_REC, W_REC)), const2((1, W_REC)),
                  const2((1, W_REC)), const2((1, W_REC)),
                  pl.BlockSpec((bb, SUBLANES, W_REC), lambda i, j: (i, 0, 0)),
                  pl.BlockSpec((bb, W_REC), lambda i, j: (i, 0))],
        out_specs=[seq(W_ATT), seq(W_ATT), seq(W_ATT),
                   pl.BlockSpec((1, 1, N_HEADS, rows), lambda i, j: (i, j, 0, 0)),
                   seq(W_REC),
                   pl.BlockSpec((bb, W_REC), lambda i, j: (i, 0)),
                   pl.BlockSpec((bb, SUBLANES, W_REC), lambda i, j: (i, 0, 0))],
        out_shape=[jax.ShapeDtypeStruct((b, t, W_ATT), BF16),
                   jax.ShapeDtypeStruct((b, t, W_ATT), F32),
                   jax.ShapeDtypeStruct((b, t, W_ATT), F32),
                   jax.ShapeDtypeStruct((nb, nt, N_HEADS, rows), F32),
                   jax.ShapeDtypeStruct((b, t, W_REC), BF16),
                   jax.ShapeDtypeStruct((b, W_REC), F32),
                   jax.ShapeDtypeStruct((b, SUBLANES, W_REC), F32)],
        scratch_shapes=[pltpu.VMEM((bb, tm + SUBLANES, W_REC), F32),
                        pltpu.VMEM((W_REC // LANES, rows, LANES), F32),
                        pltpu.VMEM((W_REC // LANES, rows, LANES), F32),
                        pltpu.VMEM((W_REC // LANES, rows, LANES), F32),
                        pltpu.VMEM((bb, W_REC), F32)],
        compiler_params=pltpu.CompilerParams(dimension_semantics=("arbitrary", "arbitrary"),
                                             vmem_limit_bytes=VMEM_LIMIT),
        name="inproj",
    )(x, g1, sh1, n1, w_all, b_f, conv_w, conv_b, wa_bd, b_a, wx_bd, b_x, lam, g_rec, conv_past, h0)


def _cumsum_kernel(x_ref, o_ref):
    length = x_ref.shape[2]
    c = CUMSUM_CHUNK
    upper = (lax.broadcasted_iota(jnp.int32, (c, c), 0) <= lax.broadcasted_iota(jnp.int32, (c, c), 1)).astype(BF16)
    carry = jnp.zeros((N_HEADS, 1), F32)
    for j in range(length // c):
        x = x_ref[0, :, j * c:(j + 1) * c]
        hi, mid, lo = _split3(x)
        parts = jnp.concatenate([hi.astype(F32), mid.astype(F32), lo.astype(F32), jnp.zeros_like(x)], axis=0)
        sums = _dot(parts.astype(BF16), upper)
        out = sums[0:8] + sums[8:16] + sums[16:24] + carry
        o_ref[0, :, j * c:(j + 1) * c] = out
        carry = out[:, c - 1:c]


def _cumsum_time(lf_t):
    b, h, length = lf_t.shape
    return pl.pallas_call(
        _cumsum_kernel,
        grid=(b,),
        in_specs=[pl.BlockSpec((1, h, length), lambda i: (i, 0, 0))],
        out_specs=pl.BlockSpec((1, h, length), lambda i: (i, 0, 0)),
        out_shape=jax.ShapeDtypeStruct((b, h, length), F32),
        name="cumsum",
    )(lf_t)


def _attn_tile(qh, kt, vt, bias, state, mask):
    m, l, acc = state
    s = lax.dot_general(qh, kt, (((1,), (1,)), ((), ())), preferred_element_type=F32) + bias
    if mask is not None:
        s = jnp.where(mask, s, NEG_BIG)
    m_new = jnp.maximum(m, jnp.max(s, axis=1, keepdims=True))
    alpha = jnp.exp2(m - m_new)
    p = jnp.exp2(s - m_new)
    l = alpha * l + jnp.sum(p, axis=1, keepdims=True)
    acc = alpha * acc + _dot(p.astype(BF16), vt)
    return m_new, l, acc


def _attn_kernel(*refs, n_past_tiles, past_tile, tq):
    if n_past_tiles:
        q_ref, kn_ref, vn_ref, fn_ref, kp_ref, vp_ref, fp_ref, o_ref, kb, vb = refs
        past = n_past_tiles * past_tile
    else:
        q_ref, kn_ref, vn_ref, fn_ref, o_ref, kb, vb = refs
        past = 0
    qi = pl.program_id(2)
    t_new = kn_ref.shape[1]

    @pl.when(qi == 0)
    def _():
        if n_past_tiles:
            kb[0:past, :] = kp_ref[0].astype(BF16)
            vb[0:past, :] = vp_ref[0].astype(BF16)
        kb[past:past + t_new, :] = kn_ref[0].astype(BF16)
        vb[past:past + t_new, :] = vn_ref[0].astype(BF16)

    q = q_ref[0]
    lane = lax.broadcasted_iota(jnp.int32, q.shape, 1)
    first_head = lane < HEAD_DIM
    qh = (jnp.where(first_head, q, jnp.zeros_like(q)), jnp.where(first_head, jnp.zeros_like(q), q))

    f_diag = fn_ref[0, 0, qi]
    ref_f = f_diag[:, 0:1]

    def bias_of(f_tile, hd):
        return (ref_f[hd:hd + 1, :] - f_tile[hd:hd + 1, :]) * LOG2E

    init = (jnp.full((tq, 1), NEG_BIG, F32), jnp.zeros((tq, 1), F32), jnp.zeros((tq, LANES), F32))
    state = (init, init)

    for j in range(n_past_tiles):
        kt = kb[j * past_tile:(j + 1) * past_tile, :]
        vt = vb[j * past_tile:(j + 1) * past_tile, :]
        f_tile = fp_ref[0, 0, j]
        state = tuple(_attn_tile(qh[hd], kt, vt, bias_of(f_tile, hd), state[hd], None) for hd in range(2))

    def body(j, st):
        start = pl.multiple_of(past + j * tq, tq)
        kt = kb[pl.ds(start, tq), :]
        vt = vb[pl.ds(start, tq), :]
        f_tile = fn_ref[0, 0, j]
        return tuple(_attn_tile(qh[hd], kt, vt, bias_of(f_tile, hd), st[hd], None) for hd in range(2))

    state = lax.fori_loop(0, qi, body, state)

    start = pl.multiple_of(past + qi * tq, tq)
    kt = kb[pl.ds(start, tq), :]
    vt = vb[pl.ds(start, tq), :]
    causal = lax.broadcasted_iota(jnp.int32, (tq, tq), 1) <= lax.broadcasted_iota(jnp.int32, (tq, tq), 0)
    state = tuple(_attn_tile(qh[hd], kt, vt, bias_of(f_diag, hd), state[hd], causal) for hd in range(2))

    outs = [acc / l for (_, l, acc) in state]
    o_ref[0] = jnp.where(first_head, outs[0], outs[1]).astype(BF16)


def _attention(q, k_new, v_new, f_new, k_past=None, v_past=None, f_past=None):
    b, t, _ = q.shape
    tq = min(t, ATTN_TILE)
    nq = t // tq
    pairs = N_HEADS // 2
    q_spec = pl.BlockSpec((1, tq, LANES), lambda i, p, j: (i, j, p))
    full = lambda rows: pl.BlockSpec((1, rows, LANES), lambda i, p, j: (i, 0, p))
    f_spec = lambda arr: pl.BlockSpec((1, 1) + arr.shape[2:], lambda i, p, j: (i, p, 0, 0, 0))
    in_specs = [q_spec, full(t), full(t), f_spec(f_new)]
    args = [q, k_new, v_new, f_new]
    past = 0
    n_past_tiles = 0
    past_tile = 0
    if k_past is not None:
        past = k_past.shape[1]
        past_tile = f_past.shape[-1]
        n_past_tiles = past // past_tile
        in_specs += [full(past), full(past), f_spec(f_past)]
        args += [k_past, v_past, f_past]
    return pl.pallas_call(
        functools.partial(_attn_kernel, n_past_tiles=n_past_tiles, past_tile=past_tile, tq=tq),
        grid=(b, pairs, nq),
        in_specs=in_specs,
        out_specs=q_spec,
        out_shape=jax.ShapeDtypeStruct((b, t, W_ATT), BF16),
        scratch_shapes=[pltpu.VMEM((past + t, LANES), BF16), pltpu.VMEM((past + t, LANES), BF16)],
        compiler_params=pltpu.CompilerParams(dimension_semantics=("arbitrary", "arbitrary", "arbitrary"),
                                             vmem_limit_bytes=VMEM_LIMIT),
        name="attn",
    )(*args)


def _outproj_kernel(attn_ref, rec_ref, x_ref, ga1_ref, g2_ref, sh2_ref, n2_ref, gatt_ref, woa_ref, wor_ref,
                    wrh_ref, wrm_ref, br_ref,
                    x1_ref, u2p_ref, route_ref):
    bb, tm, _ = x_ref.shape
    rows = bb * tm
    attn = attn_ref[...].astype(F32)
    an = (_rms(attn) * gatt_ref[...]).reshape(rows, W_ATT).astype(BF16)
    mix = _dot(an, woa_ref[...]) + _dot(rec_ref[...].reshape(rows, W_REC), wor_ref[...])
    x1 = x_ref[...] + ga1_ref[...] * mix.reshape(bb, tm, D_MODEL)
    x1_ref[...] = x1
    u2 = (_rms(x1) * (n2_ref[...] * g2_ref[...]) + sh2_ref[...]).reshape(rows, D_MODEL)
    u2p_ref[...] = _pack_bf16_pairs(u2).reshape(bb, tm, D_MODEL // 2)

    u_hi, u_mid, _ = _split3(u2)
    logits = _dot(u_hi, wrh_ref[...]) + _dot(u_mid, wrh_ref[...]) + _dot(u_hi, wrm_ref[...]) + br_ref[...]
    lt = logits.T
    row8 = lax.broadcasted_iota(jnp.int32, (SUBLANES, rows), 0).astype(F32)
    lg = lt[0:SUBLANES]
    m_g = jnp.max(lg, axis=0, keepdims=True)
    gidx = jnp.min(jnp.where(lg == m_g, row8, float(SUBLANES)), axis=0, keepdims=True)
    p_top = 1.0 / jnp.sum(jnp.exp(lg - m_g), axis=0, keepdims=True)
    leg = jnp.zeros((EXPERTS_PER_GROUP, rows), F32)
    for g in range(N_GROUPS):
        lo = EXPERT_COL0 + g * EXPERTS_PER_GROUP
        leg = jnp.where(gidx == float(g), lt[lo:lo + EXPERTS_PER_GROUP], leg)
    v1 = jnp.max(leg, axis=0, keepdims=True)
    i1 = jnp.min(jnp.where(leg == v1, row8, float(SUBLANES)), axis=0, keepdims=True)
    leg2 = jnp.where(row8 == i1, -jnp.inf, leg)
    v2 = jnp.max(leg2, axis=0, keepdims=True)
    i2 = jnp.min(jnp.where(leg2 == v2, row8, float(SUBLANES)), axis=0, keepdims=True)
    e21 = jnp.exp(v2 - v1)
    w1 = p_top / (1.0 + e21)
    w2 = w1 * e21
    base = gidx * float(EXPERTS_PER_GROUP)
    out = jnp.where(row8 == 3.0, w2, 0.0)
    for r_idx, val in ((2.0, w1), (1.0, base + i2), (0.0, base + i1)):
        out = jnp.where(row8 == r_idx, val, out)
    route_ref[0, 0] = out


def _outproj(attn, rec, x, ga1, g2, sh2, n2, g_att, wo_a, wo_r, wr_hi, wr_mid, b_r, bb, tm):
    b, t, _ = x.shape
    nb, nt = b // bb, t // tm
    rows = bb * tm
    const2 = lambda shape: pl.BlockSpec(shape, lambda i, j: (0, 0))
    per_b = pl.BlockSpec((bb, 1, D_MODEL), lambda i, j: (i, 0, 0))
    seq = lambda last: pl.BlockSpec((bb, tm, last), lambda i, j: (i, j, 0))
    return pl.pallas_call(
        _outproj_kernel,
        grid=(nb, nt),
        in_specs=[seq(W_ATT), seq(W_REC), seq(D_MODEL), per_b, per_b, per_b, const2((1, D_MODEL)),
                  const2((1, W_ATT)), const2((W_ATT, D_MODEL)), const2((W_REC, D_MODEL)),
                  const2((D_MODEL, ROUTE_COLS)), const2((D_MODEL, ROUTE_COLS)), const2((1, ROUTE_COLS))],
        out_specs=[seq(D_MODEL), seq(D_MODEL // 2),
                   pl.BlockSpec((1, 1, SUBLANES, rows), lambda i, j: (i, j, 0, 0))],
        out_shape=[jax.ShapeDtypeStruct((b, t, D_MODEL), F32),
                   jax.ShapeDtypeStruct((b, t, D_MODEL // 2), U32),
                   jax.ShapeDtypeStruct((nb, nt, SUBLANES, rows), F32)],
        compiler_params=pltpu.CompilerParams(dimension_semantics=("arbitrary", "arbitrary"),
                                             vmem_limit_bytes=VMEM_LIMIT),
        name="outproj",
    )(attn, rec, x, ga1, g2, sh2, n2, g_att, wo_a, wo_r, wr_hi, wr_mid, b_r)


def _sc_gather(table, idx):
    m = idx.shape[0]
    d = table.shape[1]
    mesh = plsc.VectorSubcoreMesh(core_axis_name="core", subcore_axis_name="subcore")

    @pl.kernel(out_type=jax.ShapeDtypeStruct((m, d), table.dtype), mesh=mesh, scratch_types=[])
    def gather(x_hbm, i_hbm, o_hbm):
        def body(i_vmem, o_vmem):
            pltpu.sync_copy(x_hbm.at[i_vmem.at[0]], o_vmem)

        pltpu.emit_pipeline(
            body,
            grid=(m // GATHER_WINDOW,),
            in_specs=[pl.BlockSpec((1, GATHER_WINDOW), lambda i: (0, i))],
            out_specs=[pl.BlockSpec((GATHER_WINDOW, d), lambda i: (i, 0))],
            core_axis_name=("core", "subcore"),
            dimension_semantics=(pltpu.PARALLEL,),
        )(i_hbm, o_hbm)

    return gather(table, idx.reshape(1, m))


def _gather_rows(table, idx):
    v, width = table.shape
    pieces = width // GATHER_COLS
    idx2 = (idx[:, None] * pieces + jnp.arange(pieces, dtype=jnp.int32)[None, :]).reshape(-1)
    out = _sc_gather(table.reshape(v * pieces, GATHER_COLS), idx2)
    return out.reshape(idx.shape[0], width)


def _expert_kernel(blk_e_ref, blk_first_ref, n_used_ref, xs_ref, wg_ref, wu_ref, wd_ref, yo_ref, wg_b, wu_b, wd_b):
    b = pl.program_id(0)

    @pl.when(blk_first_ref[b] == 1)
    def _():
        wg_b[...] = wg_ref[0].astype(BF16)
        wu_b[...] = wu_ref[0].astype(BF16)
        wd_b[...] = wd_ref[0].astype(BF16)

    @pl.when(b < n_used_ref[0])
    def _():
        half = D_MODEL // 2
        x_lo, x_hi = _unpack_bf16_pairs(xs_ref[...])
        x_lo = x_lo.astype(BF16)
        x_hi = x_hi.astype(BF16)
        g = _dot(x_lo, wg_b[0:half, :]) + _dot(x_hi, wg_b[half:D_MODEL, :])
        u = _dot(x_lo, wu_b[0:half, :]) + _dot(x_hi, wu_b[half:D_MODEL, :])
        h = (g * jax.nn.sigmoid(g)) * u
        yo_ref[...] = _pack_bf16_pairs(_dot(h.astype(BF16), wd_b[...]))

    @pl.when(b >= n_used_ref[0])
    def _():
        yo_ref[...] = jnp.zeros(yo_ref.shape, U32)


def _experts(xs, blk_e, blk_first, n_used, w_gate, w_up, w_down):
    p = xs.shape[0]
    tb = EXPERT_BLOCK
    nblk = p // tb
    half = D_MODEL // 2
    grid_spec = pltpu.PrefetchScalarGridSpec(
        num_scalar_prefetch=3,
        grid=(nblk,),
        in_specs=[pl.BlockSpec((tb, half), lambda i, e, f, n: (i, 0)),
                  pl.BlockSpec((1, D_MODEL, D_EXPERT), lambda i, e, f, n: (e[i], 0, 0)),
                  pl.BlockSpec((1, D_MODEL, D_EXPERT), lambda i, e, f, n: (e[i], 0, 0)),
                  pl.BlockSpec((1, D_EXPERT, D_MODEL), lambda i, e, f, n: (e[i], 0, 0))],
        out_specs=pl.BlockSpec((tb, half), lambda i, e, f, n: (i, 0)),
        scratch_shapes=[pltpu.VMEM((D_MODEL, D_EXPERT), BF16),
                        pltpu.VMEM((D_MODEL, D_EXPERT), BF16),
                        pltpu.VMEM((D_EXPERT, D_MODEL), BF16)],
    )
    return pl.pallas_call(
        _expert_kernel,
        grid_spec=grid_spec,
        out_shape=jax.ShapeDtypeStruct((p, half), U32),
        compiler_params=pltpu.CompilerParams(dimension_semantics=("arbitrary",), vmem_limit_bytes=VMEM_LIMIT),
        name="experts",
    )(blk_e, blk_first, n_used, xs, w_gate, w_up, w_down)


def _combine_kernel(x1_ref, yg_ref, w_ref, ga2_ref, fg_ref, o_ref):
    bb, tm, _ = x1_ref.shape
    half = D_MODEL // 2
    yg = yg_ref[...]
    w = w_ref[...]
    y = None
    for k in range(2):
        lo, hi = _unpack_bf16_pairs(yg[:, k * half:(k + 1) * half])
        yk = w[:, k:k + 1] * jnp.concatenate([lo, hi], axis=1)
        y = yk if y is None else y + yk
    out = x1_ref[...] + ga2_ref[...] * y.reshape(bb, tm, D_MODEL)
    o_ref[...] = _rms(out) * fg_ref[...]


def _combine(x1, yg, wts, ga2, final_g, row_block0, bb, tm):
    b, t, _ = x1.shape
    nb, nt = b // bb, t // tm
    rows = bb * tm
    flat = lambda last: pl.BlockSpec((rows, last), lambda i, j: (row_block0 + i * nt + j, 0))
    return pl.pallas_call(
        _combine_kernel,
        grid=(nb, nt),
        in_specs=[pl.BlockSpec((bb, tm, D_MODEL), lambda i, j: (i, j, 0)),
                  flat(D_MODEL), flat(2),
                  pl.BlockSpec((bb, 1, D_MODEL), lambda i, j: (i, 0, 0)),
                  pl.BlockSpec((1, D_MODEL), lambda i, j: (0, 0))],
        out_specs=pl.BlockSpec((bb, tm, D_MODEL), lambda i, j: (i, j, 0)),
        out_shape=jax.ShapeDtypeStruct((b, t, D_MODEL), F32),
        compiler_params=pltpu.CompilerParams(dimension_semantics=("arbitrary", "arbitrary"),
                                             vmem_limit_bytes=VMEM_LIMIT),
        name="combine",
    )(x1, yg, wts, ga2, final_g)


def _block_diag(w):
    n, k, _ = w.shape
    eye = jnp.eye(n, dtype=w.dtype)
    return (eye[:, None, :, None] * w[:, :, None, :]).reshape(n * k, n * k)


def _tiles(f, tile):
    b, _, length = f.shape
    return f.reshape(b, N_HEADS // 2, 2, length // tile, tile).transpose(0, 1, 3, 2, 4)


def _untile_rows(a, b, t):
    nb, nt, r, rows = a.shape
    bb = b // nb
    tm = t // nt
    return a.reshape(nb, nt, r, bb, tm).transpose(2, 0, 3, 1, 4).reshape(r, b, t)


def _group_front(x, c_mod, layer, conv_past, h0, cache):
    b, t, _ = x.shape
    sh1, sc1, ga1, sh2, sc2, ga2 = [m.reshape(b, 1, D_MODEL) for m in jnp.split(c_mod, 6, axis=-1)]
    q, k, v, lf_steps, rec, h_t, ctail = _inproj(
        x, 1.0 + sc1, sh1, layer["n1"], layer["w_all"], layer["b_f"], layer["conv_w"], layer["conv_b"],
        layer["wa_bd"], layer["b_a"], layer["wx_bd"], layer["b_x"], layer["lam"], layer["g_rec"], conv_past, h0)
    lf_t = _untile_rows(lf_steps, b, t).transpose(1, 0, 2)
    tq = min(t, ATTN_TILE)
    if cache is None:
        f_all = _cumsum_time(lf_t)
        attn = _attention(q, k, v, _tiles(f_all, tq))
    else:
        k_past, v_past, lf_past = cache
        past = k_past.shape[1]
        total = past + t
        padded = -(-total // CUMSUM_CHUNK) * CUMSUM_CHUNK
        lf_all = jnp.concatenate([lf_past.transpose(0, 2, 1), lf_t,
                                  jnp.zeros((b, N_HEADS, padded - total), F32)], axis=2)
        f_all = _cumsum_time(lf_all)
        attn = _attention(q, k, v, _tiles(f_all[:, :, past:total], tq),
                          k_past.reshape(b, past, W_ATT), v_past.reshape(b, past, W_ATT),
                          _tiles(f_all[:, :, :past], ATTN_TILE))
    bb, tm = (1, ROW_TILE) if t >= ROW_TILE else (ROW_TILE // t, t)
    x1, u2p, route = _outproj(attn, rec, x, ga1, 1.0 + sc2, sh2, layer["n2"], layer["g_att"],
                              layer["wo_a"], layer["wo_r"], layer["wr_hi"], layer["wr_mid"], layer["b_r"], bb, tm)
    route = _untile_rows(route, b, t).reshape(SUBLANES, b * t)
    leaves = (k.reshape(1, b, t, N_HEADS, HEAD_DIM), v.reshape(1, b, t, N_HEADS, HEAD_DIM),
              lf_t.transpose(0, 2, 1)[None], h_t[None], ctail[None, :, SUBLANES - (CONV_W - 1):, :])
    return x1, u2p.reshape(b * t, D_MODEL // 2), route, ga2, (bb, tm), leaves


def kernel(x_prompt, x_sample, c_prompt, c_sample, cache_k, cache_v, cache_logf, state_h, state_conv, norm1_g, norm2_g, w_ada, b_ada, w_in, b_f, conv_w, conv_b, w_rg_a, b_rg_a, w_rg_x, b_rg_x, rg_lambda, out_g_att, out_g_rec, w_out, w_route_group, b_route_group, w_route_expert, b_route_expert, w_exp_gate, w_exp_up, w_exp_down, final_g):
    bp, tp, _ = x_prompt.shape
    bs, ts, _ = x_sample.shape
    l = 0
    o1, o2, o3 = W_ATT, 2 * W_ATT, 3 * W_ATT
    o4 = o3 + N_HEADS
    o5 = o4 + W_REC
    w_in_l = w_in[l]
    w_all = jnp.concatenate([w_in_l[:, :o3], w_in_l[:, o4:], w_in_l[:, o3:o4],
                             jnp.zeros((D_MODEL, LANES - N_HEADS), F32)], axis=1).astype(BF16)
    w_r = jnp.zeros((D_MODEL, ROUTE_COLS), F32)
    w_r = w_r.at[:, 0:N_GROUPS].set(w_route_group[l]).at[:, EXPERT_COL0:EXPERT_COL0 + N_EXPERTS].set(w_route_expert[l])
    wr_hi = w_r.astype(BF16)
    wr_mid = (w_r - wr_hi.astype(F32)).astype(BF16)
    b_r = jnp.zeros((1, ROUTE_COLS), F32)
    b_r = b_r.at[0, 0:N_GROUPS].set(b_route_group[l]).at[0, N_GROUPS:SUBLANES].set(NEG_BIG)
    b_r = b_r.at[0, EXPERT_COL0:EXPERT_COL0 + N_EXPERTS].set(b_route_expert[l])
    layer = {
        "n1": norm1_g[l].reshape(1, D_MODEL), "n2": norm2_g[l].reshape(1, D_MODEL),
        "w_all": w_all,
        "b_f": jnp.concatenate([b_f[l], jnp.zeros((LANES - N_HEADS,), F32)]).reshape(1, LANES),
        "conv_w": conv_w[l], "conv_b": conv_b[l].reshape(1, W_REC),
        "wa_bd": _block_diag(w_rg_a[l]).astype(BF16), "b_a": b_rg_a[l].reshape(1, W_REC),
        "wx_bd": _block_diag(w_rg_x[l]).astype(BF16), "b_x": b_rg_x[l].reshape(1, W_REC),
        "lam": rg_lambda[l].reshape(1, W_REC), "g_rec": out_g_rec[l].reshape(1, W_REC),
        "g_att": out_g_att[l].reshape(1, W_ATT),
        "wo_a": w_out[l, :W_ATT].astype(BF16), "wo_r": w_out[l, W_ATT:].astype(BF16),
        "wr_hi": wr_hi, "wr_mid": wr_mid, "b_r": b_r,
    }

    ada = _ada(jnp.concatenate([c_prompt, c_sample], axis=0), w_ada[l], b_ada[l])
    pad_rows = SUBLANES - (CONV_W - 1)
    x1p, u2p_p, route_p, ga2p, tile_p, leaves_p = _group_front(
        x_prompt, ada[:bp], layer, jnp.zeros((bp, SUBLANES, W_REC), F32), jnp.zeros((bp, W_REC), F32), None)
    conv_past_s = jnp.concatenate([jnp.zeros((bs, pad_rows, W_REC), F32), state_conv[l]], axis=1)
    x1s, u2p_s, route_s, ga2s, tile_s, leaves_s = _group_front(
        x_sample, ada[bp:], layer, conv_past_s, state_h[l], (cache_k[l], cache_v[l], cache_logf[l]))

    n_tok = bp * tp + bs * ts
    route = jnp.concatenate([route_p, route_s], axis=1)
    eid = route[0:2].T.astype(jnp.int32).reshape(-1)
    wts = route[2:4].T
    n_slot = 2 * n_tok
    tb = EXPERT_BLOCK
    onehot = (eid[:, None] == jnp.arange(N_EXPERTS, dtype=jnp.int32)[None, :]).astype(jnp.int32)
    csum = jnp.cumsum(onehot, axis=0)
    counts = csum[-1]
    rank = jnp.take_along_axis(csum, eid[:, None], axis=1)[:, 0] - 1
    padded = ((counts + tb - 1) // tb) * tb
    pend = jnp.cumsum(padded)
    dest = (pend - padded)[eid] + rank
    nblk = -(-n_slot // tb) + N_EXPERTS
    row_tok = jnp.zeros((nblk * tb,), jnp.int32).at[dest].set(jnp.arange(n_slot, dtype=jnp.int32) // 2)
    blk_e = jnp.minimum(jnp.searchsorted(pend, jnp.arange(nblk, dtype=jnp.int32) * tb, side="right"),
                        N_EXPERTS - 1).astype(jnp.int32)
    blk_first = jnp.concatenate([jnp.ones((1,), jnp.int32), (blk_e[1:] != blk_e[:-1]).astype(jnp.int32)])
    n_used = (pend[-1:] // tb).astype(jnp.int32)

    u2p = jnp.concatenate([u2p_p, u2p_s], axis=0)
    xs = _gather_rows(u2p, row_tok)
    yo = _experts(xs, blk_e, blk_first, n_used, w_exp_gate[l], w_exp_up[l], w_exp_down[l])
    yg = _gather_rows(yo, dest).reshape(n_tok, D_MODEL)

    fg = final_g.reshape(1, D_MODEL)
    y_prompt = _combine(x1p, yg, wts, ga2p, fg, 0, *tile_p)
    y_sample = _combine(x1s, yg, wts, ga2s, fg, (bp * tp) // ROW_TILE, *tile_s)
    return (y_prompt, y_sample) + leaves_p + leaves_s
```

```python
import functools
import math

import jax
import jax.numpy as jnp
from jax import lax
from jax.experimental import pallas as pl
from jax.experimental.pallas import tpu as pltpu
from jax.experimental.pallas import tpu_sc as plsc

F32 = jnp.float32
BF16 = jnp.bfloat16
U32 = jnp.uint32

D_MODEL = 1024
N_HEADS = 8
HEAD_DIM = 64
W_ATT = N_HEADS * HEAD_DIM
W_REC = D_MODEL - W_ATT
N_REC_BLOCKS = 8
REC_BLOCK = W_REC // N_REC_BLOCKS
CONV_W = 4
RG_C = 8.0
N_GROUPS = 4
EXPERTS_PER_GROUP = 8
N_EXPERTS = N_GROUPS * EXPERTS_PER_GROUP
D_EXPERT = 256
EPS = 1e-6

LANES = 128
SUBLANES = 8
LOG2E = math.log2(math.e)
Q_SCALE = LOG2E * HEAD_DIM ** -0.5
NEG_BIG = -1e30
VMEM_LIMIT = 48 * 1024 * 1024

SEQ_ROWS = 64
BATCH_ROWS = 8
CUMSUM_CHUNK = 512
ATTN_TILE = 512
ROW_TILE = 512
EXPERT_BLOCK = 256
GATHER_WINDOW = 128
PIECE_COLS = 256
N_PIECES = D_MODEL // (2 * PIECE_COLS)
ROUTE_COLS = 128
EXPERT_COL0 = 8


def _split3(x):
    hi = x.astype(BF16)
    r1 = x - hi.astype(F32)
    mid = r1.astype(BF16)
    lo = (r1 - mid.astype(F32)).astype(BF16)
    return hi, mid, lo


def _dot(a, b):
    return jnp.dot(a, b, preferred_element_type=F32)


def _rms(x):
    return x * lax.rsqrt(jnp.mean(x * x, axis=-1, keepdims=True) + EPS)


def _pack_bf16_pairs(y):
    n = y.shape[-1] // 2
    yb = y.astype(BF16).astype(F32)
    lo = pltpu.bitcast(yb[:, :n], U32)
    hi = pltpu.bitcast(yb[:, n:], U32)
    return (lo >> 16) | hi


def _unpack_bf16_pairs(p):
    lo = pltpu.bitcast(p << 16, F32)
    hi = pltpu.bitcast(p & jnp.uint32(0xFFFF0000), F32)
    return lo, hi


def _pack_pieces(y):
    w = 2 * PIECE_COLS
    return [_pack_bf16_pairs(y[:, i * w:(i + 1) * w]) for i in range(N_PIECES)]


def _unpack_pieces(pieces):
    chunks = []
    for p in pieces:
        chunks.extend(_unpack_bf16_pairs(p))
    return chunks


def _ada_kernel(c_ref, w_ref, b_ref, o_ref):
    c = c_ref[...]
    a = c * jax.nn.sigmoid(c)
    a_hi, a_mid, _ = _split3(a)
    w_hi, w_mid, _ = _split3(w_ref[...])
    o_ref[...] = _dot(a_hi, w_hi) + _dot(a_mid, w_hi) + _dot(a_hi, w_mid) + b_ref[...]


def _ada(c, w_ada, b_ada):
    rows = c.shape[0]
    n = w_ada.shape[1]
    tn = 1536
    return pl.pallas_call(
        _ada_kernel,
        grid=(n // tn,),
        in_specs=[pl.BlockSpec((rows, D_MODEL), lambda j: (0, 0)),
                  pl.BlockSpec((D_MODEL, tn), lambda j: (0, j)),
                  pl.BlockSpec((1, tn), lambda j: (0, j))],
        out_specs=pl.BlockSpec((rows, tn), lambda j: (0, j)),
        out_shape=jax.ShapeDtypeStruct((rows, n), F32),
        compiler_params=pltpu.CompilerParams(vmem_limit_bytes=VMEM_LIMIT),
        name="ada",
    )(c, w_ada, b_ada.reshape(1, n))


def _inproj_kernel(x_ref, g1_ref, sh1_ref, n1_ref, w_ref, bf_ref, cw_ref, cb_ref, wa_ref, ba_ref, wx_ref, bx_ref,
                   lam_ref, grec_ref, cpast_ref, h0_ref,
                   q_ref, k_ref, v_ref, lf_ref, rec_ref, ht_ref, ctail_ref,
                   xbuf, a_scr, u_scr, hs_scr, h_scr):
    bb, tm, _ = x_ref.shape
    rows = bb * tm

    @pl.when(pl.program_id(1) == 0)
    def _():
        xbuf[:, 0:SUBLANES, :] = cpast_ref[...]
        h_scr[...] = h0_ref[...]

    x = x_ref[...]
    u = _rms(x) * (n1_ref[...] * g1_ref[...]) + sh1_ref[...]
    z = _dot(u.reshape(rows, D_MODEL).astype(BF16), w_ref[...])

    q_ref[...] = (z[:, 0:W_ATT] * Q_SCALE).reshape(bb, tm, W_ATT).astype(BF16)
    k_ref[...] = z[:, W_ATT:2 * W_ATT].reshape(bb, tm, W_ATT)
    v_ref[...] = z[:, 2 * W_ATT:3 * W_ATT].reshape(bb, tm, W_ATT)
    o_x = 3 * W_ATT
    o_g = o_x + W_REC
    o_f = o_g + W_REC
    xb = z[:, o_x:o_g].reshape(bb, tm, W_REC)
    gb = z[:, o_g:o_f]
    zf = z[:, o_f:o_f + LANES] + bf_ref[...]
    lf = jnp.minimum(zf, 0.0) - jnp.log1p(jnp.exp(-jnp.abs(zf)))
    lf_ref[0, 0] = lf.T[0:N_HEADS, :]

    xbuf[:, SUBLANES:SUBLANES + tm, :] = xb
    cw = cw_ref[...]
    xc = cb_ref[...] + cw[3:4, :] * xb
    for j in range(CONV_W - 1):
        off = SUBLANES - (CONV_W - 1) + j
        xc = xc + cw[j:j + 1, :] * xbuf[:, off:off + tm, :]
    tail = xbuf[:, tm:tm + SUBLANES, :]
    ctail_ref[...] = tail
    xbuf[:, 0:SUBLANES, :] = tail

    xc2 = xc.reshape(rows, W_REC)
    xcb = xc2.astype(BF16)
    r = jax.nn.sigmoid(_dot(xcb, wa_ref[...]) + ba_ref[...])
    i = jax.nn.sigmoid(_dot(xcb, wx_ref[...]) + bx_ref[...])
    nlam = -lam_ref[...]
    softplus = jnp.maximum(nlam, 0.0) + jnp.log1p(jnp.exp(-jnp.abs(nlam)))
    log_a = (-RG_C * softplus) * r
    t = jnp.tanh(log_a)
    neg_expm1 = -2.0 * t / (1.0 - t)
    a = jnp.exp(log_a)
    uu = jnp.sqrt(neg_expm1) * (i * xc2)
    n_lane_tiles = W_REC // LANES
    for c in range(n_lane_tiles):
        a_scr[c] = a[:, c * LANES:(c + 1) * LANES]
        u_scr[c] = uu[:, c * LANES:(c + 1) * LANES]

    for c in range(n_lane_tiles):
        h = h_scr[:, c * LANES:(c + 1) * LANES]
        for step in range(tm):
            rows_t = pl.ds(step, bb, stride=tm)
            h = a_scr[c, rows_t, :] * h + u_scr[c, rows_t, :]
            hs_scr[c, rows_t, :] = h
        h_scr[:, c * LANES:(c + 1) * LANES] = h
    ht_ref[...] = h_scr[...]

    gelu = 0.5 * gb * (1.0 + jnp.tanh(math.sqrt(2.0 / math.pi) * (gb + 0.044715 * (gb * gb * gb))))
    y = gelu * jnp.concatenate([hs_scr[c] for c in range(n_lane_tiles)], axis=1)
    rec_ref[...] = (_rms(y) * grec_ref[...]).reshape(bb, tm, W_REC).astype(BF16)


def _inproj(x, g1, sh1, n1, w_all, b_f, conv_w, conv_b, wa_bd, b_a, wx_bd, b_x, lam, g_rec, conv_past, h0):
    b, t, _ = x.shape
    bb, tm = BATCH_ROWS, SEQ_ROWS
    nb, nt = b // bb, t // tm
    rows = bb * tm
    wcols = w_all.shape[1]
    const2 = lambda shape: pl.BlockSpec(shape, lambda i, j: (0, 0))
    per_b = lambda last: pl.BlockSpec((bb, 1, last), lambda i, j: (i, 0, 0))
    seq = lambda last: pl.BlockSpec((bb, tm, last), lambda i, j: (i, j, 0))
    return pl.pallas_call(
        _inproj_kernel,
        grid=(nb, nt),
        in_specs=[seq(D_MODEL), per_b(D_MODEL), per_b(D_MODEL), const2((1, D_MODEL)),
                  const2((D_MODEL, wcols)), const2((1, LANES)),
                  const2((CONV_W, W_REC)), const2((1, W_REC)),
                  const2((W_REC, W_REC)), const2((1, W_REC)), const2((W_REC, W_REC)), const2((1, W_REC)),
                  const2((1, W_REC)), const2((1, W_REC)),
                  pl.BlockSpec((bb, SUBLANES, W_REC), lambda i, j: (i, 0, 0)),
                  pl.BlockSpec((bb, W_REC), lambda i, j: (i, 0))],
        out_specs=[seq(W_ATT), seq(W_ATT), seq(W_ATT),
                   pl.BlockSpec((1, 1, N_HEADS, rows), lambda i, j: (i, j, 0, 0)),
                   seq(W_REC),
                   pl.BlockSpec((bb, W_REC), lambda i, j: (i, 0)),
                   pl.BlockSpec((bb, SUBLANES, W_REC), lambda i, j: (i, 0, 0))],
        out_shape=[jax.ShapeDtypeStruct((b, t, W_ATT), BF16),
                   jax.ShapeDtypeStruct((b, t, W_ATT), F32),
                   jax.ShapeDtypeStruct((b, t, W_ATT), F32),
                   jax.ShapeDtypeStruct((nb, nt, N_HEADS, rows), F32),
                   jax.ShapeDtypeStruct((b, t, W_REC), BF16),
                   jax.ShapeDtypeStruct((b, W_REC), F32),
                   jax.ShapeDtypeStruct((b, SUBLANES, W_REC), F32)],
        scratch_shapes=[pltpu.VMEM((bb, tm + SUBLANES, W_REC), F32),
                        pltpu.VMEM((W_REC // LANES, rows, LANES), F32),
                        pltpu.VMEM((W_REC // LANES, rows, LANES), F32),
                        pltpu.VMEM((W_REC // LANES, rows, LANES), F32),
                        pltpu.VMEM((bb, W_REC), F32)],
        compiler_params=pltpu.CompilerParams(dimension_semantics=("arbitrary", "arbitrary"),
                                             vmem_limit_bytes=VMEM_LIMIT),
        name="inproj",
    )(x, g1, sh1, n1, w_all, b_f, conv_w, conv_b, wa_bd, b_a, wx_bd, b_x, lam, g_rec, conv_past, h0)


def _cumsum_kernel(x_ref, o_ref):
    length = x_ref.shape[2]
    c = CUMSUM_CHUNK
    upper = (lax.broadcasted_iota(jnp.int32, (c, c), 0) <= lax.broadcasted_iota(jnp.int32, (c, c), 1)).astype(BF16)
    carry = jnp.zeros((N_HEADS, 1), F32)
    for j in range(length // c):
        x = x_ref[0, :, j * c:(j + 1) * c]
        hi, mid, lo = _split3(x)
        parts = jnp.concatenate([hi.astype(F32), mid.astype(F32), lo.astype(F32), jnp.zeros_like(x)], axis=0)
        sums = _dot(parts.astype(BF16), upper)
        out = sums[0:8] + sums[8:16] + sums[16:24] + carry
        o_ref[0, :, j * c:(j + 1) * c] = out
        carry = out[:, c - 1:c]


def _cumsum_time(lf_t):
    b, h, length = lf_t.shape
    return pl.pallas_call(
        _cumsum_kernel,
        grid=(b,),
        in_specs=[pl.BlockSpec((1, h, length), lambda i: (i, 0, 0))],
        out_specs=pl.BlockSpec((1, h, length), lambda i: (i, 0, 0)),
        out_shape=jax.ShapeDtypeStruct((b, h, length), F32),
        name="cumsum",
    )(lf_t)


def _attn_tile(qh, kt, vt, bias, state, mask):
    m, l, acc = state
    s = lax.dot_general(qh, kt, (((1,), (1,)), ((), ())), preferred_element_type=F32) + bias
    if mask is not None:
        s = jnp.where(mask, s, NEG_BIG)
    m_new = jnp.maximum(m, jnp.max(s, axis=1, keepdims=True))
    alpha = jnp.exp2(m - m_new)
    p = jnp.exp2(s - m_new)
    l = alpha * l + jnp.sum(p, axis=1, keepdims=True)
    acc = alpha * acc + _dot(p.astype(BF16), vt)
    return m_new, l, acc


def _attn_kernel(*refs, n_past_tiles, past_tile, tq):
    if n_past_tiles:
        q_ref, kn_ref, vn_ref, fn_ref, kp_ref, vp_ref, fp_ref, o_ref, kb, vb = refs
        past = n_past_tiles * past_tile
    else:
        q_ref, kn_ref, vn_ref, fn_ref, o_ref, kb, vb = refs
        past = 0
    qi = pl.program_id(2)
    t_new = kn_ref.shape[1]

    @pl.when(qi == 0)
    def _():
        if n_past_tiles:
            kb[0:past, :] = kp_ref[0].astype(BF16)
            vb[0:past, :] = vp_ref[0].astype(BF16)
        kb[past:past + t_new, :] = kn_ref[0].astype(BF16)
        vb[past:past + t_new, :] = vn_ref[0].astype(BF16)

    q = q_ref[0]
    lane = lax.broadcasted_iota(jnp.int32, q.shape, 1)
    first_head = lane < HEAD_DIM
    qh = (jnp.where(first_head, q, jnp.zeros_like(q)), jnp.where(first_head, jnp.zeros_like(q), q))

    f_diag = fn_ref[0, 0, qi]
    ref_f = f_diag[:, 0:1]

    def bias_of(f_tile, hd):
        return (ref_f[hd:hd + 1, :] - f_tile[hd:hd + 1, :]) * LOG2E

    init = (jnp.full((tq, 1), NEG_BIG, F32), jnp.zeros((tq, 1), F32), jnp.zeros((tq, LANES), F32))
    state = (init, init)

    for j in range(n_past_tiles):
        kt = kb[j * past_tile:(j + 1) * past_tile, :]
        vt = vb[j * past_tile:(j + 1) * past_tile, :]
        f_tile = fp_ref[0, 0, j]
        state = tuple(_attn_tile(qh[hd], kt, vt, bias_of(f_tile, hd), state[hd], None) for hd in range(2))

    def body(j, st):
        start = pl.multiple_of(past + j * tq, tq)
        kt = kb[pl.ds(start, tq), :]
        vt = vb[pl.ds(start, tq), :]
        f_tile = fn_ref[0, 0, j]
        return tuple(_attn_tile(qh[hd], kt, vt, bias_of(f_tile, hd), st[hd], None) for hd in range(2))

    state = lax.fori_loop(0, qi, body, state)

    start = pl.multiple_of(past + qi * tq, tq)
    kt = kb[pl.ds(start, tq), :]
    vt = vb[pl.ds(start, tq), :]
    causal = lax.broadcasted_iota(jnp.int32, (tq, tq), 1) <= lax.broadcasted_iota(jnp.int32, (tq, tq), 0)
    state = tuple(_attn_tile(qh[hd], kt, vt, bias_of(f_diag, hd), state[hd], causal) for hd in range(2))

    outs = [acc / l for (_, l, acc) in state]
    o_ref[0] = jnp.where(first_head, outs[0], outs[1]).astype(BF16)


def _attention(q, k_new, v_new, f_new, k_past=None, v_past=None, f_past=None):
    b, t, _ = q.shape
    tq = min(t, ATTN_TILE)
    nq = t // tq
    pairs = N_HEADS // 2
    q_spec = pl.BlockSpec((1, tq, LANES), lambda i, p, j: (i, j, p))
    full = lambda rows: pl.BlockSpec((1, rows, LANES), lambda i, p, j: (i, 0, p))
    f_spec = lambda arr: pl.BlockSpec((1, 1) + arr.shape[2:], lambda i, p, j: (i, p, 0, 0, 0))
    in_specs = [q_spec, full(t), full(t), f_spec(f_new)]
    args = [q, k_new, v_new, f_new]
    past = 0
    n_past_tiles = 0
    past_tile = 0
    if k_past is not None:
        past = k_past.shape[1]
        past_tile = f_past.shape[-1]
        n_past_tiles = past // past_tile
        in_specs += [full(past), full(past), f_spec(f_past)]
        args += [k_past, v_past, f_past]
    return pl.pallas_call(
        functools.partial(_attn_kernel, n_past_tiles=n_past_tiles, past_tile=past_tile, tq=tq),
        grid=(b, pairs, nq),
        in_specs=in_specs,
        out_specs=q_spec,
        out_shape=jax.ShapeDtypeStruct((b, t, W_ATT), BF16),
        scratch_shapes=[pltpu.VMEM((past + t, LANES), BF16), pltpu.VMEM((past + t, LANES), BF16)],
        compiler_params=pltpu.CompilerParams(dimension_semantics=("arbitrary", "arbitrary", "arbitrary"),
                                             vmem_limit_bytes=VMEM_LIMIT),
        name="attn",
    )(*args)


def _outproj_kernel(attn_ref, rec_ref, x_ref, ga1_ref, g2_ref, sh2_ref, n2_ref, gatt_ref, woa_ref, wor_ref,
                    wrh_ref, wrm_ref, br_ref,
                    x1_ref, u2p_ref, route_ref):
    bb, tm, _ = x_ref.shape
    rows = bb * tm
    attn = attn_ref[...].astype(F32)
    an = (_rms(attn) * gatt_ref[...]).reshape(rows, W_ATT).astype(BF16)
    mix = _dot(an, woa_ref[...]) + _dot(rec_ref[...].reshape(rows, W_REC), wor_ref[...])
    x1 = x_ref[...] + ga1_ref[...] * mix.reshape(bb, tm, D_MODEL)
    x1_ref[...] = x1
    u2 = (_rms(x1) * (n2_ref[...] * g2_ref[...]) + sh2_ref[...]).reshape(rows, D_MODEL)
    for i, piece in enumerate(_pack_pieces(u2)):
        u2p_ref[i] = piece.reshape(bb, tm, PIECE_COLS)

    u_hi, u_mid, _ = _split3(u2)
    logits = _dot(u_hi, wrh_ref[...]) + _dot(u_mid, wrh_ref[...]) + _dot(u_hi, wrm_ref[...]) + br_ref[...]
    lt = logits.T
    row8 = lax.broadcasted_iota(jnp.int32, (SUBLANES, rows), 0).astype(F32)
    lg = lt[0:SUBLANES]
    m_g = jnp.max(lg, axis=0, keepdims=True)
    gidx = jnp.min(jnp.where(lg == m_g, row8, float(SUBLANES)), axis=0, keepdims=True)
    p_top = 1.0 / jnp.sum(jnp.exp(lg - m_g), axis=0, keepdims=True)
    leg = jnp.zeros((EXPERTS_PER_GROUP, rows), F32)
    for g in range(N_GROUPS):
        lo = EXPERT_COL0 + g * EXPERTS_PER_GROUP
        leg = jnp.where(gidx == float(g), lt[lo:lo + EXPERTS_PER_GROUP], leg)
    v1 = jnp.max(leg, axis=0, keepdims=True)
    i1 = jnp.min(jnp.where(leg == v1, row8, float(SUBLANES)), axis=0, keepdims=True)
    leg2 = jnp.where(row8 == i1, -jnp.inf, leg)
    v2 = jnp.max(leg2, axis=0, keepdims=True)
    i2 = jnp.min(jnp.where(leg2 == v2, row8, float(SUBLANES)), axis=0, keepdims=True)
    e21 = jnp.exp(v2 - v1)
    w1 = p_top / (1.0 + e21)
    w2 = w1 * e21
    base = gidx * float(EXPERTS_PER_GROUP)
    out = jnp.where(row8 == 3.0, w2, 0.0)
    for r_idx, val in ((2.0, w1), (1.0, base + i2), (0.0, base + i1)):
        out = jnp.where(row8 == r_idx, val, out)
    route_ref[0, 0] = out


def _outproj(attn, rec, x, ga1, g2, sh2, n2, g_att, wo_a, wo_r, wr_hi, wr_mid, b_r, bb, tm):
    b, t, _ = x.shape
    nb, nt = b // bb, t // tm
    rows = bb * tm
    const2 = lambda shape: pl.BlockSpec(shape, lambda i, j: (0, 0))
    per_b = pl.BlockSpec((bb, 1, D_MODEL), lambda i, j: (i, 0, 0))
    seq = lambda last: pl.BlockSpec((bb, tm, last), lambda i, j: (i, j, 0))
    return pl.pallas_call(
        _outproj_kernel,
        grid=(nb, nt),
        in_specs=[seq(W_ATT), seq(W_REC), seq(D_MODEL), per_b, per_b, per_b, const2((1, D_MODEL)),
                  const2((1, W_ATT)), const2((W_ATT, D_MODEL)), const2((W_REC, D_MODEL)),
                  const2((D_MODEL, ROUTE_COLS)), const2((D_MODEL, ROUTE_COLS)), const2((1, ROUTE_COLS))],
        out_specs=[seq(D_MODEL),
                   pl.BlockSpec((N_PIECES, bb, tm, PIECE_COLS), lambda i, j: (0, i, j, 0)),
                   pl.BlockSpec((1, 1, SUBLANES, rows), lambda i, j: (i, j, 0, 0))],
        out_shape=[jax.ShapeDtypeStruct((b, t, D_MODEL), F32),
                   jax.ShapeDtypeStruct((N_PIECES, b, t, PIECE_COLS), U32),
                   jax.ShapeDtypeStruct((nb, nt, SUBLANES, rows), F32)],
        compiler_params=pltpu.CompilerParams(dimension_semantics=("arbitrary", "arbitrary"),
                                             vmem_limit_bytes=VMEM_LIMIT),
        name="outproj",
    )(attn, rec, x, ga1, g2, sh2, n2, g_att, wo_a, wo_r, wr_hi, wr_mid, b_r)


_SC_AXES = ("core", "subcore")


def _sc_mesh():
    return plsc.VectorSubcoreMesh(core_axis_name=_SC_AXES[0], subcore_axis_name=_SC_AXES[1])


def _sc_scatter_rows(src_hbm, idx_hbm, n, dst_hbm):
    def body(x_vmem, i_vmem):
        pltpu.sync_copy(x_vmem, dst_hbm.at[i_vmem.at[0]])

    pltpu.emit_pipeline(
        body,
        grid=(n // GATHER_WINDOW,),
        in_specs=[pl.BlockSpec((GATHER_WINDOW, PIECE_COLS), lambda i: (i, 0)),
                  pl.BlockSpec((1, GATHER_WINDOW), lambda i: (0, i))],
        out_specs=[],
        core_axis_name=_SC_AXES,
        dimension_semantics=(pltpu.PARALLEL,),
    )(src_hbm, idx_hbm)


def _sc_gather_rows(tab_hbm, idx_hbm, n, dst_hbm):
    def body(i_vmem, o_vmem):
        pltpu.sync_copy(tab_hbm.at[i_vmem.at[0]], o_vmem)

    pltpu.emit_pipeline(
        body,
        grid=(n // GATHER_WINDOW,),
        in_specs=[pl.BlockSpec((1, GATHER_WINDOW), lambda i: (0, i))],
        out_specs=[pl.BlockSpec((GATHER_WINDOW, PIECE_COLS), lambda i: (i, 0))],
        core_axis_name=_SC_AXES,
        dimension_semantics=(pltpu.PARALLEL,),
    )(idx_hbm, dst_hbm)


def _dispatch(tables, dests, dest_pad, n_rows):
    n_pad = dest_pad.shape[1]
    sizes = [t.shape[1] for t in tables]
    n_groups = len(tables)

    @pl.kernel(out_type=jax.ShapeDtypeStruct((N_PIECES, n_rows, PIECE_COLS), U32), mesh=_sc_mesh(), scratch_types=[])
    def scatter(*refs):
        tabs = refs[:n_groups]
        idxs = refs[n_groups:3 * n_groups]
        pad_hbm, o_hbm = refs[3 * n_groups], refs[3 * n_groups + 1]
        for piece in range(N_PIECES):
            dst = o_hbm.at[piece]
            for g in range(n_groups):
                for k in range(2):
                    _sc_scatter_rows(tabs[g].at[piece], idxs[2 * g + k], sizes[g], dst)
            _sc_scatter_rows(tabs[0].at[piece], pad_hbm, n_pad, dst)

    flat_idx = [d for pair in dests for d in pair]
    return scatter(*tables, *flat_idx, dest_pad)


def _collect(yo, dests):
    sizes = [pair[0].shape[1] for pair in dests]
    n_groups = len(dests)
    out_type = [jax.ShapeDtypeStruct((2, N_PIECES, n, PIECE_COLS), U32) for n in sizes]

    @pl.kernel(out_type=out_type, mesh=_sc_mesh(), scratch_types=[])
    def gather(*refs):
        yo_hbm = refs[0]
        idxs = refs[1:1 + 2 * n_groups]
        outs = refs[1 + 2 * n_groups:]
        for piece in range(N_PIECES):
            for g in range(n_groups):
                for k in range(2):
                    _sc_gather_rows(yo_hbm.at[piece], idxs[2 * g + k], sizes[g], outs[g].at[k, piece])

    flat_idx = [d for pair in dests for d in pair]
    return gather(yo, *flat_idx)


def _expert_kernel(blk_e_ref, blk_first_ref, n_used_ref, xs_ref, wg_ref, wu_ref, wd_ref, yo_ref, wg_b, wu_b, wd_b):
    b = pl.program_id(0)

    @pl.when(blk_first_ref[b] == 1)
    def _():
        wg_b[...] = wg_ref[0].astype(BF16)
        wu_b[...] = wu_ref[0].astype(BF16)
        wd_b[...] = wd_ref[0].astype(BF16)

    @pl.when(b < n_used_ref[0])
    def _():
        chunks = [c.astype(BF16) for c in _unpack_pieces([xs_ref[i] for i in range(N_PIECES)])]
        g = None
        u = None
        for i, xc in enumerate(chunks):
            rows_i = slice(i * PIECE_COLS, (i + 1) * PIECE_COLS)
            gi = _dot(xc, wg_b[rows_i, :])
            ui = _dot(xc, wu_b[rows_i, :])
            g = gi if g is None else g + gi
            u = ui if u is None else u + ui
        h = (g * jax.nn.sigmoid(g)) * u
        for i, piece in enumerate(_pack_pieces(_dot(h.astype(BF16), wd_b[...]))):
            yo_ref[i] = piece

    @pl.when(b >= n_used_ref[0])
    def _():
        yo_ref[...] = jnp.zeros(yo_ref.shape, U32)


def _experts(xs, blk_e, blk_first, n_used, w_gate, w_up, w_down):
    p = xs.shape[1]
    tb = EXPERT_BLOCK
    nblk = p // tb
    row_block = pl.BlockSpec((N_PIECES, tb, PIECE_COLS), lambda i, e, f, n: (0, i, 0))
    grid_spec = pltpu.PrefetchScalarGridSpec(
        num_scalar_prefetch=3,
        grid=(nblk,),
        in_specs=[row_block,
                  pl.BlockSpec((1, D_MODEL, D_EXPERT), lambda i, e, f, n: (e[i], 0, 0)),
                  pl.BlockSpec((1, D_MODEL, D_EXPERT), lambda i, e, f, n: (e[i], 0, 0)),
                  pl.BlockSpec((1, D_EXPERT, D_MODEL), lambda i, e, f, n: (e[i], 0, 0))],
        out_specs=row_block,
        scratch_shapes=[pltpu.VMEM((D_MODEL, D_EXPERT), BF16),
                        pltpu.VMEM((D_MODEL, D_EXPERT), BF16),
                        pltpu.VMEM((D_EXPERT, D_MODEL), BF16)],
    )
    return pl.pallas_call(
        _expert_kernel,
        grid_spec=grid_spec,
        out_shape=jax.ShapeDtypeStruct((N_PIECES, p, PIECE_COLS), U32),
        compiler_params=pltpu.CompilerParams(dimension_semantics=("arbitrary",), vmem_limit_bytes=VMEM_LIMIT),
        name="experts",
    )(blk_e, blk_first, n_used, xs, w_gate, w_up, w_down)


def _combine_kernel(x1_ref, yg_ref, w_ref, ga2_ref, fg_ref, o_ref):
    bb, tm, _ = x1_ref.shape
    w = w_ref[...]
    y = None
    for k in range(2):
        chunks = _unpack_pieces([yg_ref[k, i] for i in range(N_PIECES)])
        yk = w[:, k:k + 1] * jnp.concatenate(chunks, axis=1)
        y = yk if y is None else y + yk
    out = x1_ref[...] + ga2_ref[...] * y.reshape(bb, tm, D_MODEL)
    o_ref[...] = _rms(out) * fg_ref[...]


def _combine(x1, yg, wts, ga2, final_g, bb, tm):
    b, t, _ = x1.shape
    nb, nt = b // bb, t // tm
    rows = bb * tm
    return pl.pallas_call(
        _combine_kernel,
        grid=(nb, nt),
        in_specs=[pl.BlockSpec((bb, tm, D_MODEL), lambda i, j: (i, j, 0)),
                  pl.BlockSpec((2, N_PIECES, rows, PIECE_COLS), lambda i, j: (0, 0, i * nt + j, 0)),
                  pl.BlockSpec((rows, 2), lambda i, j: (i * nt + j, 0)),
                  pl.BlockSpec((bb, 1, D_MODEL), lambda i, j: (i, 0, 0)),
                  pl.BlockSpec((1, D_MODEL), lambda i, j: (0, 0))],
        out_specs=pl.BlockSpec((bb, tm, D_MODEL), lambda i, j: (i, j, 0)),
        out_shape=jax.ShapeDtypeStruct((b, t, D_MODEL), F32),
        compiler_params=pltpu.CompilerParams(dimension_semantics=("arbitrary", "arbitrary"),
                                             vmem_limit_bytes=VMEM_LIMIT),
        name="combine",
    )(x1, yg, wts, ga2, final_g)


def _block_diag(w):
    n, k, _ = w.shape
    eye = jnp.eye(n, dtype=w.dtype)
    return (eye[:, None, :, None] * w[:, :, None, :]).reshape(n * k, n * k)


def _tiles(f, tile):
    b, _, length = f.shape
    return f.reshape(b, N_HEADS // 2, 2, length // tile, tile).transpose(0, 1, 3, 2, 4)


def _untile_rows(a, b, t):
    nb, nt, r, rows = a.shape
    bb = b // nb
    tm = t // nt
    return a.reshape(nb, nt, r, bb, tm).transpose(2, 0, 3, 1, 4).reshape(r, b, t)


def _group_front(x, c_mod, layer, conv_past, h0, cache):
    b, t, _ = x.shape
    sh1, sc1, ga1, sh2, sc2, ga2 = [m.reshape(b, 1, D_MODEL) for m in jnp.split(c_mod, 6, axis=-1)]
    q, k, v, lf_steps, rec, h_t, ctail = _inproj(
        x, 1.0 + sc1, sh1, layer["n1"], layer["w_all"], layer["b_f"], layer["conv_w"], layer["conv_b"],
        layer["wa_bd"], layer["b_a"], layer["wx_bd"], layer["b_x"], layer["lam"], layer["g_rec"], conv_past, h0)
    lf_t = _untile_rows(lf_steps, b, t).transpose(1, 0, 2)
    tq = min(t, ATTN_TILE)
    if cache is None:
        f_all = _cumsum_time(lf_t)
        attn = _attention(q, k, v, _tiles(f_all, tq))
    else:
        k_past, v_past, lf_past = cache
        past = k_past.shape[1]
        total = past + t
        padded = -(-total // CUMSUM_CHUNK) * CUMSUM_CHUNK
        lf_all = jnp.concatenate([lf_past.transpose(0, 2, 1), lf_t,
                                  jnp.zeros((b, N_HEADS, padded - total), F32)], axis=2)
        f_all = _cumsum_time(lf_all)
        attn = _attention(q, k, v, _tiles(f_all[:, :, past:total], tq),
                          k_past.reshape(b, past, W_ATT), v_past.reshape(b, past, W_ATT),
                          _tiles(f_all[:, :, :past], ATTN_TILE))
    bb, tm = (1, ROW_TILE) if t >= ROW_TILE else (ROW_TILE // t, t)
    x1, u2p, route = _outproj(attn, rec, x, ga1, 1.0 + sc2, sh2, layer["n2"], layer["g_att"],
                              layer["wo_a"], layer["wo_r"], layer["wr_hi"], layer["wr_mid"], layer["b_r"], bb, tm)
    route = _untile_rows(route, b, t).reshape(SUBLANES, b * t)
    leaves = (k.reshape(1, b, t, N_HEADS, HEAD_DIM), v.reshape(1, b, t, N_HEADS, HEAD_DIM),
              lf_t.transpose(0, 2, 1)[None], h_t[None], ctail[None, :, SUBLANES - (CONV_W - 1):, :])
    return x1, u2p.reshape(N_PIECES, b * t, PIECE_COLS), route, ga2, (bb, tm), leaves


def kernel(x_prompt, x_sample, c_prompt, c_sample, cache_k, cache_v, cache_logf, state_h, state_conv, norm1_g, norm2_g, w_ada, b_ada, w_in, b_f, conv_w, conv_b, w_rg_a, b_rg_a, w_rg_x, b_rg_x, rg_lambda, out_g_att, out_g_rec, w_out, w_route_group, b_route_group, w_route_expert, b_route_expert, w_exp_gate, w_exp_up, w_exp_down, final_g):
    bp, tp, _ = x_prompt.shape
    bs, ts, _ = x_sample.shape
    l = 0
    o1, o2, o3 = W_ATT, 2 * W_ATT, 3 * W_ATT
    o4 = o3 + N_HEADS
    o5 = o4 + W_REC
    w_in_l = w_in[l]
    w_all = jnp.concatenate([w_in_l[:, :o3], w_in_l[:, o4:], w_in_l[:, o3:o4],
                             jnp.zeros((D_MODEL, LANES - N_HEADS), F32)], axis=1).astype(BF16)
    w_r = jnp.zeros((D_MODEL, ROUTE_COLS), F32)
    w_r = w_r.at[:, 0:N_GROUPS].set(w_route_group[l]).at[:, EXPERT_COL0:EXPERT_COL0 + N_EXPERTS].set(w_route_expert[l])
    wr_hi = w_r.astype(BF16)
    wr_mid = (w_r - wr_hi.astype(F32)).astype(BF16)
    b_r = jnp.zeros((1, ROUTE_COLS), F32)
    b_r = b_r.at[0, 0:N_GROUPS].set(b_route_group[l]).at[0, N_GROUPS:SUBLANES].set(NEG_BIG)
    b_r = b_r.at[0, EXPERT_COL0:EXPERT_COL0 + N_EXPERTS].set(b_route_expert[l])
    layer = {
        "n1": norm1_g[l].reshape(1, D_MODEL), "n2": norm2_g[l].reshape(1, D_MODEL),
        "w_all": w_all,
        "b_f": jnp.concatenate([b_f[l], jnp.zeros((LANES - N_HEADS,), F32)]).reshape(1, LANES),
        "conv_w": conv_w[l], "conv_b": conv_b[l].reshape(1, W_REC),
        "wa_bd": _block_diag(w_rg_a[l]).astype(BF16), "b_a": b_rg_a[l].reshape(1, W_REC),
        "wx_bd": _block_diag(w_rg_x[l]).astype(BF16), "b_x": b_rg_x[l].reshape(1, W_REC),
        "lam": rg_lambda[l].reshape(1, W_REC), "g_rec": out_g_rec[l].reshape(1, W_REC),
        "g_att": out_g_att[l].reshape(1, W_ATT),
        "wo_a": w_out[l, :W_ATT].astype(BF16), "wo_r": w_out[l, W_ATT:].astype(BF16),
        "wr_hi": wr_hi, "wr_mid": wr_mid, "b_r": b_r,
    }

    ada = _ada(jnp.concatenate([c_prompt, c_sample], axis=0), w_ada[l], b_ada[l])
    pad_rows = SUBLANES - (CONV_W - 1)
    x1p, u2p_p, route_p, ga2p, tile_p, leaves_p = _group_front(
        x_prompt, ada[:bp], layer, jnp.zeros((bp, SUBLANES, W_REC), F32), jnp.zeros((bp, W_REC), F32), None)
    conv_past_s = jnp.concatenate([jnp.zeros((bs, pad_rows, W_REC), F32), state_conv[l]], axis=1)
    x1s, u2p_s, route_s, ga2s, tile_s, leaves_s = _group_front(
        x_sample, ada[bp:], layer, conv_past_s, state_h[l], (cache_k[l], cache_v[l], cache_logf[l]))

    n_p, n_s = bp * tp, bs * ts
    n_tok = n_p + n_s
    route = jnp.concatenate([route_p, route_s], axis=1)
    eid = route[0:2].astype(jnp.int32).reshape(-1)
    n_slot = 2 * n_tok
    tb = EXPERT_BLOCK
    experts = jnp.arange(N_EXPERTS, dtype=jnp.int32)
    onehot = (eid[:, None] == experts[None, :]).astype(jnp.int32)
    csum = jnp.cumsum(onehot, axis=0)
    counts = csum[-1]
    rank = jnp.sum(csum * onehot, axis=1) - 1
    padded = ((counts + tb - 1) // tb) * tb
    pend = jnp.cumsum(padded)
    pstart = pend - padded
    dest = jnp.sum(onehot * pstart[None, :], axis=1) + rank
    nblk = -(-n_slot // tb) + N_EXPERTS
    n_rows = nblk * tb
    blk_row0 = jnp.arange(nblk, dtype=jnp.int32) * tb
    blk_e = jnp.minimum(jnp.sum((pend[None, :] <= blk_row0[:, None]).astype(jnp.int32), axis=1), N_EXPERTS - 1)
    blk_first = jnp.concatenate([jnp.ones((1,), jnp.int32), (blk_e[1:] != blk_e[:-1]).astype(jnp.int32)])
    n_used = (pend[-1:] // tb).astype(jnp.int32)
    pad_e = padded - counts
    cpad = jnp.cumsum(pad_e)
    j = jnp.arange(n_rows - n_slot, dtype=jnp.int32)
    owner = (cpad[None, :] <= j[:, None]).astype(jnp.int32)
    e_j = jnp.minimum(jnp.sum(owner, axis=1), N_EXPERTS - 1)
    sel = (e_j[:, None] == experts[None, :]).astype(jnp.int32)
    in_expert = jnp.sum(sel * (pstart + counts - (cpad - pad_e))[None, :], axis=1) + j
    dest_pad = jnp.where(j < cpad[-1], in_expert, pend[-1] + (j - cpad[-1])).reshape(1, -1)

    dest = dest.reshape(2, n_tok)
    dests = [[dest[k, :n_p].reshape(1, n_p) for k in range(2)], [dest[k, n_p:].reshape(1, n_s) for k in range(2)]]
    xs = _dispatch([u2p_p, u2p_s], dests, dest_pad, n_rows)
    yo = _experts(xs, blk_e, blk_first, n_used, w_exp_gate[l], w_exp_up[l], w_exp_down[l])
    yg_p, yg_s = _collect(yo, dests)

    fg = final_g.reshape(1, D_MODEL)
    wts = route[2:4].T
    y_prompt = _combine(x1p, yg_p, wts[:n_p], ga2p, fg, *tile_p)
    y_sample = _combine(x1s, yg_s, wts[n_p:], ga2s, fg, *tile_s)
    return (y_prompt, y_sample) + leaves_p + leaves_s
```

```python
import functools
import math

import jax
import jax.numpy as jnp
from jax import lax
from jax.experimental import pallas as pl
from jax.experimental.pallas import tpu as pltpu
from jax.experimental.pallas import tpu_sc as plsc

F32 = jnp.float32
BF16 = jnp.bfloat16
U32 = jnp.uint32

D_MODEL = 1024
N_HEADS = 8
HEAD_DIM = 64
W_ATT = N_HEADS * HEAD_DIM
W_REC = D_MODEL - W_ATT
N_REC_BLOCKS = 8
REC_BLOCK = W_REC // N_REC_BLOCKS
CONV_W = 4
RG_C = 8.0
N_GROUPS = 4
EXPERTS_PER_GROUP = 8
N_EXPERTS = N_GROUPS * EXPERTS_PER_GROUP
D_EXPERT = 256
EPS = 1e-6

LANES = 128
SUBLANES = 8
LOG2E = math.log2(math.e)
Q_SCALE = LOG2E * HEAD_DIM ** -0.5
NEG_BIG = -1e30
VMEM_LIMIT = 48 * 1024 * 1024

SEQ_ROWS = 64
BATCH_ROWS = 8
INPROJ_GROUPS = 1
CUMSUM_CHUNK = 512
ATTN_TILE = 2048
ATTN_CHUNK = 512
ATTN_UNROLL = 2
PAST_TILE = 1024
ROW_TILE = 512
EXPERT_BLOCK = 256
GATHER_WINDOW = 128
PIECE_COLS = 256
N_PIECES = D_MODEL // (2 * PIECE_COLS)
ROUTE_COLS = 128
EXPERT_COL0 = 8


def _split3(x):
    hi = x.astype(BF16)
    r1 = x - hi.astype(F32)
    mid = r1.astype(BF16)
    lo = (r1 - mid.astype(F32)).astype(BF16)
    return hi, mid, lo


def _dot(a, b):
    return jnp.dot(a, b, preferred_element_type=F32)


def _rms(x):
    return x * lax.rsqrt(jnp.mean(x * x, axis=-1, keepdims=True) + EPS)


def _sigmoid(x):
    return 0.5 * jnp.tanh(0.5 * x) + 0.5


def _pack_bf16_pairs(y):
    n = y.shape[-1] // 2
    yb = y.astype(BF16).astype(F32)
    lo = pltpu.bitcast(yb[:, :n], U32)
    hi = pltpu.bitcast(yb[:, n:], U32)
    return (lo >> 16) | hi


def _unpack_bf16_pairs(p):
    lo = pltpu.bitcast(p << 16, F32)
    hi = pltpu.bitcast(p & jnp.uint32(0xFFFF0000), F32)
    return lo, hi


def _pack_pieces(y):
    w = 2 * PIECE_COLS
    return [_pack_bf16_pairs(y[:, i * w:(i + 1) * w]) for i in range(N_PIECES)]


def _unpack_pieces(pieces):
    chunks = []
    for p in pieces:
        chunks.extend(_unpack_bf16_pairs(p))
    return chunks


def _ada_kernel(c_ref, w_ref, b_ref, o_ref):
    c = c_ref[...]
    a = c * jax.nn.sigmoid(c)
    a_hi, a_mid, _ = _split3(a)
    w_hi, w_mid, _ = _split3(w_ref[...])
    o_ref[...] = _dot(a_hi, w_hi) + _dot(a_mid, w_hi) + _dot(a_hi, w_mid) + b_ref[...]


def _ada(c, w_ada, b_ada):
    rows = c.shape[0]
    n = w_ada.shape[1]
    tn = 1536
    return pl.pallas_call(
        _ada_kernel,
        grid=(n // tn,),
        in_specs=[pl.BlockSpec((rows, D_MODEL), lambda j: (0, 0)),
                  pl.BlockSpec((D_MODEL, tn), lambda j: (0, j)),
                  pl.BlockSpec((1, tn), lambda j: (0, j))],
        out_specs=pl.BlockSpec((rows, tn), lambda j: (0, j)),
        out_shape=jax.ShapeDtypeStruct((rows, n), F32),
        compiler_params=pltpu.CompilerParams(vmem_limit_bytes=VMEM_LIMIT),
        name="ada",
    )(c, w_ada, b_ada.reshape(1, n))


def _inproj_kernel(x_ref, g1_ref, sh1_ref, n1_ref, w_ref, bf_ref, cw_ref, cb_ref, wa_ref, ba_ref, wx_ref, bx_ref,
                   lam_ref, grec_ref, cpast_ref, h0_ref,
                   q_ref, k_ref, v_ref, lf_ref, rec_ref, ht_ref, ctail_ref,
                   xbuf, a_scr, u_scr, hs_scr, gl_scr, h_scr):
    bb, tm, _ = x_ref.shape
    n_lane_tiles = W_REC // LANES
    half = W_REC // 2

    @pl.when(pl.program_id(1) == 0)
    def _():
        xbuf[:, 0:SUBLANES, :] = cpast_ref[...]
        h_scr[...] = h0_ref[...]

    mod = n1_ref[...] * g1_ref[...]
    nlam = -lam_ref[...]
    softplus = jnp.maximum(nlam, 0.0) + jnp.log1p(jnp.exp(-jnp.abs(nlam)))
    cw = cw_ref[...]
    o_x = 3 * W_ATT
    o_g = o_x + W_REC
    o_f = o_g + W_REC

    gsz = bb // INPROJ_GROUPS
    rows_g = gsz * tm
    for g in range(INPROJ_GROUPS):
        bs = slice(g * gsz, (g + 1) * gsz)
        rs = slice(g * rows_g, (g + 1) * rows_g)
        u = _rms(x_ref[bs]) * mod[bs] + sh1_ref[bs]
        ub = u.reshape(rows_g, D_MODEL).astype(BF16)
        z = _dot(ub, w_ref[:, o_x:])
        xb = z[:, 0:W_REC].reshape(gsz, tm, W_REC)
        gb = z[:, W_REC:2 * W_REC]
        zf = z[:, 2 * W_REC:2 * W_REC + LANES] + bf_ref[...]
        lf = jnp.minimum(zf, 0.0) - jnp.log1p(jnp.exp(-jnp.abs(zf)))
        lf_ref[0, 0, :, rs] = lf.T[0:N_HEADS, :]

        xbuf[bs, SUBLANES:SUBLANES + tm, :] = xb
        xc = cb_ref[...] + cw[3:4, :] * xb
        for j in range(CONV_W - 1):
            off = SUBLANES - (CONV_W - 1) + j
            xc = xc + cw[j:j + 1, :] * xbuf[bs, off:off + tm, :]
        tail = xbuf[bs, tm:tm + SUBLANES, :]
        ctail_ref[bs] = tail
        xbuf[bs, 0:SUBLANES, :] = tail

        xc2 = xc.reshape(rows_g, W_REC)
        xcb = xc2.astype(BF16)
        pre_r = jnp.concatenate([_dot(xcb[:, :half], wa_ref[:half, :half]),
                                 _dot(xcb[:, half:], wa_ref[half:, half:])], axis=1)
        pre_i = jnp.concatenate([_dot(xcb[:, :half], wx_ref[:half, :half]),
                                 _dot(xcb[:, half:], wx_ref[half:, half:])], axis=1)
        r = _sigmoid(pre_r + ba_ref[...])
        i = _sigmoid(pre_i + bx_ref[...])
        log_a = (-RG_C * softplus) * r
        t = jnp.tanh(log_a)
        neg_expm1 = -2.0 * t / (1.0 - t)
        a = jnp.exp(log_a)
        root = jnp.where(neg_expm1 > 0.0, neg_expm1 * lax.rsqrt(neg_expm1), 0.0)
        uu = root * (i * xc2)
        for c in range(n_lane_tiles):
            a_scr[c, rs, :] = a[:, c * LANES:(c + 1) * LANES]
            u_scr[c, rs, :] = uu[:, c * LANES:(c + 1) * LANES]
        gl_scr[rs, :] = 0.5 * gb * (1.0 + jnp.tanh(math.sqrt(2.0 / math.pi) * (gb + 0.044715 * (gb * gb * gb))))

        zq = _dot(ub, w_ref[:, :o_x])
        q_ref[bs] = (zq[:, 0:W_ATT] * Q_SCALE).reshape(gsz, tm, W_ATT).astype(BF16)
        k_ref[bs] = zq[:, W_ATT:2 * W_ATT].reshape(gsz, tm, W_ATT)
        v_ref[bs] = zq[:, 2 * W_ATT:3 * W_ATT].reshape(gsz, tm, W_ATT)

    for c in range(n_lane_tiles):
        h = h_scr[:, c * LANES:(c + 1) * LANES]
        for step in range(tm):
            rows_t = pl.ds(step, bb, stride=tm)
            h = a_scr[c, rows_t, :] * h + u_scr[c, rows_t, :]
            hs_scr[c, rows_t, :] = h
        h_scr[:, c * LANES:(c + 1) * LANES] = h
    ht_ref[...] = h_scr[...]

    y = gl_scr[...] * jnp.concatenate([hs_scr[c] for c in range(n_lane_tiles)], axis=1)
    rec_ref[...] = (_rms(y) * grec_ref[...]).reshape(bb, tm, W_REC).astype(BF16)


def _inproj(x, g1, sh1, n1, w_all, b_f, conv_w, conv_b, wa_bd, b_a, wx_bd, b_x, lam, g_rec, conv_past, h0):
    b, t, _ = x.shape
    bb, tm = BATCH_ROWS, SEQ_ROWS
    nb, nt = b // bb, t // tm
    rows = bb * tm
    wcols = w_all.shape[1]
    const2 = lambda shape: pl.BlockSpec(shape, lambda i, j: (0, 0))
    per_b = lambda last: pl.BlockSpec((bb, 1, last), lambda i, j: (i, 0, 0))
    seq = lambda last: pl.BlockSpec((bb, tm, last), lambda i, j: (i, j, 0))
    return pl.pallas_call(
        _inproj_kernel,
        grid=(nb, nt),
        in_specs=[seq(D_MODEL), per_b(D_MODEL), per_b(D_MODEL), const2((1, D_MODEL)),
                  const2((D_MODEL, wcols)), const2((1, LANES)),
                  const2((CONV_W, W_REC)), const2((1, W_REC)),
                  const2((W_REC, W_REC)), const2((1, W_REC)), const2((W_REC, W_REC)), const2((1, W_REC)),
                  const2((1, W_REC)), const2((1, W_REC)),
                  pl.BlockSpec((bb, SUBLANES, W_REC), lambda i, j: (i, 0, 0)),
                  pl.BlockSpec((bb, W_REC), lambda i, j: (i, 0))],
        out_specs=[seq(W_ATT), seq(W_ATT), seq(W_ATT),
                   pl.BlockSpec((1, 1, N_HEADS, rows), lambda i, j: (i, j, 0, 0)),
                   seq(W_REC),
                   pl.BlockSpec((bb, W_REC), lambda i, j: (i, 0)),
                   pl.BlockSpec((bb, SUBLANES, W_REC), lambda i, j: (i, 0, 0))],
        out_shape=[jax.ShapeDtypeStruct((b, t, W_ATT), BF16),
                   jax.ShapeDtypeStruct((b, t, W_ATT), F32),
                   jax.ShapeDtypeStruct((b, t, W_ATT), F32),
                   jax.ShapeDtypeStruct((nb, nt, N_HEADS, rows), F32),
                   jax.ShapeDtypeStruct((b, t, W_REC), BF16),
                   jax.ShapeDtypeStruct((b, W_REC), F32),
                   jax.ShapeDtypeStruct((b, SUBLANES, W_REC), F32)],
        scratch_shapes=[pltpu.VMEM((bb, tm + SUBLANES, W_REC), F32),
                        pltpu.VMEM((W_REC // LANES, rows, LANES), F32),
                        pltpu.VMEM((W_REC // LANES, rows, LANES), F32),
                        pltpu.VMEM((W_REC // LANES, rows, LANES), F32),
                        pltpu.VMEM((rows, W_REC), F32),
                        pltpu.VMEM((bb, W_REC), F32)],
        compiler_params=pltpu.CompilerParams(dimension_semantics=("arbitrary", "arbitrary"),
                                             vmem_limit_bytes=VMEM_LIMIT),
        name="inproj",
    )(x, g1, sh1, n1, w_all, b_f, conv_w, conv_b, wa_bd, b_a, wx_bd, b_x, lam, g_rec, conv_past, h0)


def _cumsum_kernel(x_ref, o_ref):
    length = x_ref.shape[2]
    c = CUMSUM_CHUNK
    upper = (lax.broadcasted_iota(jnp.int32, (c, c), 0) <= lax.broadcasted_iota(jnp.int32, (c, c), 1)).astype(BF16)
    carry = jnp.zeros((N_HEADS, 1), F32)
    for j in range(length // c):
        x = x_ref[0, :, j * c:(j + 1) * c]
        hi, mid, lo = _split3(x)
        parts = jnp.concatenate([hi.astype(F32), mid.astype(F32), lo.astype(F32), jnp.zeros_like(x)], axis=0)
        sums = _dot(parts.astype(BF16), upper)
        out = sums[0:8] + sums[8:16] + sums[16:24] + carry
        o_ref[0, :, j * c:(j + 1) * c] = out
        carry = out[:, c - 1:c]


def _cumsum_time(lf_t):
    b, h, length = lf_t.shape
    return pl.pallas_call(
        _cumsum_kernel,
        grid=(b,),
        in_specs=[pl.BlockSpec((1, h, length), lambda i: (i, 0, 0))],
        out_specs=pl.BlockSpec((1, h, length), lambda i: (i, 0, 0)),
        out_shape=jax.ShapeDtypeStruct((b, h, length), F32),
        name="cumsum",
    )(lf_t)


def _attn_tile(qh, kt, vt, bias, m_ref, acc_ref, mask):
    s = lax.dot_general(qh, kt, (((1,), (1,)), ((), ())), preferred_element_type=F32) + bias
    if mask is not None:
        s = jnp.where(mask, s, NEG_BIG)
    width = min(LANES, s.shape[1])
    slabs = [s[:, c * width:(c + 1) * width] for c in range(s.shape[1] // width)]
    part = slabs[0]
    for sl in slabs[1:]:
        part = jnp.maximum(part, sl)
    m = m_ref[...]
    m_new = jnp.maximum(m, jnp.broadcast_to(jnp.max(part, axis=1, keepdims=True), m.shape))
    p = jnp.concatenate([jnp.exp2(sl - m_new[:, :width]) for sl in slabs], axis=1)
    acc_ref[...] = jnp.exp2(m - m_new) * acc_ref[...] + _dot(p.astype(BF16), vt)
    m_ref[...] = m_new


def _attn_kernel(*refs, n_past_tiles, past_tile, tq, ck):
    if n_past_tiles:
        q_ref, kn_ref, vn_ref, fn_ref, kp_ref, vp_ref, fp_ref, o_ref, kb, vb, m_scr, acc_scr = refs
        past = n_past_tiles * past_tile
    else:
        q_ref, kn_ref, vn_ref, fn_ref, o_ref, kb, vb, m_scr, acc_scr = refs
        past = 0
    qi = pl.program_id(2)
    t_new = kn_ref.shape[1]
    chunks_per_tile = tq // ck

    @pl.when(qi == 0)
    def _():
        def fill(lo, k_f32, v_f32):
            n = k_f32.shape[0]
            first = lax.broadcasted_iota(jnp.int32, (n, LANES), 1) < HEAD_DIM
            kb[lo:lo + n, :] = k_f32.astype(BF16)
            vb[0, lo:lo + n, :] = jnp.where(first, v_f32, 1.0).astype(BF16)
            vb[1, lo:lo + n, :] = jnp.where(first, 1.0, v_f32).astype(BF16)

        if n_past_tiles:
            fill(0, kp_ref[0], vp_ref[0])
        fill(past, kn_ref[0], vn_ref[0])

    q = q_ref[0]
    first_head = lax.broadcasted_iota(jnp.int32, q.shape, 1) < HEAD_DIM
    qh = (jnp.where(first_head, q, jnp.zeros_like(q)), jnp.where(first_head, jnp.zeros_like(q), q))

    ref_f = fn_ref[0, 0, qi * chunks_per_tile][:, 0:1]
    m_scr[...] = jnp.full(m_scr.shape, NEG_BIG, F32)
    acc_scr[...] = jnp.zeros(acc_scr.shape, F32)

    def tile(start, size, f_tile, row0=0, masked=False):
        kt = kb[pl.ds(start, size), :]
        mask = None
        if masked:
            shape = (tq - row0, size)
            mask = lax.broadcasted_iota(jnp.int32, shape, 1) <= lax.broadcasted_iota(jnp.int32, shape, 0)
        for hd in range(2):
            bias = (ref_f[hd:hd + 1, :] - f_tile[hd:hd + 1, :]) * LOG2E
            _attn_tile(qh[hd][row0:, :], kt, vb[hd, pl.ds(start, size), :], bias,
                       m_scr.at[hd, row0:tq], acc_scr.at[hd, row0:tq], mask)

    for j in range(n_past_tiles):
        tile(j * past_tile, past_tile, fp_ref[0, 0, j])

    if tq < t_new:
        n_chunks = qi * chunks_per_tile

        def chunk(j):
            tile(pl.multiple_of(past + j * ck, ck), ck, fn_ref[0, 0, j])

        def chunk_body(jj, carry):
            for u in range(ATTN_UNROLL):
                chunk(jj * ATTN_UNROLL + u)
            return carry

        lax.fori_loop(0, n_chunks // ATTN_UNROLL, chunk_body, 0)
        for r in range(ATTN_UNROLL - 1):
            @pl.when(r < n_chunks % ATTN_UNROLL)
            def _():
                chunk((n_chunks // ATTN_UNROLL) * ATTN_UNROLL + r)

    for c in range(chunks_per_tile):
        j = qi * chunks_per_tile + c
        tile(pl.multiple_of(past + j * ck, ck), ck, fn_ref[0, 0, j], row0=c * ck, masked=True)

    outs = []
    for hd in range(2):
        acc = acc_scr[hd]
        outs.append(acc / pltpu.roll(acc, HEAD_DIM, axis=1))
    o_ref[0] = jnp.where(first_head, outs[0], outs[1]).astype(BF16)


def _attention(q, k_new, v_new, f_new, k_past=None, v_past=None, f_past=None):
    b, t, _ = q.shape
    tq = min(t, ATTN_TILE)
    nq = t // tq
    pairs = N_HEADS // 2
    q_spec = pl.BlockSpec((1, tq, LANES), lambda i, p, j: (i, j, p))
    full = lambda rows: pl.BlockSpec((1, rows, LANES), lambda i, p, j: (i, 0, p))
    f_spec = lambda arr: pl.BlockSpec((1, 1) + arr.shape[2:], lambda i, p, j: (i, p, 0, 0, 0))
    in_specs = [q_spec, full(t), full(t), f_spec(f_new)]
    args = [q, k_new, v_new, f_new]
    past = 0
    n_past_tiles = 0
    past_tile = 0
    if k_past is not None:
        past = k_past.shape[1]
        past_tile = f_past.shape[-1]
        n_past_tiles = past // past_tile
        in_specs += [full(past), full(past), f_spec(f_past)]
        args += [k_past, v_past, f_past]
    ck = f_new.shape[-1]
    return pl.pallas_call(
        functools.partial(_attn_kernel, n_past_tiles=n_past_tiles, past_tile=past_tile, tq=tq, ck=ck),
        grid=(b, pairs, nq),
        in_specs=in_specs,
        out_specs=q_spec,
        out_shape=jax.ShapeDtypeStruct((b, t, W_ATT), BF16),
        scratch_shapes=[pltpu.VMEM((past + t, LANES), BF16), pltpu.VMEM((2, past + t, LANES), BF16),
                        pltpu.VMEM((2, tq, LANES), F32), pltpu.VMEM((2, tq, LANES), F32)],
        compiler_params=pltpu.CompilerParams(dimension_semantics=("arbitrary", "arbitrary", "arbitrary"),
                                             vmem_limit_bytes=VMEM_LIMIT),
        name="attn",
    )(*args)


def _outproj_kernel(attn_ref, rec_ref, x_ref, ga1_ref, g2_ref, sh2_ref, n2_ref, gatt_ref, woa_ref, wor_ref,
                    wrh_ref, wrm_ref, br_ref,
                    x1_ref, u2p_ref, route_ref):
    bb, tm, _ = x_ref.shape
    rows = bb * tm
    attn = attn_ref[...].astype(F32)
    an = (_rms(attn) * gatt_ref[...]).reshape(rows, W_ATT).astype(BF16)
    mix = _dot(an, woa_ref[...]) + _dot(rec_ref[...].reshape(rows, W_REC), wor_ref[...])
    x1 = x_ref[...] + ga1_ref[...] * mix.reshape(bb, tm, D_MODEL)
    x1_ref[...] = x1
    u2 = (_rms(x1) * (n2_ref[...] * g2_ref[...]) + sh2_ref[...]).reshape(rows, D_MODEL)
    for i, piece in enumerate(_pack_pieces(u2)):
        u2p_ref[i] = piece.reshape(bb, tm, PIECE_COLS)

    u_hi, u_mid, _ = _split3(u2)
    logits = _dot(u_hi, wrh_ref[...]) + _dot(u_mid, wrh_ref[...]) + _dot(u_hi, wrm_ref[...]) + br_ref[...]
    lt = logits.T
    row8 = lax.broadcasted_iota(jnp.int32, (SUBLANES, rows), 0).astype(F32)
    lg = lt[0:SUBLANES]
    m_g = jnp.max(lg, axis=0, keepdims=True)
    gidx = jnp.min(jnp.where(lg == m_g, row8, float(SUBLANES)), axis=0, keepdims=True)
    p_top = 1.0 / jnp.sum(jnp.exp(lg - m_g), axis=0, keepdims=True)
    leg = jnp.zeros((EXPERTS_PER_GROUP, rows), F32)
    for g in range(N_GROUPS):
        lo = EXPERT_COL0 + g * EXPERTS_PER_GROUP
        leg = jnp.where(gidx == float(g), lt[lo:lo + EXPERTS_PER_GROUP], leg)
    v1 = jnp.max(leg, axis=0, keepdims=True)
    i1 = jnp.min(jnp.where(leg == v1, row8, float(SUBLANES)), axis=0, keepdims=True)
    leg2 = jnp.where(row8 == i1, -jnp.inf, leg)
    v2 = jnp.max(leg2, axis=0, keepdims=True)
    i2 = jnp.min(jnp.where(leg2 == v2, row8, float(SUBLANES)), axis=0, keepdims=True)
    e21 = jnp.exp(v2 - v1)
    w1 = p_top / (1.0 + e21)
    w2 = w1 * e21
    base = gidx * float(EXPERTS_PER_GROUP)
    out = jnp.where(row8 == 3.0, w2, 0.0)
    for r_idx, val in ((2.0, w1), (1.0, base + i2), (0.0, base + i1)):
        out = jnp.where(row8 == r_idx, val, out)
    route_ref[0, 0] = out


def _outproj(attn, rec, x, ga1, g2, sh2, n2, g_att, wo_a, wo_r, wr_hi, wr_mid, b_r, bb, tm):
    b, t, _ = x.shape
    nb, nt = b // bb, t // tm
    rows = bb * tm
    const2 = lambda shape: pl.BlockSpec(shape, lambda i, j: (0, 0))
    per_b = pl.BlockSpec((bb, 1, D_MODEL), lambda i, j: (i, 0, 0))
    seq = lambda last: pl.BlockSpec((bb, tm, last), lambda i, j: (i, j, 0))
    return pl.pallas_call(
        _outproj_kernel,
        grid=(nb, nt),
        in_specs=[seq(W_ATT), seq(W_REC), seq(D_MODEL), per_b, per_b, per_b, const2((1, D_MODEL)),
                  const2((1, W_ATT)), const2((W_ATT, D_MODEL)), const2((W_REC, D_MODEL)),
                  const2((D_MODEL, ROUTE_COLS)), const2((D_MODEL, ROUTE_COLS)), const2((1, ROUTE_COLS))],
        out_specs=[seq(D_MODEL),
                   pl.BlockSpec((N_PIECES, bb, tm, PIECE_COLS), lambda i, j: (0, i, j, 0)),
                   pl.BlockSpec((1, 1, SUBLANES, rows), lambda i, j: (i, j, 0, 0))],
        out_shape=[jax.ShapeDtypeStruct((b, t, D_MODEL), F32),
                   jax.ShapeDtypeStruct((N_PIECES, b, t, PIECE_COLS), U32),
                   jax.ShapeDtypeStruct((nb, nt, SUBLANES, rows), F32)],
        compiler_params=pltpu.CompilerParams(dimension_semantics=("arbitrary", "arbitrary"),
                                             vmem_limit_bytes=VMEM_LIMIT),
        name="outproj",
    )(attn, rec, x, ga1, g2, sh2, n2, g_att, wo_a, wo_r, wr_hi, wr_mid, b_r)


_SC_AXES = ("core", "subcore")


def _sc_mesh():
    return plsc.VectorSubcoreMesh(core_axis_name=_SC_AXES[0], subcore_axis_name=_SC_AXES[1])


def _sc_scatter_rows(src_hbm, idx_hbm, n, dst_hbm):
    def body(x_vmem, i_vmem):
        pltpu.sync_copy(x_vmem, dst_hbm.at[i_vmem.at[0]])

    pltpu.emit_pipeline(
        body,
        grid=(n // GATHER_WINDOW,),
        in_specs=[pl.BlockSpec((GATHER_WINDOW, PIECE_COLS), lambda i: (i, 0)),
                  pl.BlockSpec((1, GATHER_WINDOW), lambda i: (0, i))],
        out_specs=[],
        core_axis_name=_SC_AXES,
        dimension_semantics=(pltpu.PARALLEL,),
    )(src_hbm, idx_hbm)


def _sc_gather_rows(tab_hbm, idx_hbm, n, dst_hbm):
    def body(i_vmem, o_vmem):
        pltpu.sync_copy(tab_hbm.at[i_vmem.at[0]], o_vmem)

    pltpu.emit_pipeline(
        body,
        grid=(n // GATHER_WINDOW,),
        in_specs=[pl.BlockSpec((1, GATHER_WINDOW), lambda i: (0, i))],
        out_specs=[pl.BlockSpec((GATHER_WINDOW, PIECE_COLS), lambda i: (i, 0))],
        core_axis_name=_SC_AXES,
        dimension_semantics=(pltpu.PARALLEL,),
    )(idx_hbm, dst_hbm)


def _dispatch(tables, dests, dest_pad, n_rows):
    n_pad = dest_pad.shape[1]
    sizes = [t.shape[1] for t in tables]
    n_groups = len(tables)

    @pl.kernel(out_type=jax.ShapeDtypeStruct((N_PIECES, n_rows, PIECE_COLS), U32), mesh=_sc_mesh(), scratch_types=[])
    def scatter(*refs):
        tabs = refs[:n_groups]
        idxs = refs[n_groups:3 * n_groups]
        pad_hbm, o_hbm = refs[3 * n_groups], refs[3 * n_groups + 1]
        for piece in range(N_PIECES):
            dst = o_hbm.at[piece]
            for g in range(n_groups):
                for k in range(2):
                    _sc_scatter_rows(tabs[g].at[piece], idxs[2 * g + k], sizes[g], dst)
            _sc_scatter_rows(tabs[0].at[piece], pad_hbm, n_pad, dst)

    flat_idx = [d for pair in dests for d in pair]
    return scatter(*tables, *flat_idx, dest_pad)


def _collect(yo, dests):
    sizes = [pair[0].shape[1] for pair in dests]
    n_groups = len(dests)
    out_type = [jax.ShapeDtypeStruct((2, N_PIECES, n, PIECE_COLS), U32) for n in sizes]

    @pl.kernel(out_type=out_type, mesh=_sc_mesh(), scratch_types=[])
    def gather(*refs):
        yo_hbm = refs[0]
        idxs = refs[1:1 + 2 * n_groups]
        outs = refs[1 + 2 * n_groups:]
        for piece in range(N_PIECES):
            for g in range(n_groups):
                for k in range(2):
                    _sc_gather_rows(yo_hbm.at[piece], idxs[2 * g + k], sizes[g], outs[g].at[k, piece])

    flat_idx = [d for pair in dests for d in pair]
    return gather(yo, *flat_idx)


def _expert_kernel(blk_e_ref, blk_first_ref, n_used_ref, xs_ref, wg_ref, wu_ref, wd_ref, yo_ref, wg_b, wu_b, wd_b):
    b = pl.program_id(0)

    @pl.when(blk_first_ref[b] == 1)
    def _():
        wg_b[...] = wg_ref[0].astype(BF16)
        wu_b[...] = wu_ref[0].astype(BF16)
        wd_b[...] = wd_ref[0].astype(BF16)

    @pl.when(b < n_used_ref[0])
    def _():
        chunks = [c.astype(BF16) for c in _unpack_pieces([xs_ref[i] for i in range(N_PIECES)])]
        g = None
        u = None
        for i, xc in enumerate(chunks):
            rows_i = slice(i * PIECE_COLS, (i + 1) * PIECE_COLS)
            gi = _dot(xc, wg_b[rows_i, :])
            ui = _dot(xc, wu_b[rows_i, :])
            g = gi if g is None else g + gi
            u = ui if u is None else u + ui
        h = (g * _sigmoid(g)) * u
        for i, piece in enumerate(_pack_pieces(_dot(h.astype(BF16), wd_b[...]))):
            yo_ref[i] = piece

    @pl.when(b >= n_used_ref[0])
    def _():
        yo_ref[...] = jnp.zeros(yo_ref.shape, U32)


def _experts(xs, blk_e, blk_first, n_used, w_gate, w_up, w_down):
    p = xs.shape[1]
    tb = EXPERT_BLOCK
    nblk = p // tb
    row_block = pl.BlockSpec((N_PIECES, tb, PIECE_COLS), lambda i, e, f, n: (0, i, 0))
    grid_spec = pltpu.PrefetchScalarGridSpec(
        num_scalar_prefetch=3,
        grid=(nblk,),
        in_specs=[row_block,
                  pl.BlockSpec((1, D_MODEL, D_EXPERT), lambda i, e, f, n: (e[i], 0, 0)),
                  pl.BlockSpec((1, D_MODEL, D_EXPERT), lambda i, e, f, n: (e[i], 0, 0)),
                  pl.BlockSpec((1, D_EXPERT, D_MODEL), lambda i, e, f, n: (e[i], 0, 0))],
        out_specs=row_block,
        scratch_shapes=[pltpu.VMEM((D_MODEL, D_EXPERT), BF16),
                        pltpu.VMEM((D_MODEL, D_EXPERT), BF16),
                        pltpu.VMEM((D_EXPERT, D_MODEL), BF16)],
    )
    return pl.pallas_call(
        _expert_kernel,
        grid_spec=grid_spec,
        out_shape=jax.ShapeDtypeStruct((N_PIECES, p, PIECE_COLS), U32),
        compiler_params=pltpu.CompilerParams(dimension_semantics=("arbitrary",), vmem_limit_bytes=VMEM_LIMIT),
        name="experts",
    )(blk_e, blk_first, n_used, xs, w_gate, w_up, w_down)


def _combine_kernel(x1_ref, yg_ref, w_ref, ga2_ref, fg_ref, o_ref):
    bb, tm, _ = x1_ref.shape
    w = w_ref[...]
    y = None
    for k in range(2):
        chunks = _unpack_pieces([yg_ref[k, i] for i in range(N_PIECES)])
        yk = w[:, k:k + 1] * jnp.concatenate(chunks, axis=1)
        y = yk if y is None else y + yk
    out = x1_ref[...] + ga2_ref[...] * y.reshape(bb, tm, D_MODEL)
    o_ref[...] = _rms(out) * fg_ref[...]


def _combine(x1, yg, wts, ga2, final_g, bb, tm):
    b, t, _ = x1.shape
    nb, nt = b // bb, t // tm
    rows = bb * tm
    return pl.pallas_call(
        _combine_kernel,
        grid=(nb, nt),
        in_specs=[pl.BlockSpec((bb, tm, D_MODEL), lambda i, j: (i, j, 0)),
                  pl.BlockSpec((2, N_PIECES, rows, PIECE_COLS), lambda i, j: (0, 0, i * nt + j, 0)),
                  pl.BlockSpec((rows, 2), lambda i, j: (i * nt + j, 0)),
                  pl.BlockSpec((bb, 1, D_MODEL), lambda i, j: (i, 0, 0)),
                  pl.BlockSpec((1, D_MODEL), lambda i, j: (0, 0))],
        out_specs=pl.BlockSpec((bb, tm, D_MODEL), lambda i, j: (i, j, 0)),
        out_shape=jax.ShapeDtypeStruct((b, t, D_MODEL), F32),
        compiler_params=pltpu.CompilerParams(dimension_semantics=("arbitrary", "arbitrary"),
                                             vmem_limit_bytes=VMEM_LIMIT),
        name="combine",
    )(x1, yg, wts, ga2, final_g)


def _block_diag(w):
    n, k, _ = w.shape
    eye = jnp.eye(n, dtype=w.dtype)
    return (eye[:, None, :, None] * w[:, :, None, :]).reshape(n * k, n * k)


def _tiles(f, tile):
    b, _, length = f.shape
    return f.reshape(b, N_HEADS // 2, 2, length // tile, tile).transpose(0, 1, 3, 2, 4)


def _untile_rows(a, b, t):
    nb, nt, r, rows = a.shape
    bb = b // nb
    tm = t // nt
    return a.reshape(nb, nt, r, bb, tm).transpose(2, 0, 3, 1, 4).reshape(r, b, t)


def _group_front(x, c_mod, layer, conv_past, h0, cache):
    b, t, _ = x.shape
    sh1, sc1, ga1, sh2, sc2, ga2 = [m.reshape(b, 1, D_MODEL) for m in jnp.split(c_mod, 6, axis=-1)]
    q, k, v, lf_steps, rec, h_t, ctail = _inproj(
        x, 1.0 + sc1, sh1, layer["n1"], layer["w_all"], layer["b_f"], layer["conv_w"], layer["conv_b"],
        layer["wa_bd"], layer["b_a"], layer["wx_bd"], layer["b_x"], layer["lam"], layer["g_rec"], conv_past, h0)
    lf_t = _untile_rows(lf_steps, b, t).transpose(1, 0, 2)
    ck = min(t, ATTN_CHUNK)
    if cache is None:
        f_all = _cumsum_time(lf_t)
        attn = _attention(q, k, v, _tiles(f_all, ck))
    else:
        k_past, v_past, lf_past = cache
        past = k_past.shape[1]
        total = past + t
        padded = -(-total // CUMSUM_CHUNK) * CUMSUM_CHUNK
        lf_all = jnp.concatenate([lf_past.transpose(0, 2, 1), lf_t,
                                  jnp.zeros((b, N_HEADS, padded - total), F32)], axis=2)
        f_all = _cumsum_time(lf_all)
        attn = _attention(q, k, v, _tiles(f_all[:, :, past:total], ck),
                          k_past.reshape(b, past, W_ATT), v_past.reshape(b, past, W_ATT),
                          _tiles(f_all[:, :, :past], PAST_TILE))
    bb, tm = (1, ROW_TILE) if t >= ROW_TILE else (ROW_TILE // t, t)
    x1, u2p, route = _outproj(attn, rec, x, ga1, 1.0 + sc2, sh2, layer["n2"], layer["g_att"],
                              layer["wo_a"], layer["wo_r"], layer["wr_hi"], layer["wr_mid"], layer["b_r"], bb, tm)
    route = _untile_rows(route, b, t).reshape(SUBLANES, b * t)
    leaves = (k.reshape(1, b, t, N_HEADS, HEAD_DIM), v.reshape(1, b, t, N_HEADS, HEAD_DIM),
              lf_t.transpose(0, 2, 1)[None], h_t[None], ctail[None, :, SUBLANES - (CONV_W - 1):, :])
    return x1, u2p.reshape(N_PIECES, b * t, PIECE_COLS), route, ga2, (bb, tm), leaves


def kernel(x_prompt, x_sample, c_prompt, c_sample, cache_k, cache_v, cache_logf, state_h, state_conv, norm1_g, norm2_g, w_ada, b_ada, w_in, b_f, conv_w, conv_b, w_rg_a, b_rg_a, w_rg_x, b_rg_x, rg_lambda, out_g_att, out_g_rec, w_out, w_route_group, b_route_group, w_route_expert, b_route_expert, w_exp_gate, w_exp_up, w_exp_down, final_g):
    bp, tp, _ = x_prompt.shape
    bs, ts, _ = x_sample.shape
    l = 0
    o1, o2, o3 = W_ATT, 2 * W_ATT, 3 * W_ATT
    o4 = o3 + N_HEADS
    o5 = o4 + W_REC
    w_in_l = w_in[l]
    w_all = jnp.concatenate([w_in_l[:, :o3], w_in_l[:, o4:], w_in_l[:, o3:o4],
                             jnp.zeros((D_MODEL, LANES - N_HEADS), F32)], axis=1).astype(BF16)
    w_r = jnp.zeros((D_MODEL, ROUTE_COLS), F32)
    w_r = w_r.at[:, 0:N_GROUPS].set(w_route_group[l]).at[:, EXPERT_COL0:EXPERT_COL0 + N_EXPERTS].set(w_route_expert[l])
    wr_hi = w_r.astype(BF16)
    wr_mid = (w_r - wr_hi.astype(F32)).astype(BF16)
    b_r = jnp.zeros((1, ROUTE_COLS), F32)
    b_r = b_r.at[0, 0:N_GROUPS].set(b_route_group[l]).at[0, N_GROUPS:SUBLANES].set(NEG_BIG)
    b_r = b_r.at[0, EXPERT_COL0:EXPERT_COL0 + N_EXPERTS].set(b_route_expert[l])
    layer = {
        "n1": norm1_g[l].reshape(1, D_MODEL), "n2": norm2_g[l].reshape(1, D_MODEL),
        "w_all": w_all,
        "b_f": jnp.concatenate([b_f[l], jnp.zeros((LANES - N_HEADS,), F32)]).reshape(1, LANES),
        "conv_w": conv_w[l], "conv_b": conv_b[l].reshape(1, W_REC),
        "wa_bd": _block_diag(w_rg_a[l]).astype(BF16), "b_a": b_rg_a[l].reshape(1, W_REC),
        "wx_bd": _block_diag(w_rg_x[l]).astype(BF16), "b_x": b_rg_x[l].reshape(1, W_REC),
        "lam": rg_lambda[l].reshape(1, W_REC), "g_rec": out_g_rec[l].reshape(1, W_REC),
        "g_att": out_g_att[l].reshape(1, W_ATT),
        "wo_a": w_out[l, :W_ATT].astype(BF16), "wo_r": w_out[l, W_ATT:].astype(BF16),
        "wr_hi": wr_hi, "wr_mid": wr_mid, "b_r": b_r,
    }

    ada = _ada(jnp.concatenate([c_prompt, c_sample], axis=0), w_ada[l], b_ada[l])
    pad_rows = SUBLANES - (CONV_W - 1)
    x1p, u2p_p, route_p, ga2p, tile_p, leaves_p = _group_front(
        x_prompt, ada[:bp], layer, jnp.zeros((bp, SUBLANES, W_REC), F32), jnp.zeros((bp, W_REC), F32), None)
    conv_past_s = jnp.concatenate([jnp.zeros((bs, pad_rows, W_REC), F32), state_conv[l]], axis=1)
    x1s, u2p_s, route_s, ga2s, tile_s, leaves_s = _group_front(
        x_sample, ada[bp:], layer, conv_past_s, state_h[l], (cache_k[l], cache_v[l], cache_logf[l]))

    n_p, n_s = bp * tp, bs * ts
    n_tok = n_p + n_s
    route = jnp.concatenate([route_p, route_s], axis=1)
    eid = route[0:2].astype(jnp.int32).reshape(-1)
    n_slot = 2 * n_tok
    tb = EXPERT_BLOCK
    experts = jnp.arange(N_EXPERTS, dtype=jnp.int32)
    onehot = (eid[:, None] == experts[None, :]).astype(jnp.int32)
    csum = jnp.cumsum(onehot, axis=0)
    counts = csum[-1]
    rank = jnp.sum(csum * onehot, axis=1) - 1
    padded = ((counts + tb - 1) // tb) * tb
    pend = jnp.cumsum(padded)
    pstart = pend - padded
    dest = jnp.sum(onehot * pstart[None, :], axis=1) + rank
    nblk = -(-n_slot // tb) + N_EXPERTS
    n_rows = nblk * tb
    blk_row0 = jnp.arange(nblk, dtype=jnp.int32) * tb
    blk_e = jnp.minimum(jnp.sum((pend[None, :] <= blk_row0[:, None]).astype(jnp.int32), axis=1), N_EXPERTS - 1)
    blk_first = jnp.concatenate([jnp.ones((1,), jnp.int32), (blk_e[1:] != blk_e[:-1]).astype(jnp.int32)])
    n_used = (pend[-1:] // tb).astype(jnp.int32)
    pad_e = padded - counts
    cpad = jnp.cumsum(pad_e)
    j = jnp.arange(n_rows - n_slot, dtype=jnp.int32)
    owner = (cpad[None, :] <= j[:, None]).astype(jnp.int32)
    e_j = jnp.minimum(jnp.sum(owner, axis=1), N_EXPERTS - 1)
    sel = (e_j[:, None] == experts[None, :]).astype(jnp.int32)
    in_expert = jnp.sum(sel * (pstart + counts - (cpad - pad_e))[None, :], axis=1) + j
    dest_pad = jnp.where(j < cpad[-1], in_expert, pend[-1] + (j - cpad[-1])).reshape(1, -1)

    dest = dest.reshape(2, n_tok)
    dests = [[dest[k, :n_p].reshape(1, n_p) for k in range(2)], [dest[k, n_p:].reshape(1, n_s) for k in range(2)]]
    xs = _dispatch([u2p_p, u2p_s], dests, dest_pad, n_rows)
    yo = _experts(xs, blk_e, blk_first, n_used, w_exp_gate[l], w_exp_up[l], w_exp_down[l])
    yg_p, yg_s = _collect(yo, dests)

    fg = final_g.reshape(1, D_MODEL)
    wts = route[2:4].T
    y_prompt = _combine(x1p, yg_p, wts[:n_p], ga2p, fg, *tile_p)
    y_sample = _combine(x1s, yg_s, wts[n_p:], ga2s, fg, *tile_s)
    return (y_prompt, y_sample) + leaves_p + leaves_s
```

```python
import functools
import math

import jax
import jax.numpy as jnp
from jax import lax
from jax.experimental import pallas as pl
from jax.experimental.pallas import tpu as pltpu
from jax.experimental.pallas import tpu_sc as plsc

F32 = jnp.float32
BF16 = jnp.bfloat16
U32 = jnp.uint32

D_MODEL = 1024
N_HEADS = 8
HEAD_DIM = 64
W_ATT = N_HEADS * HEAD_DIM
W_REC = D_MODEL - W_ATT
N_REC_BLOCKS = 8
REC_BLOCK = W_REC // N_REC_BLOCKS
CONV_W = 4
RG_C = 8.0
N_GROUPS = 4
EXPERTS_PER_GROUP = 8
N_EXPERTS = N_GROUPS * EXPERTS_PER_GROUP
D_EXPERT = 256
EPS = 1e-6

LANES = 128
SUBLANES = 8
LOG2E = math.log2(math.e)
Q_SCALE = LOG2E * HEAD_DIM ** -0.5
NEG_BIG = -1e30
VMEM_LIMIT = 48 * 1024 * 1024

SEQ_ROWS = 64
BATCH_ROWS = 8
CUMSUM_CHUNK = 512
ATTN_TILE = 2048
ATTN_CHUNK = 512
ATTN_UNROLL = 2
PAST_TILE = 1024
ROW_TILE = 512
EXPERT_BLOCK = 512
COMBINE_TILE = 1024
GATHER_WINDOW = 128
PIECE_COLS = 256
N_PIECES = D_MODEL // (2 * PIECE_COLS)
ROUTE_COLS = 128
EXPERT_COL0 = 8


def _split3(x):
    hi = x.astype(BF16)
    r1 = x - hi.astype(F32)
    mid = r1.astype(BF16)
    lo = (r1 - mid.astype(F32)).astype(BF16)
    return hi, mid, lo


def _dot(a, b):
    return jnp.dot(a, b, preferred_element_type=F32)


def _rms(x):
    return x * lax.rsqrt(jnp.mean(x * x, axis=-1, keepdims=True) + EPS)


def _sigmoid(x):
    return 0.5 * jnp.tanh(0.5 * x) + 0.5


def _pack_bf16_pairs(y):
    n = y.shape[-1] // 2
    yb = y.astype(BF16).astype(F32)
    lo = pltpu.bitcast(yb[:, :n], U32)
    hi = pltpu.bitcast(yb[:, n:], U32)
    return (lo >> 16) | hi


def _unpack_bf16_pairs(p):
    lo = pltpu.bitcast(p << 16, F32)
    hi = pltpu.bitcast(p & jnp.uint32(0xFFFF0000), F32)
    return lo, hi


def _pack_pieces(y):
    w = 2 * PIECE_COLS
    return [_pack_bf16_pairs(y[:, i * w:(i + 1) * w]) for i in range(N_PIECES)]


def _unpack_pieces(pieces):
    chunks = []
    for p in pieces:
        chunks.extend(_unpack_bf16_pairs(p))
    return chunks


def _ada_kernel(c_ref, w_ref, b_ref, o_ref):
    c = c_ref[...]
    a = c * jax.nn.sigmoid(c)
    a_hi, a_mid, _ = _split3(a)
    w_hi, w_mid, _ = _split3(w_ref[...])
    o_ref[...] = _dot(a_hi, w_hi) + _dot(a_mid, w_hi) + _dot(a_hi, w_mid) + b_ref[...]


def _ada(c, w_ada, b_ada):
    rows = c.shape[0]
    n = w_ada.shape[1]
    tn = 1536
    return pl.pallas_call(
        _ada_kernel,
        grid=(n // tn,),
        in_specs=[pl.BlockSpec((rows, D_MODEL), lambda j: (0, 0)),
                  pl.BlockSpec((D_MODEL, tn), lambda j: (0, j)),
                  pl.BlockSpec((1, tn), lambda j: (0, j))],
        out_specs=pl.BlockSpec((rows, tn), lambda j: (0, j)),
        out_shape=jax.ShapeDtypeStruct((rows, n), F32),
        compiler_params=pltpu.CompilerParams(vmem_limit_bytes=VMEM_LIMIT),
        name="ada",
    )(c, w_ada, b_ada.reshape(1, n))


def _inproj_kernel(x_ref, g1_ref, sh1_ref, n1_ref, w_ref, bf_ref, cw_ref, cb_ref, wa_ref, ba_ref, wx_ref, bx_ref,
                   lam_ref, grec_ref, cpast_ref, h0_ref,
                   q_ref, k_ref, v_ref, lf_ref, rec_ref, ht_ref, ctail_ref,
                   xbuf, ub_scr, z_scr, xc_scr, pr_scr, pi_scr, a_scr, u_scr, hs_scr, gl_scr, h_scr):
    bb, tm, _ = x_ref.shape
    rows = bb * tm
    n_lane_tiles = W_REC // LANES
    half = W_REC // 2
    o_x = 3 * W_ATT

    @pl.when(pl.program_id(1) == 0)
    def _():
        xbuf[:, 0:SUBLANES, :] = cpast_ref[...]
        h_scr[...] = h0_ref[...]

    mod = n1_ref[...] * g1_ref[...]
    for b in range(bb):
        rb = slice(b * tm, (b + 1) * tm)
        ub_scr[rb, :] = (_rms(x_ref[b]) * mod[b] + sh1_ref[b]).astype(BF16)

    z_scr[...] = _dot(ub_scr[...], w_ref[:, o_x:])

    cw = cw_ref[...]
    for b in range(bb):
        rb = slice(b * tm, (b + 1) * tm)
        xb = z_scr[rb, 0:W_REC]
        xbuf[b, SUBLANES:SUBLANES + tm, :] = xb
        xc = cb_ref[...] + cw[3:4, :] * xb
        for j in range(CONV_W - 1):
            off = SUBLANES - (CONV_W - 1) + j
            xc = xc + cw[j:j + 1, :] * xbuf[b, off:off + tm, :]
        tail = xbuf[b, tm:tm + SUBLANES, :]
        ctail_ref[b] = tail
        xbuf[b, 0:SUBLANES, :] = tail
        xc_scr[rb, :] = xc

    xcb = xc_scr[...].astype(BF16)
    for w_gate, pre in ((wa_ref, pr_scr), (wx_ref, pi_scr)):
        pre[:, :half] = _dot(xcb[:, :half], w_gate[:half, :half])
        pre[:, half:] = _dot(xcb[:, half:], w_gate[half:, half:])

    zq = _dot(ub_scr[...], w_ref[:, :o_x])
    q_ref[...] = (zq[:, 0:W_ATT] * Q_SCALE).reshape(bb, tm, W_ATT).astype(BF16)
    k_ref[...] = zq[:, W_ATT:2 * W_ATT].reshape(bb, tm, W_ATT)
    v_ref[...] = zq[:, 2 * W_ATT:3 * W_ATT].reshape(bb, tm, W_ATT)

    zf = z_scr[:, 2 * W_REC:2 * W_REC + LANES] + bf_ref[...]
    lf = jnp.minimum(zf, 0.0) - jnp.log1p(jnp.exp(-jnp.abs(zf)))
    lf_ref[0, 0] = lf.T[0:N_HEADS, :]

    nlam = -lam_ref[...]
    decay = -RG_C * (jnp.maximum(nlam, 0.0) + jnp.log1p(jnp.exp(-jnp.abs(nlam))))
    for b in range(bb):
        rb = slice(b * tm, (b + 1) * tm)
        r = _sigmoid(pr_scr[rb, :] + ba_ref[...])
        i = _sigmoid(pi_scr[rb, :] + bx_ref[...])
        log_a = decay * r
        t = jnp.tanh(log_a)
        neg_expm1 = -2.0 * t / (1.0 - t)
        root = jnp.where(neg_expm1 > 0.0, neg_expm1 * lax.rsqrt(neg_expm1), 0.0)
        pr_scr[rb, :] = jnp.exp(log_a)
        pi_scr[rb, :] = root * (i * xc_scr[rb, :])
        gb = z_scr[rb, W_REC:2 * W_REC]
        gl_scr[rb, :] = 0.5 * gb * (1.0 + jnp.tanh(math.sqrt(2.0 / math.pi) * (gb + 0.044715 * (gb * gb * gb))))

    def lane_tile(ref, c):
        return ref[:, c * LANES:(c + 1) * LANES].reshape(bb, tm, LANES)

    for c in range(n_lane_tiles):
        a_scr[c] = jnp.swapaxes(lane_tile(pr_scr, c), 0, 1)
        u_scr[c] = jnp.swapaxes(lane_tile(pi_scr, c), 0, 1)
    for c in range(n_lane_tiles):
        h = h_scr[:, c * LANES:(c + 1) * LANES]
        for step in range(tm):
            h = a_scr[c, step] * h + u_scr[c, step]
            hs_scr[c, step] = h
        h_scr[:, c * LANES:(c + 1) * LANES] = h
    ht_ref[...] = h_scr[...]
    for c in range(n_lane_tiles):
        pr_scr[:, c * LANES:(c + 1) * LANES] = jnp.swapaxes(hs_scr[c], 0, 1).reshape(rows, LANES)

    for b in range(bb):
        rb = slice(b * tm, (b + 1) * tm)
        y = gl_scr[rb, :] * pr_scr[rb, :]
        rec_ref[b] = (_rms(y) * grec_ref[...]).astype(BF16)


def _inproj(x, g1, sh1, n1, w_all, b_f, conv_w, conv_b, wa_bd, b_a, wx_bd, b_x, lam, g_rec, conv_past, h0):
    b, t, _ = x.shape
    bb, tm = BATCH_ROWS, SEQ_ROWS
    nb, nt = b // bb, t // tm
    rows = bb * tm
    wcols = w_all.shape[1]
    const2 = lambda shape: pl.BlockSpec(shape, lambda i, j: (0, 0))
    per_b = lambda last: pl.BlockSpec((bb, 1, last), lambda i, j: (i, 0, 0))
    seq = lambda last: pl.BlockSpec((bb, tm, last), lambda i, j: (i, j, 0))
    return pl.pallas_call(
        _inproj_kernel,
        grid=(nb, nt),
        in_specs=[seq(D_MODEL), per_b(D_MODEL), per_b(D_MODEL), const2((1, D_MODEL)),
                  const2((D_MODEL, wcols)), const2((1, LANES)),
                  const2((CONV_W, W_REC)), const2((1, W_REC)),
                  const2((W_REC, W_REC)), const2((1, W_REC)), const2((W_REC, W_REC)), const2((1, W_REC)),
                  const2((1, W_REC)), const2((1, W_REC)),
                  pl.BlockSpec((bb, SUBLANES, W_REC), lambda i, j: (i, 0, 0)),
                  pl.BlockSpec((bb, W_REC), lambda i, j: (i, 0))],
        out_specs=[seq(W_ATT), seq(W_ATT), seq(W_ATT),
                   pl.BlockSpec((1, 1, N_HEADS, rows), lambda i, j: (i, j, 0, 0)),
                   seq(W_REC),
                   pl.BlockSpec((bb, W_REC), lambda i, j: (i, 0)),
                   pl.BlockSpec((bb, SUBLANES, W_REC), lambda i, j: (i, 0, 0))],
        out_shape=[jax.ShapeDtypeStruct((b, t, W_ATT), BF16),
                   jax.ShapeDtypeStruct((b, t, W_ATT), F32),
                   jax.ShapeDtypeStruct((b, t, W_ATT), F32),
                   jax.ShapeDtypeStruct((nb, nt, N_HEADS, rows), F32),
                   jax.ShapeDtypeStruct((b, t, W_REC), BF16),
                   jax.ShapeDtypeStruct((b, W_REC), F32),
                   jax.ShapeDtypeStruct((b, SUBLANES, W_REC), F32)],
        scratch_shapes=[pltpu.VMEM((bb, tm + SUBLANES, W_REC), F32),
                        pltpu.VMEM((rows, D_MODEL), BF16),
                        pltpu.VMEM((rows, 2 * W_REC + LANES), F32),
                        pltpu.VMEM((rows, W_REC), F32),
                        pltpu.VMEM((rows, W_REC), F32),
                        pltpu.VMEM((rows, W_REC), F32),
                        pltpu.VMEM((W_REC // LANES, tm, bb, LANES), F32),
                        pltpu.VMEM((W_REC // LANES, tm, bb, LANES), F32),
                        pltpu.VMEM((W_REC // LANES, tm, bb, LANES), F32),
                        pltpu.VMEM((rows, W_REC), F32),
                        pltpu.VMEM((bb, W_REC), F32)],
        compiler_params=pltpu.CompilerParams(dimension_semantics=("arbitrary", "arbitrary"),
                                             vmem_limit_bytes=VMEM_LIMIT),
        name="inproj",
    )(x, g1, sh1, n1, w_all, b_f, conv_w, conv_b, wa_bd, b_a, wx_bd, b_x, lam, g_rec, conv_past, h0)


def _cumsum_kernel(x_ref, o_ref):
    length = x_ref.shape[2]
    c = CUMSUM_CHUNK
    upper = (lax.broadcasted_iota(jnp.int32, (c, c), 0) <= lax.broadcasted_iota(jnp.int32, (c, c), 1)).astype(BF16)
    carry = jnp.zeros((N_HEADS, 1), F32)
    for j in range(length // c):
        x = x_ref[0, :, j * c:(j + 1) * c]
        hi, mid, lo = _split3(x)
        parts = jnp.concatenate([hi.astype(F32), mid.astype(F32), lo.astype(F32), jnp.zeros_like(x)], axis=0)
        sums = _dot(parts.astype(BF16), upper)
        out = sums[0:8] + sums[8:16] + sums[16:24] + carry
        o_ref[0, :, j * c:(j + 1) * c] = out
        carry = out[:, c - 1:c]


def _cumsum_time(lf_t):
    b, h, length = lf_t.shape
    return pl.pallas_call(
        _cumsum_kernel,
        grid=(b,),
        in_specs=[pl.BlockSpec((1, h, length), lambda i: (i, 0, 0))],
        out_specs=pl.BlockSpec((1, h, length), lambda i: (i, 0, 0)),
        out_shape=jax.ShapeDtypeStruct((b, h, length), F32),
        name="cumsum",
    )(lf_t)


def _attn_tile(qh, kt, vt, bias, m_ref, acc_ref, mask):
    s = lax.dot_general(qh, kt, (((1,), (1,)), ((), ())), preferred_element_type=F32) + bias
    if mask is not None:
        s = jnp.where(mask, s, NEG_BIG)
    width = min(LANES, s.shape[1])
    slabs = [s[:, c * width:(c + 1) * width] for c in range(s.shape[1] // width)]
    part = slabs[0]
    for sl in slabs[1:]:
        part = jnp.maximum(part, sl)
    m = m_ref[...]
    m_new = jnp.maximum(m, jnp.broadcast_to(jnp.max(part, axis=1, keepdims=True), m.shape))
    p = jnp.concatenate([jnp.exp2(sl - m_new[:, :width]) for sl in slabs], axis=1)
    acc_ref[...] = jnp.exp2(m - m_new) * acc_ref[...] + _dot(p.astype(BF16), vt)
    m_ref[...] = m_new


def _attn_kernel(*refs, n_past_tiles, past_tile, tq, ck):
    if n_past_tiles:
        q_ref, kn_ref, vn_ref, fn_ref, kp_ref, vp_ref, fp_ref, o_ref, kb, vb, m_scr, acc_scr = refs
        past = n_past_tiles * past_tile
    else:
        q_ref, kn_ref, vn_ref, fn_ref, o_ref, kb, vb, m_scr, acc_scr = refs
        past = 0
    qi = pl.program_id(2)
    t_new = kn_ref.shape[1]
    chunks_per_tile = tq // ck
    for pair in range(q_ref.shape[2] // LANES):
        lanes = slice(pair * LANES, (pair + 1) * LANES)

        @pl.when(qi == 0)
        def _():
            def fill(lo, k_rows, v_rows):
                n = k_rows.shape[0]
                first = lax.broadcasted_iota(jnp.int32, (n, LANES), 1) < HEAD_DIM
                v_f32 = v_rows.astype(F32)
                kb[pair, lo:lo + n, :] = k_rows.astype(BF16)
                vb[pair, 0, lo:lo + n, :] = jnp.where(first, v_f32, 1.0).astype(BF16)
                vb[pair, 1, lo:lo + n, :] = jnp.where(first, 1.0, v_f32).astype(BF16)

            if n_past_tiles:
                fill(0, kp_ref[0, :, lanes], vp_ref[0, :, lanes])
            fill(past, kn_ref[0, :, lanes], vn_ref[0, :, lanes])

        q = q_ref[0, :, lanes]
        first_head = lax.broadcasted_iota(jnp.int32, q.shape, 1) < HEAD_DIM
        qh = (jnp.where(first_head, q, jnp.zeros_like(q)), jnp.where(first_head, jnp.zeros_like(q), q))

        ref_f = fn_ref[0, pair, qi * chunks_per_tile][:, 0:1]
        m_scr[...] = jnp.full(m_scr.shape, NEG_BIG, F32)
        acc_scr[...] = jnp.zeros(acc_scr.shape, F32)

        def tile(start, size, f_tile, row0=0, masked=False):
            kt = kb[pair, pl.ds(start, size), :]
            mask = None
            if masked:
                shape = (tq - row0, size)
                mask = lax.broadcasted_iota(jnp.int32, shape, 1) <= lax.broadcasted_iota(jnp.int32, shape, 0)
            for hd in range(2):
                bias = (ref_f[hd:hd + 1, :] - f_tile[hd:hd + 1, :]) * LOG2E
                _attn_tile(qh[hd][row0:, :], kt, vb[pair, hd, pl.ds(start, size), :], bias,
                           m_scr.at[hd, row0:tq], acc_scr.at[hd, row0:tq], mask)

        for j in range(n_past_tiles):
            tile(j * past_tile, past_tile, fp_ref[0, pair, j])

        if tq < t_new:
            n_chunks = qi * chunks_per_tile

            def chunk(j):
                tile(pl.multiple_of(past + j * ck, ck), ck, fn_ref[0, pair, j])

            def chunk_body(jj, carry):
                for u in range(ATTN_UNROLL):
                    chunk(jj * ATTN_UNROLL + u)
                return carry

            lax.fori_loop(0, n_chunks // ATTN_UNROLL, chunk_body, 0)
            for r in range(ATTN_UNROLL - 1):
                @pl.when(r < n_chunks % ATTN_UNROLL)
                def _():
                    chunk((n_chunks // ATTN_UNROLL) * ATTN_UNROLL + r)

        for c in range(chunks_per_tile):
            j = qi * chunks_per_tile + c
            tile(pl.multiple_of(past + j * ck, ck), ck, fn_ref[0, pair, j], row0=c * ck, masked=True)

        outs = []
        for hd in range(2):
            acc = acc_scr[hd]
            outs.append(acc / pltpu.roll(acc, HEAD_DIM, axis=1))
        o_ref[0, :, lanes] = jnp.where(first_head, outs[0], outs[1]).astype(BF16)


def _attention(q, k_new, v_new, f_new, k_past=None, v_past=None, f_past=None):
    b, t, _ = q.shape
    tq = min(t, ATTN_TILE)
    nq = t // tq
    pp = 1 if nq > 1 else N_HEADS // 2
    pairs = N_HEADS // 2 // pp
    width = pp * LANES
    q_spec = pl.BlockSpec((1, tq, width), lambda i, p, j: (i, j, p))
    full = lambda rows: pl.BlockSpec((1, rows, width), lambda i, p, j: (i, 0, p))
    f_spec = lambda arr: pl.BlockSpec((1, pp) + arr.shape[2:], lambda i, p, j: (i, p, 0, 0, 0))
    in_specs = [q_spec, full(t), full(t), f_spec(f_new)]
    args = [q, k_new, v_new, f_new]
    past = 0
    n_past_tiles = 0
    past_tile = 0
    if k_past is not None:
        past = k_past.shape[1]
        past_tile = f_past.shape[-1]
        n_past_tiles = past // past_tile
        in_specs += [full(past), full(past), f_spec(f_past)]
        args += [k_past, v_past, f_past]
    ck = f_new.shape[-1]
    return pl.pallas_call(
        functools.partial(_attn_kernel, n_past_tiles=n_past_tiles, past_tile=past_tile, tq=tq, ck=ck),
        grid=(b, pairs, nq),
        in_specs=in_specs,
        out_specs=q_spec,
        out_shape=jax.ShapeDtypeStruct((b, t, W_ATT), BF16),
        scratch_shapes=[pltpu.VMEM((pp, past + t, LANES), BF16), pltpu.VMEM((pp, 2, past + t, LANES), BF16),
                        pltpu.VMEM((2, tq, LANES), F32), pltpu.VMEM((2, tq, LANES), F32)],
        compiler_params=pltpu.CompilerParams(dimension_semantics=("arbitrary", "arbitrary", "arbitrary"),
                                             vmem_limit_bytes=VMEM_LIMIT),
        name="attn",
    )(*args)


def _outproj_kernel(attn_ref, rec_ref, x_ref, ga1_ref, g2_ref, sh2_ref, n2_ref, gatt_ref, woa_ref, wor_ref,
                    wrh_ref, wrm_ref, br_ref,
                    x1_ref, u2p_ref, route_ref):
    bb, tm, _ = x_ref.shape
    rows = bb * tm
    attn = attn_ref[...].astype(F32)
    an = (_rms(attn) * gatt_ref[...]).reshape(rows, W_ATT).astype(BF16)
    mix = _dot(an, woa_ref[...]) + _dot(rec_ref[...].reshape(rows, W_REC), wor_ref[...])
    x1 = x_ref[...] + ga1_ref[...] * mix.reshape(bb, tm, D_MODEL)
    x1_ref[...] = x1
    u2 = (_rms(x1) * (n2_ref[...] * g2_ref[...]) + sh2_ref[...]).reshape(rows, D_MODEL)
    for i, piece in enumerate(_pack_pieces(u2)):
        u2p_ref[i] = piece.reshape(bb, tm, PIECE_COLS)

    u_hi, u_mid, _ = _split3(u2)
    logits = _dot(u_hi, wrh_ref[...]) + _dot(u_mid, wrh_ref[...]) + _dot(u_hi, wrm_ref[...]) + br_ref[...]
    lt = logits.T
    row8 = lax.broadcasted_iota(jnp.int32, (SUBLANES, rows), 0).astype(F32)
    lg = lt[0:SUBLANES]
    m_g = jnp.max(lg, axis=0, keepdims=True)
    gidx = jnp.min(jnp.where(lg == m_g, row8, float(SUBLANES)), axis=0, keepdims=True)
    p_top = 1.0 / jnp.sum(jnp.exp(lg - m_g), axis=0, keepdims=True)
    leg = jnp.zeros((EXPERTS_PER_GROUP, rows), F32)
    for g in range(N_GROUPS):
        lo = EXPERT_COL0 + g * EXPERTS_PER_GROUP
        leg = jnp.where(gidx == float(g), lt[lo:lo + EXPERTS_PER_GROUP], leg)
    v1 = jnp.max(leg, axis=0, keepdims=True)
    i1 = jnp.min(jnp.where(leg == v1, row8, float(SUBLANES)), axis=0, keepdims=True)
    leg2 = jnp.where(row8 == i1, -jnp.inf, leg)
    v2 = jnp.max(leg2, axis=0, keepdims=True)
    i2 = jnp.min(jnp.where(leg2 == v2, row8, float(SUBLANES)), axis=0, keepdims=True)
    e21 = jnp.exp(v2 - v1)
    w1 = p_top / (1.0 + e21)
    w2 = w1 * e21
    base = gidx * float(EXPERTS_PER_GROUP)
    out = jnp.where(row8 == 3.0, w2, 0.0)
    for r_idx, val in ((2.0, w1), (1.0, base + i2), (0.0, base + i1)):
        out = jnp.where(row8 == r_idx, val, out)
    route_ref[0, 0] = out


def _outproj(attn, rec, x, ga1, g2, sh2, n2, g_att, wo_a, wo_r, wr_hi, wr_mid, b_r, bb, tm):
    b, t, _ = x.shape
    nb, nt = b // bb, t // tm
    rows = bb * tm
    const2 = lambda shape: pl.BlockSpec(shape, lambda i, j: (0, 0))
    per_b = pl.BlockSpec((bb, 1, D_MODEL), lambda i, j: (i, 0, 0))
    seq = lambda last: pl.BlockSpec((bb, tm, last), lambda i, j: (i, j, 0))
    return pl.pallas_call(
        _outproj_kernel,
        grid=(nb, nt),
        in_specs=[seq(W_ATT), seq(W_REC), seq(D_MODEL), per_b, per_b, per_b, const2((1, D_MODEL)),
                  const2((1, W_ATT)), const2((W_ATT, D_MODEL)), const2((W_REC, D_MODEL)),
                  const2((D_MODEL, ROUTE_COLS)), const2((D_MODEL, ROUTE_COLS)), const2((1, ROUTE_COLS))],
        out_specs=[seq(D_MODEL),
                   pl.BlockSpec((N_PIECES, bb, tm, PIECE_COLS), lambda i, j: (0, i, j, 0)),
                   pl.BlockSpec((1, 1, SUBLANES, rows), lambda i, j: (i, j, 0, 0))],
        out_shape=[jax.ShapeDtypeStruct((b, t, D_MODEL), F32),
                   jax.ShapeDtypeStruct((N_PIECES, b, t, PIECE_COLS), U32),
                   jax.ShapeDtypeStruct((nb, nt, SUBLANES, rows), F32)],
        compiler_params=pltpu.CompilerParams(dimension_semantics=("arbitrary", "arbitrary"),
                                             vmem_limit_bytes=VMEM_LIMIT),
        name="outproj",
    )(attn, rec, x, ga1, g2, sh2, n2, g_att, wo_a, wo_r, wr_hi, wr_mid, b_r)


_SC_AXES = ("core", "subcore")


def _sc_mesh():
    return plsc.VectorSubcoreMesh(core_axis_name=_SC_AXES[0], subcore_axis_name=_SC_AXES[1])


def _sc_scatter_rows(src_hbm, idx_hbm, n, dst_hbm):
    def body(x_vmem, i_vmem):
        pltpu.sync_copy(x_vmem, dst_hbm.at[i_vmem.at[0]])

    pltpu.emit_pipeline(
        body,
        grid=(n // GATHER_WINDOW,),
        in_specs=[pl.BlockSpec((GATHER_WINDOW, PIECE_COLS), lambda i: (i, 0)),
                  pl.BlockSpec((1, GATHER_WINDOW), lambda i: (0, i))],
        out_specs=[],
        core_axis_name=_SC_AXES,
        dimension_semantics=(pltpu.PARALLEL,),
    )(src_hbm, idx_hbm)


def _sc_gather_rows(tab_hbm, idx_hbm, n, dst_hbm):
    def body(i_vmem, o_vmem):
        pltpu.sync_copy(tab_hbm.at[i_vmem.at[0]], o_vmem)

    pltpu.emit_pipeline(
        body,
        grid=(n // GATHER_WINDOW,),
        in_specs=[pl.BlockSpec((1, GATHER_WINDOW), lambda i: (0, i))],
        out_specs=[pl.BlockSpec((GATHER_WINDOW, PIECE_COLS), lambda i: (i, 0))],
        core_axis_name=_SC_AXES,
        dimension_semantics=(pltpu.PARALLEL,),
    )(idx_hbm, dst_hbm)


def _dispatch(tables, dests, dest_pad, n_rows):
    n_pad = dest_pad.shape[1]
    sizes = [t.shape[1] for t in tables]
    n_groups = len(tables)

    @pl.kernel(out_type=jax.ShapeDtypeStruct((N_PIECES, n_rows, PIECE_COLS), U32), mesh=_sc_mesh(), scratch_types=[])
    def scatter(*refs):
        tabs = refs[:n_groups]
        idxs = refs[n_groups:3 * n_groups]
        pad_hbm, o_hbm = refs[3 * n_groups], refs[3 * n_groups + 1]
        for piece in range(N_PIECES):
            dst = o_hbm.at[piece]
            for g in range(n_groups):
                for k in range(2):
                    _sc_scatter_rows(tabs[g].at[piece], idxs[2 * g + k], sizes[g], dst)
            _sc_scatter_rows(tabs[0].at[piece], pad_hbm, n_pad, dst)

    flat_idx = [d for pair in dests for d in pair]
    return scatter(*tables, *flat_idx, dest_pad)


def _collect(yo, dests):
    sizes = [pair[0].shape[1] for pair in dests]
    n_groups = len(dests)
    out_type = [jax.ShapeDtypeStruct((2, N_PIECES, n, PIECE_COLS), U32) for n in sizes]

    @pl.kernel(out_type=out_type, mesh=_sc_mesh(), scratch_types=[])
    def gather(*refs):
        yo_hbm = refs[0]
        idxs = refs[1:1 + 2 * n_groups]
        outs = refs[1 + 2 * n_groups:]
        for piece in range(N_PIECES):
            for g in range(n_groups):
                for k in range(2):
                    _sc_gather_rows(yo_hbm.at[piece], idxs[2 * g + k], sizes[g], outs[g].at[k, piece])

    flat_idx = [d for pair in dests for d in pair]
    return gather(yo, *flat_idx)


def _expert_kernel(blk_e_ref, blk_first_ref, n_used_ref, xs_ref, wg_ref, wu_ref, wd_ref, yo_ref, wg_b, wu_b, wd_b):
    b = pl.program_id(0)

    @pl.when(blk_first_ref[b] == 1)
    def _():
        wg_b[...] = wg_ref[0].astype(BF16)
        wu_b[...] = wu_ref[0].astype(BF16)
        wd_b[...] = wd_ref[0].astype(BF16)

    @pl.when(b < n_used_ref[0])
    def _():
        chunks = [c.astype(BF16) for c in _unpack_pieces([xs_ref[i] for i in range(N_PIECES)])]
        g = None
        u = None
        for i, xc in enumerate(chunks):
            rows_i = slice(i * PIECE_COLS, (i + 1) * PIECE_COLS)
            gi = _dot(xc, wg_b[rows_i, :])
            ui = _dot(xc, wu_b[rows_i, :])
            g = gi if g is None else g + gi
            u = ui if u is None else u + ui
        h = (g * _sigmoid(g)) * u
        for i, piece in enumerate(_pack_pieces(_dot(h.astype(BF16), wd_b[...]))):
            yo_ref[i] = piece

    @pl.when(b >= n_used_ref[0])
    def _():
        yo_ref[...] = jnp.zeros(yo_ref.shape, U32)


def _experts(xs, blk_e, blk_first, n_used, w_gate, w_up, w_down):
    p = xs.shape[1]
    tb = EXPERT_BLOCK
    nblk = p // tb
    row_block = pl.BlockSpec((N_PIECES, tb, PIECE_COLS), lambda i, e, f, n: (0, i, 0))
    grid_spec = pltpu.PrefetchScalarGridSpec(
        num_scalar_prefetch=3,
        grid=(nblk,),
        in_specs=[row_block,
                  pl.BlockSpec((1, D_MODEL, D_EXPERT), lambda i, e, f, n: (e[i], 0, 0)),
                  pl.BlockSpec((1, D_MODEL, D_EXPERT), lambda i, e, f, n: (e[i], 0, 0)),
                  pl.BlockSpec((1, D_EXPERT, D_MODEL), lambda i, e, f, n: (e[i], 0, 0))],
        out_specs=row_block,
        scratch_shapes=[pltpu.VMEM((D_MODEL, D_EXPERT), BF16),
                        pltpu.VMEM((D_MODEL, D_EXPERT), BF16),
                        pltpu.VMEM((D_EXPERT, D_MODEL), BF16)],
    )
    return pl.pallas_call(
        _expert_kernel,
        grid_spec=grid_spec,
        out_shape=jax.ShapeDtypeStruct((N_PIECES, p, PIECE_COLS), U32),
        compiler_params=pltpu.CompilerParams(dimension_semantics=("arbitrary",), vmem_limit_bytes=VMEM_LIMIT),
        name="experts",
    )(blk_e, blk_first, n_used, xs, w_gate, w_up, w_down)


def _combine_kernel(x1_ref, yg_ref, w_ref, ga2_ref, fg_ref, o_ref):
    bb, tm, _ = x1_ref.shape
    w = w_ref[...]
    y = None
    for k in range(2):
        chunks = _unpack_pieces([yg_ref[k, i] for i in range(N_PIECES)])
        yk = w[:, k:k + 1] * jnp.concatenate(chunks, axis=1)
        y = yk if y is None else y + yk
    out = x1_ref[...] + ga2_ref[...] * y.reshape(bb, tm, D_MODEL)
    o_ref[...] = _rms(out) * fg_ref[...]


def _combine(x1, yg, wts, ga2, final_g, bb, tm):
    b, t, _ = x1.shape
    nb, nt = b // bb, t // tm
    rows = bb * tm
    return pl.pallas_call(
        _combine_kernel,
        grid=(nb, nt),
        in_specs=[pl.BlockSpec((bb, tm, D_MODEL), lambda i, j: (i, j, 0)),
                  pl.BlockSpec((2, N_PIECES, rows, PIECE_COLS), lambda i, j: (0, 0, i * nt + j, 0)),
                  pl.BlockSpec((rows, 2), lambda i, j: (i * nt + j, 0)),
                  pl.BlockSpec((bb, 1, D_MODEL), lambda i, j: (i, 0, 0)),
                  pl.BlockSpec((1, D_MODEL), lambda i, j: (0, 0))],
        out_specs=pl.BlockSpec((bb, tm, D_MODEL), lambda i, j: (i, j, 0)),
        out_shape=jax.ShapeDtypeStruct((b, t, D_MODEL), F32),
        compiler_params=pltpu.CompilerParams(dimension_semantics=("arbitrary", "arbitrary"),
                                             vmem_limit_bytes=VMEM_LIMIT),
        name="combine",
    )(x1, yg, wts, ga2, final_g)


def _block_diag(w):
    n, k, _ = w.shape
    eye = jnp.eye(n, dtype=w.dtype)
    return (eye[:, None, :, None] * w[:, :, None, :]).reshape(n * k, n * k)


def _tiles(f, tile):
    b, _, length = f.shape
    return f.reshape(b, N_HEADS // 2, 2, length // tile, tile).transpose(0, 1, 3, 2, 4)


def _untile_rows(a, b, t):
    nb, nt, r, rows = a.shape
    bb = b // nb
    tm = t // nt
    return a.reshape(nb, nt, r, bb, tm).transpose(2, 0, 3, 1, 4).reshape(r, b, t)


def _group_front(x, c_mod, layer, conv_past, h0, cache):
    b, t, _ = x.shape
    sh1, sc1, ga1, sh2, sc2, ga2 = [m.reshape(b, 1, D_MODEL) for m in jnp.split(c_mod, 6, axis=-1)]
    q, k, v, lf_steps, rec, h_t, ctail = _inproj(
        x, 1.0 + sc1, sh1, layer["n1"], layer["w_all"], layer["b_f"], layer["conv_w"], layer["conv_b"],
        layer["wa_bd"], layer["b_a"], layer["wx_bd"], layer["b_x"], layer["lam"], layer["g_rec"], conv_past, h0)
    lf_t = _untile_rows(lf_steps, b, t).transpose(1, 0, 2)
    ck = min(t, ATTN_CHUNK)
    if cache is None:
        f_all = _cumsum_time(lf_t)
        attn = _attention(q, k, v, _tiles(f_all, ck))
    else:
        k_past, v_past, lf_past = cache
        past = k_past.shape[1]
        total = past + t
        padded = -(-total // CUMSUM_CHUNK) * CUMSUM_CHUNK
        lf_all = jnp.concatenate([lf_past.transpose(0, 2, 1), lf_t,
                                  jnp.zeros((b, N_HEADS, padded - total), F32)], axis=2)
        f_all = _cumsum_time(lf_all)
        attn = _attention(q, k, v, _tiles(f_all[:, :, past:total], ck),
                          k_past.reshape(b, past, W_ATT).astype(BF16), v_past.reshape(b, past, W_ATT).astype(BF16),
                          _tiles(f_all[:, :, :past], PAST_TILE))
    bb, tm = (1, ROW_TILE) if t >= ROW_TILE else (ROW_TILE // t, t)
    x1, u2p, route = _outproj(attn, rec, x, ga1, 1.0 + sc2, sh2, layer["n2"], layer["g_att"],
                              layer["wo_a"], layer["wo_r"], layer["wr_hi"], layer["wr_mid"], layer["b_r"], bb, tm)
    route = _untile_rows(route, b, t).reshape(SUBLANES, b * t)
    leaves = (k.reshape(1, b, t, N_HEADS, HEAD_DIM), v.reshape(1, b, t, N_HEADS, HEAD_DIM),
              lf_t.transpose(0, 2, 1)[None], h_t[None], ctail[None, :, SUBLANES - (CONV_W - 1):, :])
    combine_tile = (1, COMBINE_TILE) if t >= COMBINE_TILE else (bb, tm)
    return x1, u2p.reshape(N_PIECES, b * t, PIECE_COLS), route, ga2, combine_tile, leaves


def kernel(x_prompt, x_sample, c_prompt, c_sample, cache_k, cache_v, cache_logf, state_h, state_conv, norm1_g, norm2_g, w_ada, b_ada, w_in, b_f, conv_w, conv_b, w_rg_a, b_rg_a, w_rg_x, b_rg_x, rg_lambda, out_g_att, out_g_rec, w_out, w_route_group, b_route_group, w_route_expert, b_route_expert, w_exp_gate, w_exp_up, w_exp_down, final_g):
    bp, tp, _ = x_prompt.shape
    bs, ts, _ = x_sample.shape
    l = 0
    o1, o2, o3 = W_ATT, 2 * W_ATT, 3 * W_ATT
    o4 = o3 + N_HEADS
    o5 = o4 + W_REC
    w_in_l = w_in[l]
    w_all = jnp.concatenate([w_in_l[:, :o3], w_in_l[:, o4:], w_in_l[:, o3:o4],
                             jnp.zeros((D_MODEL, LANES - N_HEADS), F32)], axis=1).astype(BF16)
    w_r = jnp.zeros((D_MODEL, ROUTE_COLS), F32)
    w_r = w_r.at[:, 0:N_GROUPS].set(w_route_group[l]).at[:, EXPERT_COL0:EXPERT_COL0 + N_EXPERTS].set(w_route_expert[l])
    wr_hi = w_r.astype(BF16)
    wr_mid = (w_r - wr_hi.astype(F32)).astype(BF16)
    b_r = jnp.zeros((1, ROUTE_COLS), F32)
    b_r = b_r.at[0, 0:N_GROUPS].set(b_route_group[l]).at[0, N_GROUPS:SUBLANES].set(NEG_BIG)
    b_r = b_r.at[0, EXPERT_COL0:EXPERT_COL0 + N_EXPERTS].set(b_route_expert[l])
    layer = {
        "n1": norm1_g[l].reshape(1, D_MODEL), "n2": norm2_g[l].reshape(1, D_MODEL),
        "w_all": w_all,
        "b_f": jnp.concatenate([b_f[l], jnp.zeros((LANES - N_HEADS,), F32)]).reshape(1, LANES),
        "conv_w": conv_w[l], "conv_b": conv_b[l].reshape(1, W_REC),
        "wa_bd": _block_diag(w_rg_a[l]).astype(BF16), "b_a": b_rg_a[l].reshape(1, W_REC),
        "wx_bd": _block_diag(w_rg_x[l]).astype(BF16), "b_x": b_rg_x[l].reshape(1, W_REC),
        "lam": rg_lambda[l].reshape(1, W_REC), "g_rec": out_g_rec[l].reshape(1, W_REC),
        "g_att": out_g_att[l].reshape(1, W_ATT),
        "wo_a": w_out[l, :W_ATT].astype(BF16), "wo_r": w_out[l, W_ATT:].astype(BF16),
        "wr_hi": wr_hi, "wr_mid": wr_mid, "b_r": b_r,
    }

    ada = _ada(jnp.concatenate([c_prompt, c_sample], axis=0), w_ada[l], b_ada[l])
    pad_rows = SUBLANES - (CONV_W - 1)
    x1p, u2p_p, route_p, ga2p, tile_p, leaves_p = _group_front(
        x_prompt, ada[:bp], layer, jnp.zeros((bp, SUBLANES, W_REC), F32), jnp.zeros((bp, W_REC), F32), None)
    conv_past_s = jnp.concatenate([jnp.zeros((bs, pad_rows, W_REC), F32), state_conv[l]], axis=1)
    x1s, u2p_s, route_s, ga2s, tile_s, leaves_s = _group_front(
        x_sample, ada[bp:], layer, conv_past_s, state_h[l], (cache_k[l], cache_v[l], cache_logf[l]))

    n_p, n_s = bp * tp, bs * ts
    n_tok = n_p + n_s
    route = jnp.concatenate([route_p, route_s], axis=1)
    eid = route[0:2].astype(jnp.int32).reshape(-1)
    n_slot = 2 * n_tok
    tb = EXPERT_BLOCK
    experts = jnp.arange(N_EXPERTS, dtype=jnp.int32)
    onehot = (eid[:, None] == experts[None, :]).astype(jnp.int32)
    csum = jnp.cumsum(onehot, axis=0)
    counts = csum[-1]
    rank = jnp.sum(csum * onehot, axis=1) - 1
    padded = ((counts + tb - 1) // tb) * tb
    pend = jnp.cumsum(padded)
    pstart = pend - padded
    dest = jnp.sum(onehot * pstart[None, :], axis=1) + rank
    nblk = -(-n_slot // tb) + N_EXPERTS
    n_rows = nblk * tb
    blk_row0 = jnp.arange(nblk, dtype=jnp.int32) * tb
    blk_e = jnp.minimum(jnp.sum((pend[None, :] <= blk_row0[:, None]).astype(jnp.int32), axis=1), N_EXPERTS - 1)
    blk_first = jnp.concatenate([jnp.ones((1,), jnp.int32), (blk_e[1:] != blk_e[:-1]).astype(jnp.int32)])
    n_used = (pend[-1:] // tb).astype(jnp.int32)
    pad_e = padded - counts
    cpad = jnp.cumsum(pad_e)
    j = jnp.arange(n_rows - n_slot, dtype=jnp.int32)
    owner = (cpad[None, :] <= j[:, None]).astype(jnp.int32)
    e_j = jnp.minimum(jnp.sum(owner, axis=1), N_EXPERTS - 1)
    sel = (e_j[:, None] == experts[None, :]).astype(jnp.int32)
    in_expert = jnp.sum(sel * (pstart + counts - (cpad - pad_e))[None, :], axis=1) + j
    dest_pad = jnp.where(j < cpad[-1], in_expert, pend[-1] + (j - cpad[-1])).reshape(1, -1)

    dest = dest.reshape(2, n_tok)
    dests = [[dest[k, :n_p].reshape(1, n_p) for k in range(2)], [dest[k, n_p:].reshape(1, n_s) for k in range(2)]]
    xs = _dispatch([u2p_p, u2p_s], dests, dest_pad, n_rows)
    yo = _experts(xs, blk_e, blk_first, n_used, w_exp_gate[l], w_exp_up[l], w_exp_down[l])
    yg_p, yg_s = _collect(yo, dests)

    fg = final_g.reshape(1, D_MODEL)
    wts = route[2:4].T
    y_prompt = _combine(x1p, yg_p, wts[:n_p], ga2p, fg, *tile_p)
    y_sample = _combine(x1s, yg_s, wts[n_p:], ga2s, fg, *tile_s)
    return (y_prompt, y_sample) + leaves_p + leaves_s
```

```python
import functools
import math

import jax
import jax.numpy as jnp
from jax import lax
from jax.experimental import pallas as pl
from jax.experimental.pallas import tpu as pltpu
from jax.experimental.pallas import tpu_sc as plsc

F32 = jnp.float32
BF16 = jnp.bfloat16
U32 = jnp.uint32

D_MODEL = 1024
N_HEADS = 8
HEAD_DIM = 64
W_ATT = N_HEADS * HEAD_DIM
W_REC = D_MODEL - W_ATT
N_REC_BLOCKS = 8
REC_BLOCK = W_REC // N_REC_BLOCKS
CONV_W = 4
RG_C = 8.0
N_GROUPS = 4
EXPERTS_PER_GROUP = 8
N_EXPERTS = N_GROUPS * EXPERTS_PER_GROUP
D_EXPERT = 256
EPS = 1e-6

LANES = 128
SUBLANES = 8
LOG2E = math.log2(math.e)
Q_SCALE = LOG2E * HEAD_DIM ** -0.5
NEG_BIG = -1e30
VMEM_LIMIT = 48 * 1024 * 1024

SEQ_ROWS = 64
BATCH_ROWS = 8
CUMSUM_CHUNK = 512
ATTN_TILE = 2048
ATTN_CHUNK = 512
ATTN_UNROLL = 2
PAST_TILE = 1024
ROW_TILE = 512
EXPERT_BLOCK = 512
COMBINE_TILE = 1024
GATHER_WINDOW = 128
PIECE_COLS = 256
N_PIECES = D_MODEL // (2 * PIECE_COLS)
ROUTE_COLS = 128
EXPERT_COL0 = 8


def _split3(x):
    hi = x.astype(BF16)
    r1 = x - hi.astype(F32)
    mid = r1.astype(BF16)
    lo = (r1 - mid.astype(F32)).astype(BF16)
    return hi, mid, lo


def _dot(a, b):
    return jnp.dot(a, b, preferred_element_type=F32)


def _rms(x):
    return x * lax.rsqrt(jnp.mean(x * x, axis=-1, keepdims=True) + EPS)


def _sigmoid(x):
    return 0.5 * jnp.tanh(0.5 * x) + 0.5


def _pack_bf16_pairs(y):
    n = y.shape[-1] // 2
    yb = y.astype(BF16).astype(F32)
    lo = pltpu.bitcast(yb[:, :n], U32)
    hi = pltpu.bitcast(yb[:, n:], U32)
    return (lo >> 16) | hi


def _unpack_bf16_pairs(p):
    lo = pltpu.bitcast(p << 16, F32)
    hi = pltpu.bitcast(p & jnp.uint32(0xFFFF0000), F32)
    return lo, hi


def _pack_pieces(y):
    w = 2 * PIECE_COLS
    return [_pack_bf16_pairs(y[:, i * w:(i + 1) * w]) for i in range(N_PIECES)]


def _unpack_pieces(pieces):
    chunks = []
    for p in pieces:
        chunks.extend(_unpack_bf16_pairs(p))
    return chunks


def _ada_kernel(c_ref, w_ref, b_ref, o_ref):
    c = c_ref[...]
    a = c * jax.nn.sigmoid(c)
    a_hi, a_mid, _ = _split3(a)
    w_hi, w_mid, _ = _split3(w_ref[...])
    o_ref[...] = _dot(a_hi, w_hi) + _dot(a_mid, w_hi) + _dot(a_hi, w_mid) + b_ref[...]


def _ada(c, w_ada, b_ada):
    rows = c.shape[0]
    n = w_ada.shape[1]
    tn = 1536
    return pl.pallas_call(
        _ada_kernel,
        grid=(n // tn,),
        in_specs=[pl.BlockSpec((rows, D_MODEL), lambda j: (0, 0)),
                  pl.BlockSpec((D_MODEL, tn), lambda j: (0, j)),
                  pl.BlockSpec((1, tn), lambda j: (0, j))],
        out_specs=pl.BlockSpec((rows, tn), lambda j: (0, j)),
        out_shape=jax.ShapeDtypeStruct((rows, n), F32),
        compiler_params=pltpu.CompilerParams(vmem_limit_bytes=VMEM_LIMIT),
        name="ada",
    )(c, w_ada, b_ada.reshape(1, n))


def _inproj_kernel(x_ref, g1_ref, sh1_ref, n1_ref, w_ref, bf_ref, cw_ref, cb_ref, wa_ref, ba_ref, wx_ref, bx_ref,
                   lam_ref, grec_ref, cpast_ref, h0_ref,
                   q_ref, k_ref, v_ref, lf_ref, rec_ref, ht_ref, ctail_ref,
                   xbuf, ub_scr, z_scr, xc_scr, pr_scr, pi_scr, a_scr, u_scr, hs_scr, gl_scr, h_scr):
    bb, tm, _ = x_ref.shape
    rows = bb * tm
    n_lane_tiles = W_REC // LANES
    half = W_REC // 2
    o_x = 3 * W_ATT

    @pl.when(pl.program_id(1) == 0)
    def _():
        xbuf[:, 0:SUBLANES, :] = cpast_ref[...]
        h_scr[...] = h0_ref[...]

    mod = n1_ref[...] * g1_ref[...]
    for b in range(bb):
        rb = slice(b * tm, (b + 1) * tm)
        ub_scr[rb, :] = (_rms(x_ref[b]) * mod[b] + sh1_ref[b]).astype(BF16)

    z_scr[...] = _dot(ub_scr[...], w_ref[:, o_x:])

    cw = cw_ref[...]
    for b in range(bb):
        rb = slice(b * tm, (b + 1) * tm)
        xb = z_scr[rb, 0:W_REC]
        xbuf[b, SUBLANES:SUBLANES + tm, :] = xb
        xc = cb_ref[...] + cw[3:4, :] * xb
        for j in range(CONV_W - 1):
            off = SUBLANES - (CONV_W - 1) + j
            xc = xc + cw[j:j + 1, :] * xbuf[b, off:off + tm, :]
        tail = xbuf[b, tm:tm + SUBLANES, :]
        ctail_ref[b] = tail
        xbuf[b, 0:SUBLANES, :] = tail
        xc_scr[rb, :] = xc

    xcb = xc_scr[...].astype(BF16)
    for w_gate, pre in ((wa_ref, pr_scr), (wx_ref, pi_scr)):
        pre[:, :half] = _dot(xcb[:, :half], w_gate[:half, :half])
        pre[:, half:] = _dot(xcb[:, half:], w_gate[half:, half:])

    zq = _dot(ub_scr[...], w_ref[:, :o_x])
    q_ref[...] = (zq[:, 0:W_ATT] * Q_SCALE).reshape(bb, tm, W_ATT).astype(BF16)
    k_ref[...] = zq[:, W_ATT:2 * W_ATT].reshape(bb, tm, W_ATT)
    v_ref[...] = zq[:, 2 * W_ATT:3 * W_ATT].reshape(bb, tm, W_ATT)

    zf = z_scr[:, 2 * W_REC:2 * W_REC + LANES] + bf_ref[...]
    lf = jnp.minimum(zf, 0.0) - jnp.log1p(jnp.exp(-jnp.abs(zf)))
    lf_ref[0, 0] = lf.T[0:N_HEADS, :]

    nlam = -lam_ref[...]
    decay = -RG_C * (jnp.maximum(nlam, 0.0) + jnp.log1p(jnp.exp(-jnp.abs(nlam))))
    for b in range(bb):
        rb = slice(b * tm, (b + 1) * tm)
        r = _sigmoid(pr_scr[rb, :] + ba_ref[...])
        i = _sigmoid(pi_scr[rb, :] + bx_ref[...])
        log_a = decay * r
        t = jnp.tanh(log_a)
        neg_expm1 = -2.0 * t / (1.0 - t)
        root = jnp.where(neg_expm1 > 0.0, neg_expm1 * lax.rsqrt(neg_expm1), 0.0)
        pr_scr[rb, :] = jnp.exp(log_a)
        pi_scr[rb, :] = root * (i * xc_scr[rb, :])
        gb = z_scr[rb, W_REC:2 * W_REC]
        gl_scr[rb, :] = 0.5 * gb * (1.0 + jnp.tanh(math.sqrt(2.0 / math.pi) * (gb + 0.044715 * (gb * gb * gb))))

    def lane_tile(ref, c):
        return ref[:, c * LANES:(c + 1) * LANES].reshape(bb, tm, LANES)

    for c in range(n_lane_tiles):
        a_scr[c] = jnp.swapaxes(lane_tile(pr_scr, c), 0, 1)
        u_scr[c] = jnp.swapaxes(lane_tile(pi_scr, c), 0, 1)
    for c in range(n_lane_tiles):
        h = h_scr[:, c * LANES:(c + 1) * LANES]
        for step in range(tm):
            h = a_scr[c, step] * h + u_scr[c, step]
            hs_scr[c, step] = h
        h_scr[:, c * LANES:(c + 1) * LANES] = h
    ht_ref[...] = h_scr[...]
    for c in range(n_lane_tiles):
        pr_scr[:, c * LANES:(c + 1) * LANES] = jnp.swapaxes(hs_scr[c], 0, 1).reshape(rows, LANES)

    for b in range(bb):
        rb = slice(b * tm, (b + 1) * tm)
        y = gl_scr[rb, :] * pr_scr[rb, :]
        rec_ref[b] = (_rms(y) * grec_ref[...]).astype(BF16)


def _inproj(x, g1, sh1, n1, w_all, b_f, conv_w, conv_b, wa_bd, b_a, wx_bd, b_x, lam, g_rec, conv_past, h0):
    b, t, _ = x.shape
    bb, tm = BATCH_ROWS, SEQ_ROWS
    nb, nt = b // bb, t // tm
    rows = bb * tm
    wcols = w_all.shape[1]
    const2 = lambda shape: pl.BlockSpec(shape, lambda i, j: (0, 0))
    per_b = lambda last: pl.BlockSpec((bb, 1, last), lambda i, j: (i, 0, 0))
    seq = lambda last: pl.BlockSpec((bb, tm, last), lambda i, j: (i, j, 0))
    return pl.pallas_call(
        _inproj_kernel,
        grid=(nb, nt),
        in_specs=[seq(D_MODEL), per_b(D_MODEL), per_b(D_MODEL), const2((1, D_MODEL)),
                  const2((D_MODEL, wcols)), const2((1, LANES)),
                  const2((CONV_W, W_REC)), const2((1, W_REC)),
                  const2((W_REC, W_REC)), const2((1, W_REC)), const2((W_REC, W_REC)), const2((1, W_REC)),
                  const2((1, W_REC)), const2((1, W_REC)),
                  pl.BlockSpec((bb, SUBLANES, W_REC), lambda i, j: (i, 0, 0)),
                  pl.BlockSpec((bb, W_REC), lambda i, j: (i, 0))],
        out_specs=[seq(W_ATT), seq(W_ATT), seq(W_ATT),
                   pl.BlockSpec((1, 1, N_HEADS, rows), lambda i, j: (i, j, 0, 0)),
                   seq(W_REC),
                   pl.BlockSpec((bb, W_REC), lambda i, j: (i, 0)),
                   pl.BlockSpec((bb, SUBLANES, W_REC), lambda i, j: (i, 0, 0))],
        out_shape=[jax.ShapeDtypeStruct((b, t, W_ATT), BF16),
                   jax.ShapeDtypeStruct((b, t, W_ATT), F32),
                   jax.ShapeDtypeStruct((b, t, W_ATT), F32),
                   jax.ShapeDtypeStruct((nb, nt, N_HEADS, rows), F32),
                   jax.ShapeDtypeStruct((b, t, W_REC), BF16),
                   jax.ShapeDtypeStruct((b, W_REC), F32),
                   jax.ShapeDtypeStruct((b, SUBLANES, W_REC), F32)],
        scratch_shapes=[pltpu.VMEM((bb, tm + SUBLANES, W_REC), F32),
                        pltpu.VMEM((rows, D_MODEL), BF16),
                        pltpu.VMEM((rows, 2 * W_REC + LANES), F32),
                        pltpu.VMEM((rows, W_REC), F32),
                        pltpu.VMEM((rows, W_REC), F32),
                        pltpu.VMEM((rows, W_REC), F32),
                        pltpu.VMEM((W_REC // LANES, tm, bb, LANES), F32),
                        pltpu.VMEM((W_REC // LANES, tm, bb, LANES), F32),
                        pltpu.VMEM((W_REC // LANES, tm, bb, LANES), F32),
                        pltpu.VMEM((rows, W_REC), F32),
                        pltpu.VMEM((bb, W_REC), F32)],
        compiler_params=pltpu.CompilerParams(dimension_semantics=("arbitrary", "arbitrary"),
                                             vmem_limit_bytes=VMEM_LIMIT),
        name="inproj",
    )(x, g1, sh1, n1, w_all, b_f, conv_w, conv_b, wa_bd, b_a, wx_bd, b_x, lam, g_rec, conv_past, h0)


def _cumsum_kernel(x_ref, o_ref):
    length = x_ref.shape[2]
    c = CUMSUM_CHUNK
    upper = (lax.broadcasted_iota(jnp.int32, (c, c), 0) <= lax.broadcasted_iota(jnp.int32, (c, c), 1)).astype(BF16)
    carry = jnp.zeros((N_HEADS, 1), F32)
    for j in range(length // c):
        x = x_ref[0, :, j * c:(j + 1) * c]
        hi, mid, lo = _split3(x)
        parts = jnp.concatenate([hi.astype(F32), mid.astype(F32), lo.astype(F32), jnp.zeros_like(x)], axis=0)
        sums = _dot(parts.astype(BF16), upper)
        out = sums[0:8] + sums[8:16] + sums[16:24] + carry
        o_ref[0, :, j * c:(j + 1) * c] = out
        carry = out[:, c - 1:c]


def _cumsum_time(lf_t):
    b, h, length = lf_t.shape
    return pl.pallas_call(
        _cumsum_kernel,
        grid=(b,),
        in_specs=[pl.BlockSpec((1, h, length), lambda i: (i, 0, 0))],
        out_specs=pl.BlockSpec((1, h, length), lambda i: (i, 0, 0)),
        out_shape=jax.ShapeDtypeStruct((b, h, length), F32),
        name="cumsum",
    )(lf_t)


def _attn_tile(qh, kt, vt, bias, m_ref, acc_ref, mask):
    s = lax.dot_general(qh, kt, (((1,), (1,)), ((), ())), preferred_element_type=F32) + bias
    if mask is not None:
        s = jnp.where(mask, s, NEG_BIG)
    width = min(LANES, s.shape[1])
    slabs = [s[:, c * width:(c + 1) * width] for c in range(s.shape[1] // width)]
    part = slabs[0]
    for sl in slabs[1:]:
        part = jnp.maximum(part, sl)
    m = m_ref[...]
    m_new = jnp.maximum(m, jnp.broadcast_to(jnp.max(part, axis=1, keepdims=True), m.shape))
    p = jnp.concatenate([jnp.exp2(sl - m_new[:, :width]) for sl in slabs], axis=1)
    acc_ref[...] = jnp.exp2(m - m_new) * acc_ref[...] + _dot(p.astype(BF16), vt)
    m_ref[...] = m_new


def _attn_kernel(*refs, n_past_tiles, past_tile, tq, ck):
    if n_past_tiles:
        q_ref, kn_ref, vn_ref, fn_ref, kp_ref, vp_ref, fp_ref, o_ref, kb, vb, m_scr, acc_scr = refs
        past = n_past_tiles * past_tile
    else:
        q_ref, kn_ref, vn_ref, fn_ref, o_ref, kb, vb, m_scr, acc_scr = refs
        past = 0
    qi = pl.program_id(2)
    t_new = kn_ref.shape[1]
    chunks_per_tile = tq // ck
    if n_past_tiles:
        k_heads = jnp.swapaxes(kp_ref[0], 0, 1)
        v_heads = jnp.swapaxes(vp_ref[0], 0, 1)
    for pair in range(q_ref.shape[2] // LANES):
        lanes = slice(pair * LANES, (pair + 1) * LANES)

        @pl.when(qi == 0)
        def _():
            def fill(lo, k_rows, v_rows):
                n = k_rows.shape[0]
                first = lax.broadcasted_iota(jnp.int32, (n, LANES), 1) < HEAD_DIM
                v_f32 = v_rows.astype(F32)
                kb[pair, lo:lo + n, :] = k_rows.astype(BF16)
                vb[pair, 0, lo:lo + n, :] = jnp.where(first, v_f32, 1.0).astype(BF16)
                vb[pair, 1, lo:lo + n, :] = jnp.where(first, 1.0, v_f32).astype(BF16)

            if n_past_tiles:
                fill(0, jnp.concatenate([k_heads[2 * pair], k_heads[2 * pair + 1]], axis=1),
                     jnp.concatenate([v_heads[2 * pair], v_heads[2 * pair + 1]], axis=1))
            fill(past, kn_ref[0, :, lanes], vn_ref[0, :, lanes])

        q = q_ref[0, :, lanes]
        first_head = lax.broadcasted_iota(jnp.int32, q.shape, 1) < HEAD_DIM
        qh = (jnp.where(first_head, q, jnp.zeros_like(q)), jnp.where(first_head, jnp.zeros_like(q), q))

        ref_f = fn_ref[0, pair, qi * chunks_per_tile][:, 0:1]
        m_scr[...] = jnp.full(m_scr.shape, NEG_BIG, F32)
        acc_scr[...] = jnp.zeros(acc_scr.shape, F32)

        def tile(start, size, f_tile, row0=0, masked=False):
            kt = kb[pair, pl.ds(start, size), :]
            mask = None
            if masked:
                shape = (tq - row0, size)
                mask = lax.broadcasted_iota(jnp.int32, shape, 1) <= lax.broadcasted_iota(jnp.int32, shape, 0)
            for hd in range(2):
                bias = (ref_f[hd:hd + 1, :] - f_tile[hd:hd + 1, :]) * LOG2E
                _attn_tile(qh[hd][row0:, :], kt, vb[pair, hd, pl.ds(start, size), :], bias,
                           m_scr.at[hd, row0:tq], acc_scr.at[hd, row0:tq], mask)

        for j in range(n_past_tiles):
            tile(j * past_tile, past_tile, fp_ref[0, pair, j])

        if tq < t_new:
            n_chunks = qi * chunks_per_tile

            def chunk(j):
                tile(pl.multiple_of(past + j * ck, ck), ck, fn_ref[0, pair, j])

            def chunk_body(jj, carry):
                for u in range(ATTN_UNROLL):
                    chunk(jj * ATTN_UNROLL + u)
                return carry

            lax.fori_loop(0, n_chunks // ATTN_UNROLL, chunk_body, 0)
            for r in range(ATTN_UNROLL - 1):
                @pl.when(r < n_chunks % ATTN_UNROLL)
                def _():
                    chunk((n_chunks // ATTN_UNROLL) * ATTN_UNROLL + r)

        for c in range(chunks_per_tile):
            j = qi * chunks_per_tile + c
            tile(pl.multiple_of(past + j * ck, ck), ck, fn_ref[0, pair, j], row0=c * ck, masked=True)

        outs = []
        for hd in range(2):
            acc = acc_scr[hd]
            outs.append(acc / pltpu.roll(acc, HEAD_DIM, axis=1))
        o_ref[0, :, lanes] = jnp.where(first_head, outs[0], outs[1]).astype(BF16)


def _attention(q, k_new, v_new, f_new, k_past=None, v_past=None, f_past=None):
    b, t, _ = q.shape
    tq = min(t, ATTN_TILE)
    nq = t // tq
    pp = 1 if nq > 1 else N_HEADS // 2
    pairs = N_HEADS // 2 // pp
    width = pp * LANES
    q_spec = pl.BlockSpec((1, tq, width), lambda i, p, j: (i, j, p))
    full = lambda rows: pl.BlockSpec((1, rows, width), lambda i, p, j: (i, 0, p))
    f_spec = lambda arr: pl.BlockSpec((1, pp) + arr.shape[2:], lambda i, p, j: (i, p, 0, 0, 0))
    in_specs = [q_spec, full(t), full(t), f_spec(f_new)]
    args = [q, k_new, v_new, f_new]
    past = 0
    n_past_tiles = 0
    past_tile = 0
    if k_past is not None:
        assert pairs == 1, "cached rows are read with all heads in one block"
        past = k_past.shape[1]
        past_tile = f_past.shape[-1]
        n_past_tiles = past // past_tile
        cache_spec = pl.BlockSpec((1, past, N_HEADS, HEAD_DIM), lambda i, p, j: (i, 0, 0, 0))
        in_specs += [cache_spec, cache_spec, f_spec(f_past)]
        args += [k_past, v_past, f_past]
    ck = f_new.shape[-1]
    return pl.pallas_call(
        functools.partial(_attn_kernel, n_past_tiles=n_past_tiles, past_tile=past_tile, tq=tq, ck=ck),
        grid=(b, pairs, nq),
        in_specs=in_specs,
        out_specs=q_spec,
        out_shape=jax.ShapeDtypeStruct((b, t, W_ATT), BF16),
        scratch_shapes=[pltpu.VMEM((pp, past + t, LANES), BF16), pltpu.VMEM((pp, 2, past + t, LANES), BF16),
                        pltpu.VMEM((2, tq, LANES), F32), pltpu.VMEM((2, tq, LANES), F32)],
        compiler_params=pltpu.CompilerParams(dimension_semantics=("arbitrary", "arbitrary", "arbitrary"),
                                             vmem_limit_bytes=VMEM_LIMIT),
        name="attn",
    )(*args)


def _outproj_kernel(attn_ref, rec_ref, x_ref, ga1_ref, g2_ref, sh2_ref, n2_ref, gatt_ref, woa_ref, wor_ref,
                    wrh_ref, wrm_ref, br_ref,
                    x1_ref, u2p_ref, route_ref):
    bb, tm, _ = x_ref.shape
    rows = bb * tm
    attn = attn_ref[...].astype(F32)
    an = (_rms(attn) * gatt_ref[...]).reshape(rows, W_ATT).astype(BF16)
    mix = _dot(an, woa_ref[...]) + _dot(rec_ref[...].reshape(rows, W_REC), wor_ref[...])
    x1 = x_ref[...] + ga1_ref[...] * mix.reshape(bb, tm, D_MODEL)
    x1_ref[...] = x1
    u2 = (_rms(x1) * (n2_ref[...] * g2_ref[...]) + sh2_ref[...]).reshape(rows, D_MODEL)
    for i, piece in enumerate(_pack_pieces(u2)):
        u2p_ref[i] = piece.reshape(bb, tm, PIECE_COLS)

    u_hi, u_mid, _ = _split3(u2)
    logits = _dot(u_hi, wrh_ref[...]) + _dot(u_mid, wrh_ref[...]) + _dot(u_hi, wrm_ref[...]) + br_ref[...]
    lt = logits.T
    row8 = lax.broadcasted_iota(jnp.int32, (SUBLANES, rows), 0).astype(F32)
    lg = lt[0:SUBLANES]
    m_g = jnp.max(lg, axis=0, keepdims=True)
    gidx = jnp.min(jnp.where(lg == m_g, row8, float(SUBLANES)), axis=0, keepdims=True)
    p_top = 1.0 / jnp.sum(jnp.exp(lg - m_g), axis=0, keepdims=True)
    leg = jnp.zeros((EXPERTS_PER_GROUP, rows), F32)
    for g in range(N_GROUPS):
        lo = EXPERT_COL0 + g * EXPERTS_PER_GROUP
        leg = jnp.where(gidx == float(g), lt[lo:lo + EXPERTS_PER_GROUP], leg)
    v1 = jnp.max(leg, axis=0, keepdims=True)
    i1 = jnp.min(jnp.where(leg == v1, row8, float(SUBLANES)), axis=0, keepdims=True)
    leg2 = jnp.where(row8 == i1, -jnp.inf, leg)
    v2 = jnp.max(leg2, axis=0, keepdims=True)
    i2 = jnp.min(jnp.where(leg2 == v2, row8, float(SUBLANES)), axis=0, keepdims=True)
    e21 = jnp.exp(v2 - v1)
    w1 = p_top / (1.0 + e21)
    w2 = w1 * e21
    base = gidx * float(EXPERTS_PER_GROUP)
    out = jnp.where(row8 == 3.0, w2, 0.0)
    for r_idx, val in ((2.0, w1), (1.0, base + i2), (0.0, base + i1)):
        out = jnp.where(row8 == r_idx, val, out)
    route_ref[0, 0] = out


def _outproj(attn, rec, x, ga1, g2, sh2, n2, g_att, wo_a, wo_r, wr_hi, wr_mid, b_r, bb, tm):
    b, t, _ = x.shape
    nb, nt = b // bb, t // tm
    rows = bb * tm
    const2 = lambda shape: pl.BlockSpec(shape, lambda i, j: (0, 0))
    per_b = pl.BlockSpec((bb, 1, D_MODEL), lambda i, j: (i, 0, 0))
    seq = lambda last: pl.BlockSpec((bb, tm, last), lambda i, j: (i, j, 0))
    return pl.pallas_call(
        _outproj_kernel,
        grid=(nb, nt),
        in_specs=[seq(W_ATT), seq(W_REC), seq(D_MODEL), per_b, per_b, per_b, const2((1, D_MODEL)),
                  const2((1, W_ATT)), const2((W_ATT, D_MODEL)), const2((W_REC, D_MODEL)),
                  const2((D_MODEL, ROUTE_COLS)), const2((D_MODEL, ROUTE_COLS)), const2((1, ROUTE_COLS))],
        out_specs=[seq(D_MODEL),
                   pl.BlockSpec((N_PIECES, bb, tm, PIECE_COLS), lambda i, j: (0, i, j, 0)),
                   pl.BlockSpec((1, 1, SUBLANES, rows), lambda i, j: (i, j, 0, 0))],
        out_shape=[jax.ShapeDtypeStruct((b, t, D_MODEL), F32),
                   jax.ShapeDtypeStruct((N_PIECES, b, t, PIECE_COLS), U32),
                   jax.ShapeDtypeStruct((nb, nt, SUBLANES, rows), F32)],
        compiler_params=pltpu.CompilerParams(dimension_semantics=("arbitrary", "arbitrary"),
                                             vmem_limit_bytes=VMEM_LIMIT),
        name="outproj",
    )(attn, rec, x, ga1, g2, sh2, n2, g_att, wo_a, wo_r, wr_hi, wr_mid, b_r)


_SC_AXES = ("core", "subcore")


def _sc_mesh():
    return plsc.VectorSubcoreMesh(core_axis_name=_SC_AXES[0], subcore_axis_name=_SC_AXES[1])


def _sc_scatter_rows(src_hbm, idx_hbm, n, dst_hbm):
    def body(x_vmem, i_vmem):
        pltpu.sync_copy(x_vmem, dst_hbm.at[i_vmem.at[0]])

    pltpu.emit_pipeline(
        body,
        grid=(n // GATHER_WINDOW,),
        in_specs=[pl.BlockSpec((GATHER_WINDOW, PIECE_COLS), lambda i: (i, 0)),
                  pl.BlockSpec((1, GATHER_WINDOW), lambda i: (0, i))],
        out_specs=[],
        core_axis_name=_SC_AXES,
        dimension_semantics=(pltpu.PARALLEL,),
    )(src_hbm, idx_hbm)


def _sc_gather_rows(tab_hbm, idx_hbm, n, dst_hbm):
    def body(i_vmem, o_vmem):
        pltpu.sync_copy(tab_hbm.at[i_vmem.at[0]], o_vmem)

    pltpu.emit_pipeline(
        body,
        grid=(n // GATHER_WINDOW,),
        in_specs=[pl.BlockSpec((1, GATHER_WINDOW), lambda i: (0, i))],
        out_specs=[pl.BlockSpec((GATHER_WINDOW, PIECE_COLS), lambda i: (i, 0))],
        core_axis_name=_SC_AXES,
        dimension_semantics=(pltpu.PARALLEL,),
    )(idx_hbm, dst_hbm)


def _dispatch(tables, dests, dest_pad, n_rows):
    n_pad = dest_pad.shape[1]
    sizes = [t.shape[1] for t in tables]
    n_groups = len(tables)

    @pl.kernel(out_type=jax.ShapeDtypeStruct((N_PIECES, n_rows, PIECE_COLS), U32), mesh=_sc_mesh(), scratch_types=[])
    def scatter(*refs):
        tabs = refs[:n_groups]
        idxs = refs[n_groups:3 * n_groups]
        pad_hbm, o_hbm = refs[3 * n_groups], refs[3 * n_groups + 1]
        for piece in range(N_PIECES):
            dst = o_hbm.at[piece]
            for g in range(n_groups):
                for k in range(2):
                    _sc_scatter_rows(tabs[g].at[piece], idxs[2 * g + k], sizes[g], dst)
            _sc_scatter_rows(tabs[0].at[piece], pad_hbm, n_pad, dst)

    flat_idx = [d for pair in dests for d in pair]
    return scatter(*tables, *flat_idx, dest_pad)


def _collect(yo, dests):
    sizes = [pair[0].shape[1] for pair in dests]
    n_groups = len(dests)
    out_type = [jax.ShapeDtypeStruct((2, N_PIECES, n, PIECE_COLS), U32) for n in sizes]

    @pl.kernel(out_type=out_type, mesh=_sc_mesh(), scratch_types=[])
    def gather(*refs):
        yo_hbm = refs[0]
        idxs = refs[1:1 + 2 * n_groups]
        outs = refs[1 + 2 * n_groups:]
        for piece in range(N_PIECES):
            for g in range(n_groups):
                for k in range(2):
                    _sc_gather_rows(yo_hbm.at[piece], idxs[2 * g + k], sizes[g], outs[g].at[k, piece])

    flat_idx = [d for pair in dests for d in pair]
    return gather(yo, *flat_idx)


def _expert_kernel(blk_e_ref, blk_first_ref, n_used_ref, xs_ref, wg_ref, wu_ref, wd_ref, yo_ref, wg_b, wu_b, wd_b):
    b = pl.program_id(0)

    @pl.when(blk_first_ref[b] == 1)
    def _():
        wg_b[...] = wg_ref[0].astype(BF16)
        wu_b[...] = wu_ref[0].astype(BF16)
        wd_b[...] = wd_ref[0].astype(BF16)

    @pl.when(b < n_used_ref[0])
    def _():
        chunks = [c.astype(BF16) for c in _unpack_pieces([xs_ref[i] for i in range(N_PIECES)])]
        g = None
        u = None
        for i, xc in enumerate(chunks):
            rows_i = slice(i * PIECE_COLS, (i + 1) * PIECE_COLS)
            gi = _dot(xc, wg_b[rows_i, :])
            ui = _dot(xc, wu_b[rows_i, :])
            g = gi if g is None else g + gi
            u = ui if u is None else u + ui
        h = (g * _sigmoid(g)) * u
        for i, piece in enumerate(_pack_pieces(_dot(h.astype(BF16), wd_b[...]))):
            yo_ref[i] = piece

    @pl.when(b >= n_used_ref[0])
    def _():
        yo_ref[...] = jnp.zeros(yo_ref.shape, U32)


def _experts(xs, blk_e, blk_first, n_used, w_gate, w_up, w_down):
    p = xs.shape[1]
    tb = EXPERT_BLOCK
    nblk = p // tb
    row_block = pl.BlockSpec((N_PIECES, tb, PIECE_COLS), lambda i, e, f, n: (0, i, 0))
    grid_spec = pltpu.PrefetchScalarGridSpec(
        num_scalar_prefetch=3,
        grid=(nblk,),
        in_specs=[row_block,
                  pl.BlockSpec((1, D_MODEL, D_EXPERT), lambda i, e, f, n: (e[i], 0, 0)),
                  pl.BlockSpec((1, D_MODEL, D_EXPERT), lambda i, e, f, n: (e[i], 0, 0)),
                  pl.BlockSpec((1, D_EXPERT, D_MODEL), lambda i, e, f, n: (e[i], 0, 0))],
        out_specs=row_block,
        scratch_shapes=[pltpu.VMEM((D_MODEL, D_EXPERT), BF16),
                        pltpu.VMEM((D_MODEL, D_EXPERT), BF16),
                        pltpu.VMEM((D_EXPERT, D_MODEL), BF16)],
    )
    return pl.pallas_call(
        _expert_kernel,
        grid_spec=grid_spec,
        out_shape=jax.ShapeDtypeStruct((N_PIECES, p, PIECE_COLS), U32),
        compiler_params=pltpu.CompilerParams(dimension_semantics=("arbitrary",), vmem_limit_bytes=VMEM_LIMIT),
        name="experts",
    )(blk_e, blk_first, n_used, xs, w_gate, w_up, w_down)


def _combine_kernel(x1_ref, yg_ref, w_ref, ga2_ref, fg_ref, o_ref):
    bb, tm, _ = x1_ref.shape
    w = w_ref[...]
    y = None
    for k in range(2):
        chunks = _unpack_pieces([yg_ref[k, i] for i in range(N_PIECES)])
        yk = w[:, k:k + 1] * jnp.concatenate(chunks, axis=1)
        y = yk if y is None else y + yk
    out = x1_ref[...] + ga2_ref[...] * y.reshape(bb, tm, D_MODEL)
    o_ref[...] = _rms(out) * fg_ref[...]


def _combine(x1, yg, wts, ga2, final_g, bb, tm):
    b, t, _ = x1.shape
    nb, nt = b // bb, t // tm
    rows = bb * tm
    return pl.pallas_call(
        _combine_kernel,
        grid=(nb, nt),
        in_specs=[pl.BlockSpec((bb, tm, D_MODEL), lambda i, j: (i, j, 0)),
                  pl.BlockSpec((2, N_PIECES, rows, PIECE_COLS), lambda i, j: (0, 0, i * nt + j, 0)),
                  pl.BlockSpec((rows, 2), lambda i, j: (i * nt + j, 0)),
                  pl.BlockSpec((bb, 1, D_MODEL), lambda i, j: (i, 0, 0)),
                  pl.BlockSpec((1, D_MODEL), lambda i, j: (0, 0))],
        out_specs=pl.BlockSpec((bb, tm, D_MODEL), lambda i, j: (i, j, 0)),
        out_shape=jax.ShapeDtypeStruct((b, t, D_MODEL), F32),
        compiler_params=pltpu.CompilerParams(dimension_semantics=("arbitrary", "arbitrary"),
                                             vmem_limit_bytes=VMEM_LIMIT),
        name="combine",
    )(x1, yg, wts, ga2, final_g)


def _block_diag(w):
    n, k, _ = w.shape
    eye = jnp.eye(n, dtype=w.dtype)
    return (eye[:, None, :, None] * w[:, :, None, :]).reshape(n * k, n * k)


def _tiles(f, tile):
    b, _, length = f.shape
    return f.reshape(b, N_HEADS // 2, 2, length // tile, tile).transpose(0, 1, 3, 2, 4)


def _untile_rows(a, b, t):
    nb, nt, r, rows = a.shape
    bb = b // nb
    tm = t // nt
    return a.reshape(nb, nt, r, bb, tm).transpose(2, 0, 3, 1, 4).reshape(r, b, t)


def _group_front(x, c_mod, layer, conv_past, h0, cache):
    b, t, _ = x.shape
    sh1, sc1, ga1, sh2, sc2, ga2 = [m.reshape(b, 1, D_MODEL) for m in jnp.split(c_mod, 6, axis=-1)]
    q, k, v, lf_steps, rec, h_t, ctail = _inproj(
        x, 1.0 + sc1, sh1, layer["n1"], layer["w_all"], layer["b_f"], layer["conv_w"], layer["conv_b"],
        layer["wa_bd"], layer["b_a"], layer["wx_bd"], layer["b_x"], layer["lam"], layer["g_rec"], conv_past, h0)
    lf_t = _untile_rows(lf_steps, b, t).transpose(1, 0, 2)
    ck = min(t, ATTN_CHUNK)
    if cache is None:
        f_all = _cumsum_time(lf_t)
        attn = _attention(q, k, v, _tiles(f_all, ck))
    else:
        k_past, v_past, lf_past = cache
        past = k_past.shape[1]
        total = past + t
        padded = -(-total // CUMSUM_CHUNK) * CUMSUM_CHUNK
        lf_all = jnp.concatenate([lf_past.transpose(0, 2, 1), lf_t,
                                  jnp.zeros((b, N_HEADS, padded - total), F32)], axis=2)
        f_all = _cumsum_time(lf_all)
        attn = _attention(q, k, v, _tiles(f_all[:, :, past:total], ck),
                          k_past, v_past,
                          _tiles(f_all[:, :, :past], PAST_TILE))
    bb, tm = (1, ROW_TILE) if t >= ROW_TILE else (ROW_TILE // t, t)
    x1, u2p, route = _outproj(attn, rec, x, ga1, 1.0 + sc2, sh2, layer["n2"], layer["g_att"],
                              layer["wo_a"], layer["wo_r"], layer["wr_hi"], layer["wr_mid"], layer["b_r"], bb, tm)
    route = _untile_rows(route, b, t).reshape(SUBLANES, b * t)
    leaves = (k.reshape(1, b, t, N_HEADS, HEAD_DIM), v.reshape(1, b, t, N_HEADS, HEAD_DIM),
              lf_t.transpose(0, 2, 1)[None], h_t[None], ctail[None, :, SUBLANES - (CONV_W - 1):, :])
    combine_tile = (1, COMBINE_TILE) if t >= COMBINE_TILE else (bb, tm)
    return x1, u2p.reshape(N_PIECES, b * t, PIECE_COLS), route, ga2, combine_tile, leaves


def kernel(x_prompt, x_sample, c_prompt, c_sample, cache_k, cache_v, cache_logf, state_h, state_conv, norm1_g, norm2_g, w_ada, b_ada, w_in, b_f, conv_w, conv_b, w_rg_a, b_rg_a, w_rg_x, b_rg_x, rg_lambda, out_g_att, out_g_rec, w_out, w_route_group, b_route_group, w_route_expert, b_route_expert, w_exp_gate, w_exp_up, w_exp_down, final_g):
    bp, tp, _ = x_prompt.shape
    bs, ts, _ = x_sample.shape
    l = 0
    o1, o2, o3 = W_ATT, 2 * W_ATT, 3 * W_ATT
    o4 = o3 + N_HEADS
    o5 = o4 + W_REC
    w_in_l = w_in[l]
    w_all = jnp.concatenate([w_in_l[:, :o3], w_in_l[:, o4:], w_in_l[:, o3:o4],
                             jnp.zeros((D_MODEL, LANES - N_HEADS), F32)], axis=1).astype(BF16)
    w_r = jnp.zeros((D_MODEL, ROUTE_COLS), F32)
    w_r = w_r.at[:, 0:N_GROUPS].set(w_route_group[l]).at[:, EXPERT_COL0:EXPERT_COL0 + N_EXPERTS].set(w_route_expert[l])
    wr_hi = w_r.astype(BF16)
    wr_mid = (w_r - wr_hi.astype(F32)).astype(BF16)
    b_r = jnp.zeros((1, ROUTE_COLS), F32)
    b_r = b_r.at[0, 0:N_GROUPS].set(b_route_group[l]).at[0, N_GROUPS:SUBLANES].set(NEG_BIG)
    b_r = b_r.at[0, EXPERT_COL0:EXPERT_COL0 + N_EXPERTS].set(b_route_expert[l])
    layer = {
        "n1": norm1_g[l].reshape(1, D_MODEL), "n2": norm2_g[l].reshape(1, D_MODEL),
        "w_all": w_all,
        "b_f": jnp.concatenate([b_f[l], jnp.zeros((LANES - N_HEADS,), F32)]).reshape(1, LANES),
        "conv_w": conv_w[l], "conv_b": conv_b[l].reshape(1, W_REC),
        "wa_bd": _block_diag(w_rg_a[l]).astype(BF16), "b_a": b_rg_a[l].reshape(1, W_REC),
        "wx_bd": _block_diag(w_rg_x[l]).astype(BF16), "b_x": b_rg_x[l].reshape(1, W_REC),
        "lam": rg_lambda[l].reshape(1, W_REC), "g_rec": out_g_rec[l].reshape(1, W_REC),
        "g_att": out_g_att[l].reshape(1, W_ATT),
        "wo_a": w_out[l, :W_ATT].astype(BF16), "wo_r": w_out[l, W_ATT:].astype(BF16),
        "wr_hi": wr_hi, "wr_mid": wr_mid, "b_r": b_r,
    }

    ada = _ada(jnp.concatenate([c_prompt, c_sample], axis=0), w_ada[l], b_ada[l])
    pad_rows = SUBLANES - (CONV_W - 1)
    x1p, u2p_p, route_p, ga2p, tile_p, leaves_p = _group_front(
        x_prompt, ada[:bp], layer, jnp.zeros((bp, SUBLANES, W_REC), F32), jnp.zeros((bp, W_REC), F32), None)
    conv_past_s = jnp.concatenate([jnp.zeros((bs, pad_rows, W_REC), F32), state_conv[l]], axis=1)
    x1s, u2p_s, route_s, ga2s, tile_s, leaves_s = _group_front(
        x_sample, ada[bp:], layer, conv_past_s, state_h[l], (cache_k[l], cache_v[l], cache_logf[l]))

    n_p, n_s = bp * tp, bs * ts
    n_tok = n_p + n_s
    route = jnp.concatenate([route_p, route_s], axis=1)
    eid = route[0:2].astype(jnp.int32).reshape(-1)
    n_slot = 2 * n_tok
    tb = EXPERT_BLOCK
    experts = jnp.arange(N_EXPERTS, dtype=jnp.int32)
    onehot = (eid[:, None] == experts[None, :]).astype(jnp.int32)
    csum = jnp.cumsum(onehot, axis=0)
    counts = csum[-1]
    rank = jnp.sum(csum * onehot, axis=1) - 1
    padded = ((counts + tb - 1) // tb) * tb
    pend = jnp.cumsum(padded)
    pstart = pend - padded
    dest = jnp.sum(onehot * pstart[None, :], axis=1) + rank
    nblk = -(-n_slot // tb) + N_EXPERTS
    n_rows = nblk * tb
    blk_row0 = jnp.arange(nblk, dtype=jnp.int32) * tb
    blk_e = jnp.minimum(jnp.sum((pend[None, :] <= blk_row0[:, None]).astype(jnp.int32), axis=1), N_EXPERTS - 1)
    blk_first = jnp.concatenate([jnp.ones((1,), jnp.int32), (blk_e[1:] != blk_e[:-1]).astype(jnp.int32)])
    n_used = (pend[-1:] // tb).astype(jnp.int32)
    pad_e = padded - counts
    cpad = jnp.cumsum(pad_e)
    j = jnp.arange(n_rows - n_slot, dtype=jnp.int32)
    owner = (cpad[None, :] <= j[:, None]).astype(jnp.int32)
    e_j = jnp.minimum(jnp.sum(owner, axis=1), N_EXPERTS - 1)
    sel = (e_j[:, None] == experts[None, :]).astype(jnp.int32)
    in_expert = jnp.sum(sel * (pstart + counts - (cpad - pad_e))[None, :], axis=1) + j
    dest_pad = jnp.where(j < cpad[-1], in_expert, pend[-1] + (j - cpad[-1])).reshape(1, -1)

    dest = dest.reshape(2, n_tok)
    dests = [[dest[k, :n_p].reshape(1, n_p) for k in range(2)], [dest[k, n_p:].reshape(1, n_s) for k in range(2)]]
    xs = _dispatch([u2p_p, u2p_s], dests, dest_pad, n_rows)
    yo = _experts(xs, blk_e, blk_first, n_used, w_exp_gate[l], w_exp_up[l], w_exp_down[l])
    yg_p, yg_s = _collect(yo, dests)

    fg = final_g.reshape(1, D_MODEL)
    wts = route[2:4].T
    y_prompt = _combine(x1p, yg_p, wts[:n_p], ga2p, fg, *tile_p)
    y_sample = _combine(x1s, yg_s, wts[n_p:], ga2s, fg, *tile_s)
    return (y_prompt, y_sample) + leaves_p + leaves_s
```

```python
import functools
import math

import jax
import jax.numpy as jnp
from jax import lax
from jax.experimental import pallas as pl
from jax.experimental.pallas import tpu as pltpu
from jax.experimental.pallas import tpu_sc as plsc

F32 = jnp.float32
BF16 = jnp.bfloat16
U32 = jnp.uint32

D_MODEL = 1024
N_HEADS = 8
HEAD_DIM = 64
W_ATT = N_HEADS * HEAD_DIM
W_REC = D_MODEL - W_ATT
N_REC_BLOCKS = 8
REC_BLOCK = W_REC // N_REC_BLOCKS
CONV_W = 4
RG_C = 8.0
N_GROUPS = 4
EXPERTS_PER_GROUP = 8
N_EXPERTS = N_GROUPS * EXPERTS_PER_GROUP
D_EXPERT = 256
EPS = 1e-6

LANES = 128
SUBLANES = 8
LOG2E = math.log2(math.e)
Q_SCALE = LOG2E * HEAD_DIM ** -0.5
NEG_BIG = -1e30
VMEM_LIMIT = 48 * 1024 * 1024

SEQ_ROWS = 64
BATCH_ROWS = 8
CUMSUM_CHUNK = 512
ATTN_TILE = 2048
ATTN_CHUNK = 512
ATTN_UNROLL = 2
ROW_TILE = 512
EXPERT_BLOCK = 512
COMBINE_TILE = 1024
GATHER_WINDOW = 128
PIECE_COLS = 256
N_PIECES = D_MODEL // (2 * PIECE_COLS)
ROUTE_COLS = 128
EXPERT_COL0 = 8


def _split3(x):
    hi = x.astype(BF16)
    r1 = x - hi.astype(F32)
    mid = r1.astype(BF16)
    lo = (r1 - mid.astype(F32)).astype(BF16)
    return hi, mid, lo


def _dot(a, b):
    return jnp.dot(a, b, preferred_element_type=F32)


def _rms(x):
    return x * lax.rsqrt(jnp.mean(x * x, axis=-1, keepdims=True) + EPS)


def _sigmoid(x):
    return 0.5 * jnp.tanh(0.5 * x) + 0.5


def _pack_bf16_pairs(y):
    n = y.shape[-1] // 2
    yb = y.astype(BF16).astype(F32)
    lo = pltpu.bitcast(yb[:, :n], U32)
    hi = pltpu.bitcast(yb[:, n:], U32)
    return (lo >> 16) | hi


def _unpack_bf16_pairs(p):
    lo = pltpu.bitcast(p << 16, F32)
    hi = pltpu.bitcast(p & jnp.uint32(0xFFFF0000), F32)
    return lo, hi


def _pack_pieces(y):
    w = 2 * PIECE_COLS
    return [_pack_bf16_pairs(y[:, i * w:(i + 1) * w]) for i in range(N_PIECES)]


def _unpack_pieces(pieces):
    chunks = []
    for p in pieces:
        chunks.extend(_unpack_bf16_pairs(p))
    return chunks


def _ada_kernel(c_ref, w_ref, b_ref, o_ref):
    c = c_ref[...]
    a = c * jax.nn.sigmoid(c)
    a_hi, a_mid, _ = _split3(a)
    w_hi, w_mid, _ = _split3(w_ref[...])
    o_ref[...] = _dot(a_hi, w_hi) + _dot(a_mid, w_hi) + _dot(a_hi, w_mid) + b_ref[...]


def _ada(c, w_ada, b_ada):
    rows = c.shape[0]
    n = w_ada.shape[1]
    tn = 1536
    return pl.pallas_call(
        _ada_kernel,
        grid=(n // tn,),
        in_specs=[pl.BlockSpec((rows, D_MODEL), lambda j: (0, 0)),
                  pl.BlockSpec((D_MODEL, tn), lambda j: (0, j)),
                  pl.BlockSpec((1, tn), lambda j: (0, j))],
        out_specs=pl.BlockSpec((rows, tn), lambda j: (0, j)),
        out_shape=jax.ShapeDtypeStruct((rows, n), F32),
        compiler_params=pltpu.CompilerParams(vmem_limit_bytes=VMEM_LIMIT),
        name="ada",
    )(c, w_ada, b_ada.reshape(1, n))


def _inproj_kernel(x_ref, g1_ref, sh1_ref, n1_ref, w_ref, bf_ref, cw_ref, cb_ref, wa_ref, ba_ref, wx_ref, bx_ref,
                   lam_ref, grec_ref, cpast_ref, h0_ref,
                   q_ref, k_ref, v_ref, lf_ref, rec_ref, ht_ref, ctail_ref,
                   xbuf, ub_scr, z_scr, xc_scr, pr_scr, pi_scr, a_scr, u_scr, hs_scr, gl_scr, h_scr):
    bb, tm, _ = x_ref.shape
    rows = bb * tm
    n_lane_tiles = W_REC // LANES
    half = W_REC // 2
    o_x = 3 * W_ATT

    @pl.when(pl.program_id(1) == 0)
    def _():
        xbuf[:, 0:SUBLANES, :] = cpast_ref[...]
        h_scr[...] = h0_ref[...]

    mod = n1_ref[...] * g1_ref[...]
    for b in range(bb):
        rb = slice(b * tm, (b + 1) * tm)
        ub_scr[rb, :] = (_rms(x_ref[b]) * mod[b] + sh1_ref[b]).astype(BF16)

    z_scr[...] = _dot(ub_scr[...], w_ref[:, o_x:])

    cw = cw_ref[...]
    for b in range(bb):
        rb = slice(b * tm, (b + 1) * tm)
        xb = z_scr[rb, 0:W_REC]
        xbuf[b, SUBLANES:SUBLANES + tm, :] = xb
        xc = cb_ref[...] + cw[3:4, :] * xb
        for j in range(CONV_W - 1):
            off = SUBLANES - (CONV_W - 1) + j
            xc = xc + cw[j:j + 1, :] * xbuf[b, off:off + tm, :]
        tail = xbuf[b, tm:tm + SUBLANES, :]
        ctail_ref[b] = tail
        xbuf[b, 0:SUBLANES, :] = tail
        xc_scr[rb, :] = xc

    xcb = xc_scr[...].astype(BF16)
    for w_gate, pre in ((wa_ref, pr_scr), (wx_ref, pi_scr)):
        pre[:, :half] = _dot(xcb[:, :half], w_gate[:half, :half])
        pre[:, half:] = _dot(xcb[:, half:], w_gate[half:, half:])

    zq = _dot(ub_scr[...], w_ref[:, :o_x])
    q_ref[...] = (zq[:, 0:W_ATT] * Q_SCALE).reshape(bb, tm, W_ATT).astype(BF16)
    k_ref[...] = zq[:, W_ATT:2 * W_ATT].reshape(bb, tm, W_ATT)
    v_ref[...] = zq[:, 2 * W_ATT:3 * W_ATT].reshape(bb, tm, W_ATT)

    zf = z_scr[:, 2 * W_REC:2 * W_REC + LANES] + bf_ref[...]
    lf = jnp.minimum(zf, 0.0) - jnp.log1p(jnp.exp(-jnp.abs(zf)))
    lf_ref[0, 0] = lf.T[0:N_HEADS, :]

    nlam = -lam_ref[...]
    decay = -RG_C * (jnp.maximum(nlam, 0.0) + jnp.log1p(jnp.exp(-jnp.abs(nlam))))
    for b in range(bb):
        rb = slice(b * tm, (b + 1) * tm)
        r = _sigmoid(pr_scr[rb, :] + ba_ref[...])
        i = _sigmoid(pi_scr[rb, :] + bx_ref[...])
        log_a = decay * r
        t = jnp.tanh(log_a)
        neg_expm1 = -2.0 * t / (1.0 - t)
        root = jnp.where(neg_expm1 > 0.0, neg_expm1 * lax.rsqrt(neg_expm1), 0.0)
        pr_scr[rb, :] = jnp.exp(log_a)
        pi_scr[rb, :] = root * (i * xc_scr[rb, :])
        gb = z_scr[rb, W_REC:2 * W_REC]
        gl_scr[rb, :] = 0.5 * gb * (1.0 + jnp.tanh(math.sqrt(2.0 / math.pi) * (gb + 0.044715 * (gb * gb * gb))))

    def lane_tile(ref, c):
        return ref[:, c * LANES:(c + 1) * LANES].reshape(bb, tm, LANES)

    for c in range(n_lane_tiles):
        a_scr[c] = jnp.swapaxes(lane_tile(pr_scr, c), 0, 1)
        u_scr[c] = jnp.swapaxes(lane_tile(pi_scr, c), 0, 1)
    for c in range(n_lane_tiles):
        h = h_scr[:, c * LANES:(c + 1) * LANES]
        for step in range(tm):
            h = a_scr[c, step] * h + u_scr[c, step]
            hs_scr[c, step] = h
        h_scr[:, c * LANES:(c + 1) * LANES] = h
    ht_ref[...] = h_scr[...]
    for c in range(n_lane_tiles):
        pr_scr[:, c * LANES:(c + 1) * LANES] = jnp.swapaxes(hs_scr[c], 0, 1).reshape(rows, LANES)

    for b in range(bb):
        rb = slice(b * tm, (b + 1) * tm)
        y = gl_scr[rb, :] * pr_scr[rb, :]
        rec_ref[b] = (_rms(y) * grec_ref[...]).astype(BF16)


def _inproj(x, g1, sh1, n1, w_all, b_f, conv_w, conv_b, wa_bd, b_a, wx_bd, b_x, lam, g_rec, conv_past, h0):
    b, t, _ = x.shape
    bb, tm = BATCH_ROWS, SEQ_ROWS
    nb, nt = b // bb, t // tm
    rows = bb * tm
    wcols = w_all.shape[1]
    const2 = lambda shape: pl.BlockSpec(shape, lambda i, j: (0, 0))
    per_b = lambda last: pl.BlockSpec((bb, 1, last), lambda i, j: (i, 0, 0))
    seq = lambda last: pl.BlockSpec((bb, tm, last), lambda i, j: (i, j, 0))
    return pl.pallas_call(
        _inproj_kernel,
        grid=(nb, nt),
        in_specs=[seq(D_MODEL), per_b(D_MODEL), per_b(D_MODEL), const2((1, D_MODEL)),
                  const2((D_MODEL, wcols)), const2((1, LANES)),
                  const2((CONV_W, W_REC)), const2((1, W_REC)),
                  const2((W_REC, W_REC)), const2((1, W_REC)), const2((W_REC, W_REC)), const2((1, W_REC)),
                  const2((1, W_REC)), const2((1, W_REC)),
                  pl.BlockSpec((bb, SUBLANES, W_REC), lambda i, j: (i, 0, 0)),
                  pl.BlockSpec((bb, W_REC), lambda i, j: (i, 0))],
        out_specs=[seq(W_ATT), seq(W_ATT), seq(W_ATT),
                   pl.BlockSpec((1, 1, N_HEADS, rows), lambda i, j: (i, j, 0, 0)),
                   seq(W_REC),
                   pl.BlockSpec((bb, W_REC), lambda i, j: (i, 0)),
                   pl.BlockSpec((bb, SUBLANES, W_REC), lambda i, j: (i, 0, 0))],
        out_shape=[jax.ShapeDtypeStruct((b, t, W_ATT), BF16),
                   jax.ShapeDtypeStruct((b, t, W_ATT), F32),
                   jax.ShapeDtypeStruct((b, t, W_ATT), F32),
                   jax.ShapeDtypeStruct((nb, nt, N_HEADS, rows), F32),
                   jax.ShapeDtypeStruct((b, t, W_REC), BF16),
                   jax.ShapeDtypeStruct((b, W_REC), F32),
                   jax.ShapeDtypeStruct((b, SUBLANES, W_REC), F32)],
        scratch_shapes=[pltpu.VMEM((bb, tm + SUBLANES, W_REC), F32),
                        pltpu.VMEM((rows, D_MODEL), BF16),
                        pltpu.VMEM((rows, 2 * W_REC + LANES), F32),
                        pltpu.VMEM((rows, W_REC), F32),
                        pltpu.VMEM((rows, W_REC), F32),
                        pltpu.VMEM((rows, W_REC), F32),
                        pltpu.VMEM((W_REC // LANES, tm, bb, LANES), F32),
                        pltpu.VMEM((W_REC // LANES, tm, bb, LANES), F32),
                        pltpu.VMEM((W_REC // LANES, tm, bb, LANES), F32),
                        pltpu.VMEM((rows, W_REC), F32),
                        pltpu.VMEM((bb, W_REC), F32)],
        compiler_params=pltpu.CompilerParams(dimension_semantics=("arbitrary", "arbitrary"),
                                             vmem_limit_bytes=VMEM_LIMIT),
        name="inproj",
    )(x, g1, sh1, n1, w_all, b_f, conv_w, conv_b, wa_bd, b_a, wx_bd, b_x, lam, g_rec, conv_past, h0)


def _cumsum_kernel(x_ref, o_ref):
    length = x_ref.shape[2]
    c = CUMSUM_CHUNK
    upper = (lax.broadcasted_iota(jnp.int32, (c, c), 0) <= lax.broadcasted_iota(jnp.int32, (c, c), 1)).astype(BF16)
    carry = jnp.zeros((N_HEADS, 1), F32)
    for j in range(length // c):
        x = x_ref[0, :, j * c:(j + 1) * c]
        hi, mid, lo = _split3(x)
        parts = jnp.concatenate([hi.astype(F32), mid.astype(F32), lo.astype(F32), jnp.zeros_like(x)], axis=0)
        sums = _dot(parts.astype(BF16), upper)
        out = sums[0:8] + sums[8:16] + sums[16:24] + carry
        o_ref[0, :, j * c:(j + 1) * c] = out
        carry = out[:, c - 1:c]


def _cumsum_time(lf_t):
    b, h, length = lf_t.shape
    return pl.pallas_call(
        _cumsum_kernel,
        grid=(b,),
        in_specs=[pl.BlockSpec((1, h, length), lambda i: (i, 0, 0))],
        out_specs=pl.BlockSpec((1, h, length), lambda i: (i, 0, 0)),
        out_shape=jax.ShapeDtypeStruct((b, h, length), F32),
        name="cumsum",
    )(lf_t)


def _attn_tile(qh, kt, vt, bias, m_ref, acc_ref, mask):
    s = lax.dot_general(qh, kt, (((1,), (1,)), ((), ())), preferred_element_type=F32) + bias
    if mask is not None:
        s = jnp.where(mask, s, NEG_BIG)
    width = min(LANES, s.shape[1])
    slabs = [s[:, c * width:(c + 1) * width] for c in range(s.shape[1] // width)]
    part = slabs[0]
    for sl in slabs[1:]:
        part = jnp.maximum(part, sl)
    m = m_ref[...]
    m_new = jnp.maximum(m, jnp.broadcast_to(jnp.max(part, axis=1, keepdims=True), m.shape))
    p = jnp.concatenate([jnp.exp2(sl - m_new[:, :width]) for sl in slabs], axis=1)
    acc_ref[...] = jnp.exp2(m - m_new) * acc_ref[...] + _dot(p.astype(BF16), vt)
    m_ref[...] = m_new


def _attn_kernel(q_ref, k_ref, v_ref, f_ref, o_ref, kb, vb, m_scr, acc_scr, *, tq, ck):
    qi = pl.program_id(2)
    chunks_per_tile = tq // ck

    @pl.when(qi == 0)
    def _():
        v_f32 = v_ref[0]
        first = lax.broadcasted_iota(jnp.int32, v_f32.shape, 1) < HEAD_DIM
        kb[...] = k_ref[0].astype(BF16)
        vb[0] = jnp.where(first, v_f32, 1.0).astype(BF16)
        vb[1] = jnp.where(first, 1.0, v_f32).astype(BF16)

    q = q_ref[0]
    first_head = lax.broadcasted_iota(jnp.int32, q.shape, 1) < HEAD_DIM
    qh = (jnp.where(first_head, q, jnp.zeros_like(q)), jnp.where(first_head, jnp.zeros_like(q), q))

    ref_f = f_ref[0, 0, qi * chunks_per_tile][:, 0:1]
    m_scr[...] = jnp.full(m_scr.shape, NEG_BIG, F32)
    acc_scr[...] = jnp.zeros(acc_scr.shape, F32)

    def chunk(j, row0=0, masked=False):
        start = pl.multiple_of(j * ck, ck)
        kt = kb[pl.ds(start, ck), :]
        f_tile = f_ref[0, 0, j]
        mask = None
        if masked:
            shape = (tq - row0, ck)
            mask = lax.broadcasted_iota(jnp.int32, shape, 1) <= lax.broadcasted_iota(jnp.int32, shape, 0)
        for hd in range(2):
            bias = (ref_f[hd:hd + 1, :] - f_tile[hd:hd + 1, :]) * LOG2E
            _attn_tile(qh[hd][row0:, :], kt, vb[hd, pl.ds(start, ck), :], bias,
                       m_scr.at[hd, row0:tq], acc_scr.at[hd, row0:tq], mask)

    n_chunks = qi * chunks_per_tile

    def chunk_body(jj, carry):
        for u in range(ATTN_UNROLL):
            chunk(jj * ATTN_UNROLL + u)
        return carry

    lax.fori_loop(0, n_chunks // ATTN_UNROLL, chunk_body, 0)
    for r in range(ATTN_UNROLL - 1):
        @pl.when(r < n_chunks % ATTN_UNROLL)
        def _():
            chunk((n_chunks // ATTN_UNROLL) * ATTN_UNROLL + r)

    for c in range(chunks_per_tile):
        chunk(n_chunks + c, row0=c * ck, masked=True)

    outs = []
    for hd in range(2):
        acc = acc_scr[hd]
        outs.append(acc / pltpu.roll(acc, HEAD_DIM, axis=1))
    o_ref[0] = jnp.where(first_head, outs[0], outs[1]).astype(BF16)


def _attention(q, k, v, f_tiles):
    b, t, _ = q.shape
    tq = min(t, ATTN_TILE)
    ck = f_tiles.shape[-1]
    q_spec = pl.BlockSpec((1, tq, LANES), lambda i, p, j: (i, j, p))
    full = pl.BlockSpec((1, t, LANES), lambda i, p, j: (i, 0, p))
    return pl.pallas_call(
        functools.partial(_attn_kernel, tq=tq, ck=ck),
        grid=(b, N_HEADS // 2, t // tq),
        in_specs=[q_spec, full, full,
                  pl.BlockSpec((1, 1) + f_tiles.shape[2:], lambda i, p, j: (i, p, 0, 0, 0))],
        out_specs=q_spec,
        out_shape=jax.ShapeDtypeStruct((b, t, W_ATT), BF16),
        scratch_shapes=[pltpu.VMEM((t, LANES), BF16), pltpu.VMEM((2, t, LANES), BF16),
                        pltpu.VMEM((2, tq, LANES), F32), pltpu.VMEM((2, tq, LANES), F32)],
        compiler_params=pltpu.CompilerParams(dimension_semantics=("arbitrary", "arbitrary", "arbitrary"),
                                             vmem_limit_bytes=VMEM_LIMIT),
        name="attn",
    )(q, k, v, f_tiles)


def _attn_cached_kernel(q_ref, kn_ref, vn_ref, f_ref, kt_ref, vt_ref, o_ref, *, past):
    t = q_ref.shape[1]
    q = q_ref[0]
    kn = kn_ref[0].astype(BF16)
    vn = vn_ref[0].astype(BF16)
    f = f_ref[0]
    causal = lax.broadcasted_iota(jnp.int32, (t, t), 1) <= lax.broadcasted_iota(jnp.int32, (t, t), 0)
    last_dims = (((1,), (1,)), ((), ()))
    outs = []
    for h in range(N_HEADS):
        cols = slice(h * HEAD_DIM, (h + 1) * HEAD_DIM)
        qh = q[:, cols]
        ref_f = f[h:h + 1, past:past + 1]
        s_p = _dot(qh, kt_ref[0, h].astype(BF16)) + (ref_f - f[h:h + 1, 0:past]) * LOG2E
        s_n = lax.dot_general(qh, kn[:, cols], last_dims, preferred_element_type=F32)
        s_n = jnp.where(causal, s_n + (ref_f - f[h:h + 1, past:past + t]) * LOG2E, NEG_BIG)
        m = jnp.maximum(jnp.max(s_p, axis=1, keepdims=True), jnp.max(s_n, axis=1, keepdims=True))
        p_p = jnp.exp2(s_p - m)
        p_n = jnp.exp2(s_n - m)
        l = jnp.sum(p_p, axis=1, keepdims=True) + jnp.sum(p_n, axis=1, keepdims=True)
        o = lax.dot_general(p_p.astype(BF16), vt_ref[0, h].astype(BF16), last_dims, preferred_element_type=F32)
        o = o + _dot(p_n.astype(BF16), vn[:, cols])
        outs.append(o / l)
    o_ref[0] = jnp.concatenate(outs, axis=1).astype(BF16)


def _attention_cached(q, k_new, v_new, f_all, k_t, v_t):
    b, t, _ = q.shape
    past = k_t.shape[-1]
    rows = pl.BlockSpec((1, t, W_ATT), lambda i: (i, 0, 0))
    cache = pl.BlockSpec((1, N_HEADS, HEAD_DIM, past), lambda i: (i, 0, 0, 0))
    return pl.pallas_call(
        functools.partial(_attn_cached_kernel, past=past),
        grid=(b,),
        in_specs=[rows, rows, rows, pl.BlockSpec((1,) + f_all.shape[1:], lambda i: (i, 0, 0)), cache, cache],
        out_specs=rows,
        out_shape=jax.ShapeDtypeStruct((b, t, W_ATT), BF16),
        compiler_params=pltpu.CompilerParams(dimension_semantics=("arbitrary",), vmem_limit_bytes=VMEM_LIMIT),
        name="attn_cached",
    )(q, k_new, v_new, f_all, k_t, v_t)


def _outproj_kernel(attn_ref, rec_ref, x_ref, ga1_ref, g2_ref, sh2_ref, n2_ref, gatt_ref, woa_ref, wor_ref,
                    wrh_ref, wrm_ref, br_ref,
                    x1_ref, u2p_ref, route_ref):
    bb, tm, _ = x_ref.shape
    rows = bb * tm
    attn = attn_ref[...].astype(F32)
    an = (_rms(attn) * gatt_ref[...]).reshape(rows, W_ATT).astype(BF16)
    mix = _dot(an, woa_ref[...]) + _dot(rec_ref[...].reshape(rows, W_REC), wor_ref[...])
    x1 = x_ref[...] + ga1_ref[...] * mix.reshape(bb, tm, D_MODEL)
    x1_ref[...] = x1
    u2 = (_rms(x1) * (n2_ref[...] * g2_ref[...]) + sh2_ref[...]).reshape(rows, D_MODEL)
    for i, piece in enumerate(_pack_pieces(u2)):
        u2p_ref[i] = piece.reshape(bb, tm, PIECE_COLS)

    u_hi, u_mid, _ = _split3(u2)
    logits = _dot(u_hi, wrh_ref[...]) + _dot(u_mid, wrh_ref[...]) + _dot(u_hi, wrm_ref[...]) + br_ref[...]
    lt = logits.T
    row8 = lax.broadcasted_iota(jnp.int32, (SUBLANES, rows), 0).astype(F32)
    lg = lt[0:SUBLANES]
    m_g = jnp.max(lg, axis=0, keepdims=True)
    gidx = jnp.min(jnp.where(lg == m_g, row8, float(SUBLANES)), axis=0, keepdims=True)
    p_top = 1.0 / jnp.sum(jnp.exp(lg - m_g), axis=0, keepdims=True)
    leg = jnp.zeros((EXPERTS_PER_GROUP, rows), F32)
    for g in range(N_GROUPS):
        lo = EXPERT_COL0 + g * EXPERTS_PER_GROUP
        leg = jnp.where(gidx == float(g), lt[lo:lo + EXPERTS_PER_GROUP], leg)
    v1 = jnp.max(leg, axis=0, keepdims=True)
    i1 = jnp.min(jnp.where(leg == v1, row8, float(SUBLANES)), axis=0, keepdims=True)
    leg2 = jnp.where(row8 == i1, -jnp.inf, leg)
    v2 = jnp.max(leg2, axis=0, keepdims=True)
    i2 = jnp.min(jnp.where(leg2 == v2, row8, float(SUBLANES)), axis=0, keepdims=True)
    e21 = jnp.exp(v2 - v1)
    w1 = p_top / (1.0 + e21)
    w2 = w1 * e21
    base = gidx * float(EXPERTS_PER_GROUP)
    out = jnp.where(row8 == 3.0, w2, 0.0)
    for r_idx, val in ((2.0, w1), (1.0, base + i2), (0.0, base + i1)):
        out = jnp.where(row8 == r_idx, val, out)
    route_ref[0, 0] = out


def _outproj(attn, rec, x, ga1, g2, sh2, n2, g_att, wo_a, wo_r, wr_hi, wr_mid, b_r, bb, tm):
    b, t, _ = x.shape
    nb, nt = b // bb, t // tm
    rows = bb * tm
    const2 = lambda shape: pl.BlockSpec(shape, lambda i, j: (0, 0))
    per_b = pl.BlockSpec((bb, 1, D_MODEL), lambda i, j: (i, 0, 0))
    seq = lambda last: pl.BlockSpec((bb, tm, last), lambda i, j: (i, j, 0))
    return pl.pallas_call(
        _outproj_kernel,
        grid=(nb, nt),
        in_specs=[seq(W_ATT), seq(W_REC), seq(D_MODEL), per_b, per_b, per_b, const2((1, D_MODEL)),
                  const2((1, W_ATT)), const2((W_ATT, D_MODEL)), const2((W_REC, D_MODEL)),
                  const2((D_MODEL, ROUTE_COLS)), const2((D_MODEL, ROUTE_COLS)), const2((1, ROUTE_COLS))],
        out_specs=[seq(D_MODEL),
                   pl.BlockSpec((N_PIECES, bb, tm, PIECE_COLS), lambda i, j: (0, i, j, 0)),
                   pl.BlockSpec((1, 1, SUBLANES, rows), lambda i, j: (i, j, 0, 0))],
        out_shape=[jax.ShapeDtypeStruct((b, t, D_MODEL), F32),
                   jax.ShapeDtypeStruct((N_PIECES, b, t, PIECE_COLS), U32),
                   jax.ShapeDtypeStruct((nb, nt, SUBLANES, rows), F32)],
        compiler_params=pltpu.CompilerParams(dimension_semantics=("arbitrary", "arbitrary"),
                                             vmem_limit_bytes=VMEM_LIMIT),
        name="outproj",
    )(attn, rec, x, ga1, g2, sh2, n2, g_att, wo_a, wo_r, wr_hi, wr_mid, b_r)


_SC_AXES = ("core", "subcore")


def _sc_mesh():
    return plsc.VectorSubcoreMesh(core_axis_name=_SC_AXES[0], subcore_axis_name=_SC_AXES[1])


def _sc_scatter_rows(src_hbm, idx_hbm, n, dst_hbm):
    def body(x_vmem, i_vmem):
        pltpu.sync_copy(x_vmem, dst_hbm.at[i_vmem.at[0]])

    pltpu.emit_pipeline(
        body,
        grid=(n // GATHER_WINDOW,),
        in_specs=[pl.BlockSpec((GATHER_WINDOW, PIECE_COLS), lambda i: (i, 0)),
                  pl.BlockSpec((1, GATHER_WINDOW), lambda i: (0, i))],
        out_specs=[],
        core_axis_name=_SC_AXES,
        dimension_semantics=(pltpu.PARALLEL,),
    )(src_hbm, idx_hbm)


def _sc_gather_rows(tab_hbm, idx_hbm, n, dst_hbm):
    def body(i_vmem, o_vmem):
        pltpu.sync_copy(tab_hbm.at[i_vmem.at[0]], o_vmem)

    pltpu.emit_pipeline(
        body,
        grid=(n // GATHER_WINDOW,),
        in_specs=[pl.BlockSpec((1, GATHER_WINDOW), lambda i: (0, i))],
        out_specs=[pl.BlockSpec((GATHER_WINDOW, PIECE_COLS), lambda i: (i, 0))],
        core_axis_name=_SC_AXES,
        dimension_semantics=(pltpu.PARALLEL,),
    )(idx_hbm, dst_hbm)


def _dispatch(tables, dests, dest_pad, n_rows):
    n_pad = dest_pad.shape[1]
    sizes = [t.shape[1] for t in tables]
    n_groups = len(tables)

    @pl.kernel(out_type=jax.ShapeDtypeStruct((N_PIECES, n_rows, PIECE_COLS), U32), mesh=_sc_mesh(), scratch_types=[])
    def scatter(*refs):
        tabs = refs[:n_groups]
        idxs = refs[n_groups:3 * n_groups]
        pad_hbm, o_hbm = refs[3 * n_groups], refs[3 * n_groups + 1]
        for piece in range(N_PIECES):
            dst = o_hbm.at[piece]
            for g in range(n_groups):
                for k in range(2):
                    _sc_scatter_rows(tabs[g].at[piece], idxs[2 * g + k], sizes[g], dst)
            _sc_scatter_rows(tabs[0].at[piece], pad_hbm, n_pad, dst)

    flat_idx = [d for pair in dests for d in pair]
    return scatter(*tables, *flat_idx, dest_pad)


def _collect(yo, dests):
    sizes = [pair[0].shape[1] for pair in dests]
    n_groups = len(dests)
    out_type = [jax.ShapeDtypeStruct((2, N_PIECES, n, PIECE_COLS), U32) for n in sizes]

    @pl.kernel(out_type=out_type, mesh=_sc_mesh(), scratch_types=[])
    def gather(*refs):
        yo_hbm = refs[0]
        idxs = refs[1:1 + 2 * n_groups]
        outs = refs[1 + 2 * n_groups:]
        for piece in range(N_PIECES):
            for g in range(n_groups):
                for k in range(2):
                    _sc_gather_rows(yo_hbm.at[piece], idxs[2 * g + k], sizes[g], outs[g].at[k, piece])

    flat_idx = [d for pair in dests for d in pair]
    return gather(yo, *flat_idx)


def _expert_kernel(blk_e_ref, blk_first_ref, n_used_ref, xs_ref, wg_ref, wu_ref, wd_ref, yo_ref, wg_b, wu_b, wd_b):
    b = pl.program_id(0)

    @pl.when(blk_first_ref[b] == 1)
    def _():
        wg_b[...] = wg_ref[0].astype(BF16)
        wu_b[...] = wu_ref[0].astype(BF16)
        wd_b[...] = wd_ref[0].astype(BF16)

    @pl.when(b < n_used_ref[0])
    def _():
        chunks = [c.astype(BF16) for c in _unpack_pieces([xs_ref[i] for i in range(N_PIECES)])]
        g = None
        u = None
        for i, xc in enumerate(chunks):
            rows_i = slice(i * PIECE_COLS, (i + 1) * PIECE_COLS)
            gi = _dot(xc, wg_b[rows_i, :])
            ui = _dot(xc, wu_b[rows_i, :])
            g = gi if g is None else g + gi
            u = ui if u is None else u + ui
        h = (g * _sigmoid(g)) * u
        for i, piece in enumerate(_pack_pieces(_dot(h.astype(BF16), wd_b[...]))):
            yo_ref[i] = piece

    @pl.when(b >= n_used_ref[0])
    def _():
        yo_ref[...] = jnp.zeros(yo_ref.shape, U32)


def _experts(xs, blk_e, blk_first, n_used, w_gate, w_up, w_down):
    p = xs.shape[1]
    tb = EXPERT_BLOCK
    nblk = p // tb
    row_block = pl.BlockSpec((N_PIECES, tb, PIECE_COLS), lambda i, e, f, n: (0, i, 0))
    grid_spec = pltpu.PrefetchScalarGridSpec(
        num_scalar_prefetch=3,
        grid=(nblk,),
        in_specs=[row_block,
                  pl.BlockSpec((1, D_MODEL, D_EXPERT), lambda i, e, f, n: (e[i], 0, 0)),
                  pl.BlockSpec((1, D_MODEL, D_EXPERT), lambda i, e, f, n: (e[i], 0, 0)),
                  pl.BlockSpec((1, D_EXPERT, D_MODEL), lambda i, e, f, n: (e[i], 0, 0))],
        out_specs=row_block,
        scratch_shapes=[pltpu.VMEM((D_MODEL, D_EXPERT), BF16),
                        pltpu.VMEM((D_MODEL, D_EXPERT), BF16),
                        pltpu.VMEM((D_EXPERT, D_MODEL), BF16)],
    )
    return pl.pallas_call(
        _expert_kernel,
        grid_spec=grid_spec,
        out_shape=jax.ShapeDtypeStruct((N_PIECES, p, PIECE_COLS), U32),
        compiler_params=pltpu.CompilerParams(dimension_semantics=("arbitrary",), vmem_limit_bytes=VMEM_LIMIT),
        name="experts",
    )(blk_e, blk_first, n_used, xs, w_gate, w_up, w_down)


def _combine_kernel(x1_ref, yg_ref, route_ref, ga2_ref, fg_ref, o_ref):
    bb, tm, _ = x1_ref.shape
    rows = bb * tm
    route = jnp.concatenate([route_ref[...], jnp.zeros((LANES - SUBLANES, rows), F32)], axis=0)
    w = route.T
    y = None
    for k in range(2):
        chunks = _unpack_pieces([yg_ref[k, i] for i in range(N_PIECES)])
        yk = w[:, 2 + k:3 + k] * jnp.concatenate(chunks, axis=1)
        y = yk if y is None else y + yk
    out = x1_ref[...] + ga2_ref[...] * y.reshape(bb, tm, D_MODEL)
    o_ref[...] = _rms(out) * fg_ref[...]


def _combine(x1, yg, route, ga2, final_g, bb, tm):
    b, t, _ = x1.shape
    nb, nt = b // bb, t // tm
    rows = bb * tm
    return pl.pallas_call(
        _combine_kernel,
        grid=(nb, nt),
        in_specs=[pl.BlockSpec((bb, tm, D_MODEL), lambda i, j: (i, j, 0)),
                  pl.BlockSpec((2, N_PIECES, rows, PIECE_COLS), lambda i, j: (0, 0, i * nt + j, 0)),
                  pl.BlockSpec((SUBLANES, rows), lambda i, j: (0, i * nt + j)),
                  pl.BlockSpec((bb, 1, D_MODEL), lambda i, j: (i, 0, 0)),
                  pl.BlockSpec((1, D_MODEL), lambda i, j: (0, 0))],
        out_specs=pl.BlockSpec((bb, tm, D_MODEL), lambda i, j: (i, j, 0)),
        out_shape=jax.ShapeDtypeStruct((b, t, D_MODEL), F32),
        compiler_params=pltpu.CompilerParams(dimension_semantics=("arbitrary", "arbitrary"),
                                             vmem_limit_bytes=VMEM_LIMIT),
        name="combine",
    )(x1, yg, route, ga2, final_g)


def _block_diag(w):
    n, k, _ = w.shape
    eye = jnp.eye(n, dtype=w.dtype)
    return (eye[:, None, :, None] * w[:, :, None, :]).reshape(n * k, n * k)


def _tiles(f, tile):
    b, _, length = f.shape
    return f.reshape(b, N_HEADS // 2, 2, length // tile, tile).transpose(0, 1, 3, 2, 4)


def _untile_rows(a, b, t):
    nb, nt, r, rows = a.shape
    bb = b // nb
    tm = t // nt
    return a.reshape(nb, nt, r, bb, tm).transpose(2, 0, 3, 1, 4).reshape(r, b, t)


def _group_front(x, c_mod, layer, conv_past, h0, cache):
    b, t, _ = x.shape
    sh1, sc1, ga1, sh2, sc2, ga2 = [m.reshape(b, 1, D_MODEL) for m in jnp.split(c_mod, 6, axis=-1)]
    q, k, v, lf_steps, rec, h_t, ctail = _inproj(
        x, 1.0 + sc1, sh1, layer["n1"], layer["w_all"], layer["b_f"], layer["conv_w"], layer["conv_b"],
        layer["wa_bd"], layer["b_a"], layer["wx_bd"], layer["b_x"], layer["lam"], layer["g_rec"], conv_past, h0)
    lf_t = _untile_rows(lf_steps, b, t).transpose(1, 0, 2)
    if cache is None:
        attn = _attention(q, k, v, _tiles(_cumsum_time(lf_t), min(t, ATTN_CHUNK)))
    else:
        k_t, v_t, lf_past_t = cache
        total = k_t.shape[-1] + t
        padded = -(-total // CUMSUM_CHUNK) * CUMSUM_CHUNK
        lf_all = jnp.concatenate([lf_past_t, lf_t, jnp.zeros((b, N_HEADS, padded - total), F32)], axis=2)
        attn = _attention_cached(q, k, v, _cumsum_time(lf_all), k_t, v_t)
    bb, tm = (1, ROW_TILE) if t >= ROW_TILE else (ROW_TILE // t, t)
    x1, u2p, route = _outproj(attn, rec, x, ga1, 1.0 + sc2, sh2, layer["n2"], layer["g_att"],
                              layer["wo_a"], layer["wo_r"], layer["wr_hi"], layer["wr_mid"], layer["b_r"], bb, tm)
    route = _untile_rows(route, b, t).reshape(SUBLANES, b * t)
    leaves = (k.reshape(1, b, t, N_HEADS, HEAD_DIM), v.reshape(1, b, t, N_HEADS, HEAD_DIM),
              lf_t.transpose(0, 2, 1)[None], h_t[None], ctail[None, :, SUBLANES - (CONV_W - 1):, :])
    combine_tile = (1, COMBINE_TILE) if t >= COMBINE_TILE else (bb, tm)
    return x1, u2p.reshape(N_PIECES, b * t, PIECE_COLS), route, ga2, combine_tile, leaves


def kernel(x_prompt, x_sample, c_prompt, c_sample, cache_k, cache_v, cache_logf, state_h, state_conv, norm1_g, norm2_g, w_ada, b_ada, w_in, b_f, conv_w, conv_b, w_rg_a, b_rg_a, w_rg_x, b_rg_x, rg_lambda, out_g_att, out_g_rec, w_out, w_route_group, b_route_group, w_route_expert, b_route_expert, w_exp_gate, w_exp_up, w_exp_down, final_g):
    bp, tp, _ = x_prompt.shape
    bs, ts, _ = x_sample.shape
    l = 0
    o1, o2, o3 = W_ATT, 2 * W_ATT, 3 * W_ATT
    o4 = o3 + N_HEADS
    o5 = o4 + W_REC
    w_in_l = w_in[l]
    w_all = jnp.concatenate([w_in_l[:, :o3], w_in_l[:, o4:], w_in_l[:, o3:o4],
                             jnp.zeros((D_MODEL, LANES - N_HEADS), F32)], axis=1).astype(BF16)
    w_r = jnp.zeros((D_MODEL, ROUTE_COLS), F32)
    w_r = w_r.at[:, 0:N_GROUPS].set(w_route_group[l]).at[:, EXPERT_COL0:EXPERT_COL0 + N_EXPERTS].set(w_route_expert[l])
    wr_hi = w_r.astype(BF16)
    wr_mid = (w_r - wr_hi.astype(F32)).astype(BF16)
    b_r = jnp.zeros((1, ROUTE_COLS), F32)
    b_r = b_r.at[0, 0:N_GROUPS].set(b_route_group[l]).at[0, N_GROUPS:SUBLANES].set(NEG_BIG)
    b_r = b_r.at[0, EXPERT_COL0:EXPERT_COL0 + N_EXPERTS].set(b_route_expert[l])
    layer = {
        "n1": norm1_g[l].reshape(1, D_MODEL), "n2": norm2_g[l].reshape(1, D_MODEL),
        "w_all": w_all,
        "b_f": jnp.concatenate([b_f[l], jnp.zeros((LANES - N_HEADS,), F32)]).reshape(1, LANES),
        "conv_w": conv_w[l], "conv_b": conv_b[l].reshape(1, W_REC),
        "wa_bd": _block_diag(w_rg_a[l]).astype(BF16), "b_a": b_rg_a[l].reshape(1, W_REC),
        "wx_bd": _block_diag(w_rg_x[l]).astype(BF16), "b_x": b_rg_x[l].reshape(1, W_REC),
        "lam": rg_lambda[l].reshape(1, W_REC), "g_rec": out_g_rec[l].reshape(1, W_REC),
        "g_att": out_g_att[l].reshape(1, W_ATT),
        "wo_a": w_out[l, :W_ATT].astype(BF16), "wo_r": w_out[l, W_ATT:].astype(BF16),
        "wr_hi": wr_hi, "wr_mid": wr_mid, "b_r": b_r,
    }

    ada = _ada(jnp.concatenate([c_prompt, c_sample], axis=0), w_ada[l], b_ada[l])
    pad_rows = SUBLANES - (CONV_W - 1)
    x1p, u2p_p, route_p, ga2p, tile_p, leaves_p = _group_front(
        x_prompt, ada[:bp], layer, jnp.zeros((bp, SUBLANES, W_REC), F32), jnp.zeros((bp, W_REC), F32), None)
    conv_past_s = jnp.concatenate([jnp.zeros((bs, pad_rows, W_REC), F32), state_conv[l]], axis=1)
    x1s, u2p_s, route_s, ga2s, tile_s, leaves_s = _group_front(
        x_sample, ada[bp:], layer, conv_past_s, state_h[l],
        (cache_k[l].transpose(0, 2, 3, 1), cache_v[l].transpose(0, 2, 3, 1), cache_logf[l].transpose(0, 2, 1)))

    n_p, n_s = bp * tp, bs * ts
    n_tok = n_p + n_s
    route = jnp.concatenate([route_p, route_s], axis=1)
    eid = route[0:2].astype(jnp.int32).reshape(-1)
    n_slot = 2 * n_tok
    tb = EXPERT_BLOCK
    experts = jnp.arange(N_EXPERTS, dtype=jnp.int32)
    onehot = (eid[:, None] == experts[None, :]).astype(jnp.int32)
    csum = jnp.cumsum(onehot, axis=0)
    counts = csum[-1]
    rank = jnp.sum(csum * onehot, axis=1) - 1
    padded = ((counts + tb - 1) // tb) * tb
    pend = jnp.cumsum(padded)
    pstart = pend - padded
    dest = jnp.sum(onehot * pstart[None, :], axis=1) + rank
    nblk = -(-n_slot // tb) + N_EXPERTS
    n_rows = nblk * tb
    blk_row0 = jnp.arange(nblk, dtype=jnp.int32) * tb
    blk_e = jnp.minimum(jnp.sum((pend[None, :] <= blk_row0[:, None]).astype(jnp.int32), axis=1), N_EXPERTS - 1)
    blk_first = jnp.concatenate([jnp.ones((1,), jnp.int32), (blk_e[1:] != blk_e[:-1]).astype(jnp.int32)])
    n_used = (pend[-1:] // tb).astype(jnp.int32)
    pad_e = padded - counts
    cpad = jnp.cumsum(pad_e)
    j = jnp.arange(n_rows - n_slot, dtype=jnp.int32)
    owner = (cpad[None, :] <= j[:, None]).astype(jnp.int32)
    e_j = jnp.minimum(jnp.sum(owner, axis=1), N_EXPERTS - 1)
    sel = (e_j[:, None] == experts[None, :]).astype(jnp.int32)
    in_expert = jnp.sum(sel * (pstart + counts - (cpad - pad_e))[None, :], axis=1) + j
    dest_pad = jnp.where(j < cpad[-1], in_expert, pend[-1] + (j - cpad[-1])).reshape(1, -1)

    dest = dest.reshape(2, n_tok)
    dests = [[dest[k, :n_p].reshape(1, n_p) for k in range(2)], [dest[k, n_p:].reshape(1, n_s) for k in range(2)]]
    xs = _dispatch([u2p_p, u2p_s], dests, dest_pad, n_rows)
    yo = _experts(xs, blk_e, blk_first, n_used, w_exp_gate[l], w_exp_up[l], w_exp_down[l])
    yg_p, yg_s = _collect(yo, dests)

    fg = final_g.reshape(1, D_MODEL)
    y_prompt = _combine(x1p, yg_p, route_p, ga2p, fg, *tile_p)
    y_sample = _combine(x1s, yg_s, route_s, ga2s, fg, *tile_s)
    return (y_prompt, y_sample) + leaves_p + leaves_s
```

```python
import functools
import math

import jax
import jax.numpy as jnp
from jax import lax
from jax.experimental import pallas as pl
from jax.experimental.pallas import tpu as pltpu
from jax.experimental.pallas import tpu_sc as plsc

F32 = jnp.float32
BF16 = jnp.bfloat16
U32 = jnp.uint32

D_MODEL = 1024
N_HEADS = 8
HEAD_DIM = 64
W_ATT = N_HEADS * HEAD_DIM
W_REC = D_MODEL - W_ATT
N_REC_BLOCKS = 8
REC_BLOCK = W_REC // N_REC_BLOCKS
CONV_W = 4
RG_C = 8.0
N_GROUPS = 4
EXPERTS_PER_GROUP = 8
N_EXPERTS = N_GROUPS * EXPERTS_PER_GROUP
D_EXPERT = 256
EPS = 1e-6

LANES = 128
SUBLANES = 8
LOG2E = math.log2(math.e)
Q_SCALE = LOG2E * HEAD_DIM ** -0.5
NEG_BIG = -1e30
VMEM_LIMIT = 48 * 1024 * 1024

SEQ_ROWS = 64
BATCH_ROWS = 8
CUMSUM_CHUNK = 512
ATTN_TILE = 2048
ATTN_CHUNK = 512
ATTN_UNROLL = 2
ROW_TILE = 512
EXPERT_BLOCK = 512
EXPERT_BLOCK_SMALL = 128
COMBINE_TILE = 1024
GATHER_WINDOW = 128
PIECE_COLS = 256
N_PIECES = D_MODEL // (2 * PIECE_COLS)
ROUTE_COLS = 128
EXPERT_COL0 = 8


def _split3(x):
    hi = x.astype(BF16)
    r1 = x - hi.astype(F32)
    mid = r1.astype(BF16)
    lo = (r1 - mid.astype(F32)).astype(BF16)
    return hi, mid, lo


def _dot(a, b):
    return jnp.dot(a, b, preferred_element_type=F32)


def _rms(x):
    return x * lax.rsqrt(jnp.mean(x * x, axis=-1, keepdims=True) + EPS)


def _sigmoid(x):
    return 0.5 * jnp.tanh(0.5 * x) + 0.5


def _pack_bf16_pairs(y):
    n = y.shape[-1] // 2
    yb = y.astype(BF16).astype(F32)
    lo = pltpu.bitcast(yb[:, :n], U32)
    hi = pltpu.bitcast(yb[:, n:], U32)
    return (lo >> 16) | hi


def _unpack_bf16_pairs(p):
    lo = pltpu.bitcast(p << 16, F32)
    hi = pltpu.bitcast(p & jnp.uint32(0xFFFF0000), F32)
    return lo, hi


def _pack_pieces(y):
    w = 2 * PIECE_COLS
    return [_pack_bf16_pairs(y[:, i * w:(i + 1) * w]) for i in range(N_PIECES)]


def _unpack_pieces(pieces):
    chunks = []
    for p in pieces:
        chunks.extend(_unpack_bf16_pairs(p))
    return chunks


def _ada_kernel(c_ref, w_ref, b_ref, o_ref):
    c = c_ref[...]
    a = c * jax.nn.sigmoid(c)
    a_hi, a_mid, _ = _split3(a)
    w_hi, w_mid, _ = _split3(w_ref[...])
    o_ref[...] = _dot(a_hi, w_hi) + _dot(a_mid, w_hi) + _dot(a_hi, w_mid) + b_ref[...]


def _ada(c, w_ada, b_ada):
    rows = c.shape[0]
    n = w_ada.shape[1]
    tn = 1536
    return pl.pallas_call(
        _ada_kernel,
        grid=(n // tn,),
        in_specs=[pl.BlockSpec((rows, D_MODEL), lambda j: (0, 0)),
                  pl.BlockSpec((D_MODEL, tn), lambda j: (0, j)),
                  pl.BlockSpec((1, tn), lambda j: (0, j))],
        out_specs=pl.BlockSpec((rows, tn), lambda j: (0, j)),
        out_shape=jax.ShapeDtypeStruct((rows, n), F32),
        compiler_params=pltpu.CompilerParams(vmem_limit_bytes=VMEM_LIMIT),
        name="ada",
    )(c, w_ada, b_ada.reshape(1, n))


def _inproj_kernel(x_ref, g1_ref, sh1_ref, n1_ref, w_ref, bf_ref, cw_ref, cb_ref, wa_ref, ba_ref, wx_ref, bx_ref,
                   lam_ref, grec_ref, cpast_ref, h0_ref,
                   q_ref, k_ref, v_ref, lf_ref, rec_ref, ht_ref, ctail_ref,
                   xbuf, ub_scr, z_scr, xc_scr, pr_scr, pi_scr, a_scr, u_scr, hs_scr, gl_scr, h_scr):
    bb, tm, _ = x_ref.shape
    rows = bb * tm
    n_lane_tiles = W_REC // LANES
    half = W_REC // 2
    o_x = 3 * W_ATT

    @pl.when(pl.program_id(1) == 0)
    def _():
        xbuf[:, 0:SUBLANES, :] = cpast_ref[...]
        h_scr[...] = h0_ref[...]

    mod = n1_ref[...] * g1_ref[...]
    for b in range(bb):
        rb = slice(b * tm, (b + 1) * tm)
        ub_scr[rb, :] = (_rms(x_ref[b]) * mod[b] + sh1_ref[b]).astype(BF16)

    z_scr[...] = _dot(ub_scr[...], w_ref[:, o_x:])

    cw = cw_ref[...]
    for b in range(bb):
        rb = slice(b * tm, (b + 1) * tm)
        xb = z_scr[rb, 0:W_REC]
        xbuf[b, SUBLANES:SUBLANES + tm, :] = xb
        xc = cb_ref[...] + cw[3:4, :] * xb
        for j in range(CONV_W - 1):
            off = SUBLANES - (CONV_W - 1) + j
            xc = xc + cw[j:j + 1, :] * xbuf[b, off:off + tm, :]
        tail = xbuf[b, tm:tm + SUBLANES, :]
        ctail_ref[b] = tail
        xbuf[b, 0:SUBLANES, :] = tail
        xc_scr[rb, :] = xc

    xcb = xc_scr[...].astype(BF16)
    for w_gate, pre in ((wa_ref, pr_scr), (wx_ref, pi_scr)):
        pre[:, :half] = _dot(xcb[:, :half], w_gate[:half, :half])
        pre[:, half:] = _dot(xcb[:, half:], w_gate[half:, half:])

    zq = _dot(ub_scr[...], w_ref[:, :o_x])
    q_ref[...] = (zq[:, 0:W_ATT] * Q_SCALE).reshape(bb, tm, W_ATT).astype(BF16)
    k_ref[...] = zq[:, W_ATT:2 * W_ATT].reshape(bb, tm, W_ATT)
    v_ref[...] = zq[:, 2 * W_ATT:3 * W_ATT].reshape(bb, tm, W_ATT)

    zf = z_scr[:, 2 * W_REC:2 * W_REC + LANES] + bf_ref[...]
    lf = jnp.minimum(zf, 0.0) - jnp.log1p(jnp.exp(-jnp.abs(zf)))
    lf_ref[0, 0] = lf.T[0:N_HEADS, :]

    nlam = -lam_ref[...]
    decay = -RG_C * (jnp.maximum(nlam, 0.0) + jnp.log1p(jnp.exp(-jnp.abs(nlam))))
    for b in range(bb):
        rb = slice(b * tm, (b + 1) * tm)
        r = _sigmoid(pr_scr[rb, :] + ba_ref[...])
        i = _sigmoid(pi_scr[rb, :] + bx_ref[...])
        log_a = decay * r
        t = jnp.tanh(log_a)
        neg_expm1 = -2.0 * t / (1.0 - t)
        root = jnp.where(neg_expm1 > 0.0, neg_expm1 * lax.rsqrt(neg_expm1), 0.0)
        pr_scr[rb, :] = jnp.exp(log_a)
        pi_scr[rb, :] = root * (i * xc_scr[rb, :])
        gb = z_scr[rb, W_REC:2 * W_REC]
        gl_scr[rb, :] = 0.5 * gb * (1.0 + jnp.tanh(math.sqrt(2.0 / math.pi) * (gb + 0.044715 * (gb * gb * gb))))

    def lane_tile(ref, c):
        return ref[:, c * LANES:(c + 1) * LANES].reshape(bb, tm, LANES)

    for c in range(n_lane_tiles):
        a_scr[c] = jnp.swapaxes(lane_tile(pr_scr, c), 0, 1)
        u_scr[c] = jnp.swapaxes(lane_tile(pi_scr, c), 0, 1)
    for c in range(n_lane_tiles):
        h = h_scr[:, c * LANES:(c + 1) * LANES]
        for step in range(tm):
            h = a_scr[c, step] * h + u_scr[c, step]
            hs_scr[c, step] = h
        h_scr[:, c * LANES:(c + 1) * LANES] = h
    ht_ref[...] = h_scr[...]
    for c in range(n_lane_tiles):
        pr_scr[:, c * LANES:(c + 1) * LANES] = jnp.swapaxes(hs_scr[c], 0, 1).reshape(rows, LANES)

    for b in range(bb):
        rb = slice(b * tm, (b + 1) * tm)
        y = gl_scr[rb, :] * pr_scr[rb, :]
        rec_ref[b] = (_rms(y) * grec_ref[...]).astype(BF16)


def _inproj(x, g1, sh1, n1, w_all, b_f, conv_w, conv_b, wa_bd, b_a, wx_bd, b_x, lam, g_rec, conv_past, h0):
    b, t, _ = x.shape
    bb, tm = BATCH_ROWS, SEQ_ROWS
    nb, nt = b // bb, t // tm
    rows = bb * tm
    wcols = w_all.shape[1]
    const2 = lambda shape: pl.BlockSpec(shape, lambda i, j: (0, 0))
    per_b = lambda last: pl.BlockSpec((bb, 1, last), lambda i, j: (i, 0, 0))
    seq = lambda last: pl.BlockSpec((bb, tm, last), lambda i, j: (i, j, 0))
    return pl.pallas_call(
        _inproj_kernel,
        grid=(nb, nt),
        in_specs=[seq(D_MODEL), per_b(D_MODEL), per_b(D_MODEL), const2((1, D_MODEL)),
                  const2((D_MODEL, wcols)), const2((1, LANES)),
                  const2((CONV_W, W_REC)), const2((1, W_REC)),
                  const2((W_REC, W_REC)), const2((1, W_REC)), const2((W_REC, W_REC)), const2((1, W_REC)),
                  const2((1, W_REC)), const2((1, W_REC)),
                  pl.BlockSpec((bb, SUBLANES, W_REC), lambda i, j: (i, 0, 0)),
                  pl.BlockSpec((bb, W_REC), lambda i, j: (i, 0))],
        out_specs=[seq(W_ATT), seq(W_ATT), seq(W_ATT),
                   pl.BlockSpec((1, 1, N_HEADS, rows), lambda i, j: (i, j, 0, 0)),
                   seq(W_REC),
                   pl.BlockSpec((bb, W_REC), lambda i, j: (i, 0)),
                   pl.BlockSpec((bb, SUBLANES, W_REC), lambda i, j: (i, 0, 0))],
        out_shape=[jax.ShapeDtypeStruct((b, t, W_ATT), BF16),
                   jax.ShapeDtypeStruct((b, t, W_ATT), F32),
                   jax.ShapeDtypeStruct((b, t, W_ATT), F32),
                   jax.ShapeDtypeStruct((nb, nt, N_HEADS, rows), F32),
                   jax.ShapeDtypeStruct((b, t, W_REC), BF16),
                   jax.ShapeDtypeStruct((b, W_REC), F32),
                   jax.ShapeDtypeStruct((b, SUBLANES, W_REC), F32)],
        scratch_shapes=[pltpu.VMEM((bb, tm + SUBLANES, W_REC), F32),
                        pltpu.VMEM((rows, D_MODEL), BF16),
                        pltpu.VMEM((rows, 2 * W_REC + LANES), F32),
                        pltpu.VMEM((rows, W_REC), F32),
                        pltpu.VMEM((rows, W_REC), F32),
                        pltpu.VMEM((rows, W_REC), F32),
                        pltpu.VMEM((W_REC // LANES, tm, bb, LANES), F32),
                        pltpu.VMEM((W_REC // LANES, tm, bb, LANES), F32),
                        pltpu.VMEM((W_REC // LANES, tm, bb, LANES), F32),
                        pltpu.VMEM((rows, W_REC), F32),
                        pltpu.VMEM((bb, W_REC), F32)],
        compiler_params=pltpu.CompilerParams(dimension_semantics=("arbitrary", "arbitrary"),
                                             vmem_limit_bytes=VMEM_LIMIT),
        name="inproj",
    )(x, g1, sh1, n1, w_all, b_f, conv_w, conv_b, wa_bd, b_a, wx_bd, b_x, lam, g_rec, conv_past, h0)


def _cumsum_kernel(x_ref, o_ref):
    length = x_ref.shape[2]
    c = CUMSUM_CHUNK
    upper = (lax.broadcasted_iota(jnp.int32, (c, c), 0) <= lax.broadcasted_iota(jnp.int32, (c, c), 1)).astype(BF16)
    carry = jnp.zeros((N_HEADS, 1), F32)
    for j in range(length // c):
        x = x_ref[0, :, j * c:(j + 1) * c]
        hi, mid, lo = _split3(x)
        parts = jnp.concatenate([hi.astype(F32), mid.astype(F32), lo.astype(F32), jnp.zeros_like(x)], axis=0)
        sums = _dot(parts.astype(BF16), upper)
        out = sums[0:8] + sums[8:16] + sums[16:24] + carry
        o_ref[0, :, j * c:(j + 1) * c] = out
        carry = out[:, c - 1:c]


def _cumsum_time(lf_t):
    b, h, length = lf_t.shape
    return pl.pallas_call(
        _cumsum_kernel,
        grid=(b,),
        in_specs=[pl.BlockSpec((1, h, length), lambda i: (i, 0, 0))],
        out_specs=pl.BlockSpec((1, h, length), lambda i: (i, 0, 0)),
        out_shape=jax.ShapeDtypeStruct((b, h, length), F32),
        name="cumsum",
    )(lf_t)


def _attn_tile(qh, kt, vt, bias, m_ref, acc_ref, mask):
    s = lax.dot_general(qh, kt, (((1,), (1,)), ((), ())), preferred_element_type=F32) + bias
    if mask is not None:
        s = jnp.where(mask, s, NEG_BIG)
    width = min(LANES, s.shape[1])
    slabs = [s[:, c * width:(c + 1) * width] for c in range(s.shape[1] // width)]
    part = slabs[0]
    for sl in slabs[1:]:
        part = jnp.maximum(part, sl)
    m = m_ref[...]
    m_new = jnp.maximum(m, jnp.broadcast_to(jnp.max(part, axis=1, keepdims=True), m.shape))
    p = jnp.concatenate([jnp.exp2(sl - m_new[:, :width]) for sl in slabs], axis=1)
    acc_ref[...] = jnp.exp2(m - m_new) * acc_ref[...] + _dot(p.astype(BF16), vt)
    m_ref[...] = m_new


def _attn_kernel(q_ref, k_ref, v_ref, f_ref, o_ref, kb, vb, m_scr, acc_scr, *, tq, ck):
    qi = pl.program_id(2)
    chunks_per_tile = tq // ck

    @pl.when(qi == 0)
    def _():
        v_f32 = v_ref[0]
        first = lax.broadcasted_iota(jnp.int32, v_f32.shape, 1) < HEAD_DIM
        kb[...] = k_ref[0].astype(BF16)
        vb[0] = jnp.where(first, v_f32, 1.0).astype(BF16)
        vb[1] = jnp.where(first, 1.0, v_f32).astype(BF16)

    q = q_ref[0]
    first_head = lax.broadcasted_iota(jnp.int32, q.shape, 1) < HEAD_DIM
    qh = (jnp.where(first_head, q, jnp.zeros_like(q)), jnp.where(first_head, jnp.zeros_like(q), q))

    ref_f = f_ref[0, 0, qi * chunks_per_tile][:, 0:1]
    m_scr[...] = jnp.full(m_scr.shape, NEG_BIG, F32)
    acc_scr[...] = jnp.zeros(acc_scr.shape, F32)

    def chunk(j, row0=0, masked=False):
        start = pl.multiple_of(j * ck, ck)
        kt = kb[pl.ds(start, ck), :]
        f_tile = f_ref[0, 0, j]
        mask = None
        if masked:
            shape = (tq - row0, ck)
            mask = lax.broadcasted_iota(jnp.int32, shape, 1) <= lax.broadcasted_iota(jnp.int32, shape, 0)
        for hd in range(2):
            bias = (ref_f[hd:hd + 1, :] - f_tile[hd:hd + 1, :]) * LOG2E
            _attn_tile(qh[hd][row0:, :], kt, vb[hd, pl.ds(start, ck), :], bias,
                       m_scr.at[hd, row0:tq], acc_scr.at[hd, row0:tq], mask)

    n_chunks = qi * chunks_per_tile

    def chunk_body(jj, carry):
        for u in range(ATTN_UNROLL):
            chunk(jj * ATTN_UNROLL + u)
        return carry

    lax.fori_loop(0, n_chunks // ATTN_UNROLL, chunk_body, 0)
    for r in range(ATTN_UNROLL - 1):
        @pl.when(r < n_chunks % ATTN_UNROLL)
        def _():
            chunk((n_chunks // ATTN_UNROLL) * ATTN_UNROLL + r)

    for c in range(chunks_per_tile):
        chunk(n_chunks + c, row0=c * ck, masked=True)

    outs = []
    for hd in range(2):
        acc = acc_scr[hd]
        outs.append(acc / pltpu.roll(acc, HEAD_DIM, axis=1))
    o_ref[0] = jnp.where(first_head, outs[0], outs[1]).astype(BF16)


def _attention(q, k, v, f_tiles):
    b, t, _ = q.shape
    tq = min(t, ATTN_TILE)
    ck = f_tiles.shape[-1]
    q_spec = pl.BlockSpec((1, tq, LANES), lambda i, p, j: (i, j, p))
    full = pl.BlockSpec((1, t, LANES), lambda i, p, j: (i, 0, p))
    return pl.pallas_call(
        functools.partial(_attn_kernel, tq=tq, ck=ck),
        grid=(b, N_HEADS // 2, t // tq),
        in_specs=[q_spec, full, full,
                  pl.BlockSpec((1, 1) + f_tiles.shape[2:], lambda i, p, j: (i, p, 0, 0, 0))],
        out_specs=q_spec,
        out_shape=jax.ShapeDtypeStruct((b, t, W_ATT), BF16),
        scratch_shapes=[pltpu.VMEM((t, LANES), BF16), pltpu.VMEM((2, t, LANES), BF16),
                        pltpu.VMEM((2, tq, LANES), F32), pltpu.VMEM((2, tq, LANES), F32)],
        compiler_params=pltpu.CompilerParams(dimension_semantics=("arbitrary", "arbitrary", "arbitrary"),
                                             vmem_limit_bytes=VMEM_LIMIT),
        name="attn",
    )(q, k, v, f_tiles)


def _attn_cached_kernel(q_ref, kn_ref, vn_ref, f_ref, kt_ref, vt_ref, o_ref, *, past):
    t = q_ref.shape[1]
    q = q_ref[0]
    kn = kn_ref[0].astype(BF16)
    vn = vn_ref[0].astype(BF16)
    f = f_ref[0]
    causal = lax.broadcasted_iota(jnp.int32, (t, t), 1) <= lax.broadcasted_iota(jnp.int32, (t, t), 0)
    last_dims = (((1,), (1,)), ((), ()))
    outs = []
    for h in range(N_HEADS):
        cols = slice(h * HEAD_DIM, (h + 1) * HEAD_DIM)
        qh = q[:, cols]
        ref_f = f[h:h + 1, past:past + 1]
        s_p = _dot(qh, kt_ref[0, h].astype(BF16)) + (ref_f - f[h:h + 1, 0:past]) * LOG2E
        s_n = lax.dot_general(qh, kn[:, cols], last_dims, preferred_element_type=F32)
        s_n = jnp.where(causal, s_n + (ref_f - f[h:h + 1, past:past + t]) * LOG2E, NEG_BIG)
        m = jnp.maximum(jnp.max(s_p, axis=1, keepdims=True), jnp.max(s_n, axis=1, keepdims=True))
        p_p = jnp.exp2(s_p - m)
        p_n = jnp.exp2(s_n - m)
        l = jnp.sum(p_p, axis=1, keepdims=True) + jnp.sum(p_n, axis=1, keepdims=True)
        o = lax.dot_general(p_p.astype(BF16), vt_ref[0, h].astype(BF16), last_dims, preferred_element_type=F32)
        o = o + _dot(p_n.astype(BF16), vn[:, cols])
        outs.append(o / l)
    o_ref[0] = jnp.concatenate(outs, axis=1).astype(BF16)


def _attention_cached(q, k_new, v_new, f_all, k_t, v_t):
    b, t, _ = q.shape
    past = k_t.shape[-1]
    rows = pl.BlockSpec((1, t, W_ATT), lambda i: (i, 0, 0))
    cache = pl.BlockSpec((1, N_HEADS, HEAD_DIM, past), lambda i: (i, 0, 0, 0))
    return pl.pallas_call(
        functools.partial(_attn_cached_kernel, past=past),
        grid=(b,),
        in_specs=[rows, rows, rows, pl.BlockSpec((1,) + f_all.shape[1:], lambda i: (i, 0, 0)), cache, cache],
        out_specs=rows,
        out_shape=jax.ShapeDtypeStruct((b, t, W_ATT), BF16),
        compiler_params=pltpu.CompilerParams(dimension_semantics=("arbitrary",), vmem_limit_bytes=VMEM_LIMIT),
        name="attn_cached",
    )(q, k_new, v_new, f_all, k_t, v_t)


def _outproj_kernel(attn_ref, rec_ref, x_ref, ga1_ref, g2_ref, sh2_ref, n2_ref, gatt_ref, woa_ref, wor_ref,
                    wrh_ref, wrm_ref, br_ref,
                    x1_ref, u2p_ref, route_ref):
    bb, tm, _ = x_ref.shape
    rows = bb * tm
    attn = attn_ref[...].astype(F32)
    an = (_rms(attn) * gatt_ref[...]).reshape(rows, W_ATT).astype(BF16)
    mix = _dot(an, woa_ref[...]) + _dot(rec_ref[...].reshape(rows, W_REC), wor_ref[...])
    x1 = x_ref[...] + ga1_ref[...] * mix.reshape(bb, tm, D_MODEL)
    x1_ref[...] = x1
    u2 = (_rms(x1) * (n2_ref[...] * g2_ref[...]) + sh2_ref[...]).reshape(rows, D_MODEL)
    for i, piece in enumerate(_pack_pieces(u2)):
        u2p_ref[i] = piece.reshape(bb, tm, PIECE_COLS)

    u_hi, u_mid, _ = _split3(u2)
    logits = _dot(u_hi, wrh_ref[...]) + _dot(u_mid, wrh_ref[...]) + _dot(u_hi, wrm_ref[...]) + br_ref[...]
    lt = logits.T
    row8 = lax.broadcasted_iota(jnp.int32, (SUBLANES, rows), 0).astype(F32)
    lg = lt[0:SUBLANES]
    m_g = jnp.max(lg, axis=0, keepdims=True)
    gidx = jnp.min(jnp.where(lg == m_g, row8, float(SUBLANES)), axis=0, keepdims=True)
    p_top = 1.0 / jnp.sum(jnp.exp(lg - m_g), axis=0, keepdims=True)
    leg = jnp.zeros((EXPERTS_PER_GROUP, rows), F32)
    for g in range(N_GROUPS):
        lo = EXPERT_COL0 + g * EXPERTS_PER_GROUP
        leg = jnp.where(gidx == float(g), lt[lo:lo + EXPERTS_PER_GROUP], leg)
    v1 = jnp.max(leg, axis=0, keepdims=True)
    i1 = jnp.min(jnp.where(leg == v1, row8, float(SUBLANES)), axis=0, keepdims=True)
    leg2 = jnp.where(row8 == i1, -jnp.inf, leg)
    v2 = jnp.max(leg2, axis=0, keepdims=True)
    i2 = jnp.min(jnp.where(leg2 == v2, row8, float(SUBLANES)), axis=0, keepdims=True)
    e21 = jnp.exp(v2 - v1)
    w1 = p_top / (1.0 + e21)
    w2 = w1 * e21
    base = gidx * float(EXPERTS_PER_GROUP)
    out = jnp.where(row8 == 3.0, w2, 0.0)
    for r_idx, val in ((2.0, w1), (1.0, base + i2), (0.0, base + i1)):
        out = jnp.where(row8 == r_idx, val, out)
    route_ref[0, 0] = out


def _outproj(attn, rec, x, ga1, g2, sh2, n2, g_att, wo_a, wo_r, wr_hi, wr_mid, b_r, bb, tm):
    b, t, _ = x.shape
    nb, nt = b // bb, t // tm
    rows = bb * tm
    const2 = lambda shape: pl.BlockSpec(shape, lambda i, j: (0, 0))
    per_b = pl.BlockSpec((bb, 1, D_MODEL), lambda i, j: (i, 0, 0))
    seq = lambda last: pl.BlockSpec((bb, tm, last), lambda i, j: (i, j, 0))
    return pl.pallas_call(
        _outproj_kernel,
        grid=(nb, nt),
        in_specs=[seq(W_ATT), seq(W_REC), seq(D_MODEL), per_b, per_b, per_b, const2((1, D_MODEL)),
                  const2((1, W_ATT)), const2((W_ATT, D_MODEL)), const2((W_REC, D_MODEL)),
                  const2((D_MODEL, ROUTE_COLS)), const2((D_MODEL, ROUTE_COLS)), const2((1, ROUTE_COLS))],
        out_specs=[seq(D_MODEL),
                   pl.BlockSpec((N_PIECES, bb, tm, PIECE_COLS), lambda i, j: (0, i, j, 0)),
                   pl.BlockSpec((1, 1, SUBLANES, rows), lambda i, j: (i, j, 0, 0))],
        out_shape=[jax.ShapeDtypeStruct((b, t, D_MODEL), F32),
                   jax.ShapeDtypeStruct((N_PIECES, b, t, PIECE_COLS), U32),
                   jax.ShapeDtypeStruct((nb, nt, SUBLANES, rows), F32)],
        compiler_params=pltpu.CompilerParams(dimension_semantics=("arbitrary", "arbitrary"),
                                             vmem_limit_bytes=VMEM_LIMIT),
        name="outproj",
    )(attn, rec, x, ga1, g2, sh2, n2, g_att, wo_a, wo_r, wr_hi, wr_mid, b_r)


_SC_AXES = ("core", "subcore")


def _sc_mesh():
    return plsc.VectorSubcoreMesh(core_axis_name=_SC_AXES[0], subcore_axis_name=_SC_AXES[1])


def _sc_scatter_rows(src_hbm, idx_hbm, n, src_rows, dst_hbm):
    src_blocks = src_rows // GATHER_WINDOW

    def body(x_vmem, i_vmem):
        pltpu.sync_copy(x_vmem, dst_hbm.at[i_vmem.at[0]])

    pltpu.emit_pipeline(
        body,
        grid=(n // GATHER_WINDOW,),
        in_specs=[pl.BlockSpec((GATHER_WINDOW, PIECE_COLS), lambda i: (i % src_blocks, 0)),
                  pl.BlockSpec((1, GATHER_WINDOW), lambda i: (0, i))],
        out_specs=[],
        core_axis_name=_SC_AXES,
        dimension_semantics=(pltpu.PARALLEL,),
    )(src_hbm, idx_hbm)


def _sc_gather_rows(tab_hbm, idx_hbm, n, dst_hbm):
    def body(i_vmem, o_vmem):
        pltpu.sync_copy(tab_hbm.at[i_vmem.at[0]], o_vmem)

    pltpu.emit_pipeline(
        body,
        grid=(n // GATHER_WINDOW,),
        in_specs=[pl.BlockSpec((1, GATHER_WINDOW), lambda i: (0, i))],
        out_specs=[pl.BlockSpec((GATHER_WINDOW, PIECE_COLS), lambda i: (i, 0))],
        core_axis_name=_SC_AXES,
        dimension_semantics=(pltpu.PARALLEL,),
    )(idx_hbm, dst_hbm)


def _dispatch(table, dests, dest_pad, n_rows):
    n = table.shape[1]
    n_pad = dest_pad.shape[1]

    @pl.kernel(out_type=jax.ShapeDtypeStruct((N_PIECES, n_rows, PIECE_COLS), U32), mesh=_sc_mesh(), scratch_types=[])
    def scatter(tab_hbm, d0_hbm, d1_hbm, pad_hbm, o_hbm):
        for piece in range(N_PIECES):
            dst = o_hbm.at[piece]
            for idx_hbm in (d0_hbm, d1_hbm):
                _sc_scatter_rows(tab_hbm.at[piece], idx_hbm, n, n, dst)
            _sc_scatter_rows(tab_hbm.at[piece], pad_hbm, n_pad, n, dst)

    return scatter(table, dests[0], dests[1], dest_pad)


def _collect(yo, dests):
    n = dests[0].shape[1]

    @pl.kernel(out_type=jax.ShapeDtypeStruct((2, N_PIECES, n, PIECE_COLS), U32), mesh=_sc_mesh(), scratch_types=[])
    def gather(yo_hbm, d0_hbm, d1_hbm, o_hbm):
        for piece in range(N_PIECES):
            for k, idx_hbm in enumerate((d0_hbm, d1_hbm)):
                _sc_gather_rows(yo_hbm.at[piece], idx_hbm, n, o_hbm.at[k, piece])

    return gather(yo, dests[0], dests[1])


def _expert_kernel(blk_e_ref, blk_first_ref, n_used_ref, xs_ref, wg_ref, wu_ref, wd_ref, yo_ref, wg_b, wu_b, wd_b):
    b = pl.program_id(0)

    @pl.when(blk_first_ref[b] == 1)
    def _():
        wg_b[...] = wg_ref[0].astype(BF16)
        wu_b[...] = wu_ref[0].astype(BF16)
        wd_b[...] = wd_ref[0].astype(BF16)

    @pl.when(b < n_used_ref[0])
    def _():
        chunks = [c.astype(BF16) for c in _unpack_pieces([xs_ref[i] for i in range(N_PIECES)])]
        g = None
        u = None
        for i, xc in enumerate(chunks):
            rows_i = slice(i * PIECE_COLS, (i + 1) * PIECE_COLS)
            gi = _dot(xc, wg_b[rows_i, :])
            ui = _dot(xc, wu_b[rows_i, :])
            g = gi if g is None else g + gi
            u = ui if u is None else u + ui
        h = (g * _sigmoid(g)) * u
        for i, piece in enumerate(_pack_pieces(_dot(h.astype(BF16), wd_b[...]))):
            yo_ref[i] = piece

    @pl.when(b >= n_used_ref[0])
    def _():
        yo_ref[...] = jnp.zeros(yo_ref.shape, U32)


def _experts(xs, blk_e, blk_first, n_used, w_gate, w_up, w_down):
    p = xs.shape[1]
    nblk = blk_e.shape[0]
    tb = p // nblk
    row_block = pl.BlockSpec((N_PIECES, tb, PIECE_COLS), lambda i, e, f, n: (0, i, 0))
    grid_spec = pltpu.PrefetchScalarGridSpec(
        num_scalar_prefetch=3,
        grid=(nblk,),
        in_specs=[row_block,
                  pl.BlockSpec((1, D_MODEL, D_EXPERT), lambda i, e, f, n: (e[i], 0, 0)),
                  pl.BlockSpec((1, D_MODEL, D_EXPERT), lambda i, e, f, n: (e[i], 0, 0)),
                  pl.BlockSpec((1, D_EXPERT, D_MODEL), lambda i, e, f, n: (e[i], 0, 0))],
        out_specs=row_block,
        scratch_shapes=[pltpu.VMEM((D_MODEL, D_EXPERT), BF16),
                        pltpu.VMEM((D_MODEL, D_EXPERT), BF16),
                        pltpu.VMEM((D_EXPERT, D_MODEL), BF16)],
    )
    return pl.pallas_call(
        _expert_kernel,
        grid_spec=grid_spec,
        out_shape=jax.ShapeDtypeStruct((N_PIECES, p, PIECE_COLS), U32),
        compiler_params=pltpu.CompilerParams(dimension_semantics=("arbitrary",), vmem_limit_bytes=VMEM_LIMIT),
        name="experts",
    )(blk_e, blk_first, n_used, xs, w_gate, w_up, w_down)


def _combine_kernel(x1_ref, yg_ref, route_ref, ga2_ref, fg_ref, o_ref):
    bb, tm, _ = x1_ref.shape
    rows = bb * tm
    route = jnp.concatenate([route_ref[...], jnp.zeros((LANES - SUBLANES, rows), F32)], axis=0)
    w = route.T
    y = None
    for k in range(2):
        chunks = _unpack_pieces([yg_ref[k, i] for i in range(N_PIECES)])
        yk = w[:, 2 + k:3 + k] * jnp.concatenate(chunks, axis=1)
        y = yk if y is None else y + yk
    out = x1_ref[...] + ga2_ref[...] * y.reshape(bb, tm, D_MODEL)
    o_ref[...] = _rms(out) * fg_ref[...]


def _combine(x1, yg, route, ga2, final_g, bb, tm):
    b, t, _ = x1.shape
    nb, nt = b // bb, t // tm
    rows = bb * tm
    return pl.pallas_call(
        _combine_kernel,
        grid=(nb, nt),
        in_specs=[pl.BlockSpec((bb, tm, D_MODEL), lambda i, j: (i, j, 0)),
                  pl.BlockSpec((2, N_PIECES, rows, PIECE_COLS), lambda i, j: (0, 0, i * nt + j, 0)),
                  pl.BlockSpec((SUBLANES, rows), lambda i, j: (0, i * nt + j)),
                  pl.BlockSpec((bb, 1, D_MODEL), lambda i, j: (i, 0, 0)),
                  pl.BlockSpec((1, D_MODEL), lambda i, j: (0, 0))],
        out_specs=pl.BlockSpec((bb, tm, D_MODEL), lambda i, j: (i, j, 0)),
        out_shape=jax.ShapeDtypeStruct((b, t, D_MODEL), F32),
        compiler_params=pltpu.CompilerParams(dimension_semantics=("arbitrary", "arbitrary"),
                                             vmem_limit_bytes=VMEM_LIMIT),
        name="combine",
    )(x1, yg, route, ga2, final_g)


def _block_diag(w):
    n, k, _ = w.shape
    eye = jnp.eye(n, dtype=w.dtype)
    return (eye[:, None, :, None] * w[:, :, None, :]).reshape(n * k, n * k)


def _tiles(f, tile):
    b, _, length = f.shape
    return f.reshape(b, N_HEADS // 2, 2, length // tile, tile).transpose(0, 1, 3, 2, 4)


def _untile_rows(a, b, t):
    nb, nt, r, rows = a.shape
    bb = b // nb
    tm = t // nt
    return a.reshape(nb, nt, r, bb, tm).transpose(2, 0, 3, 1, 4).reshape(r, b, t)


def _group_front(x, c_mod, layer, conv_past, h0, cache):
    b, t, _ = x.shape
    sh1, sc1, ga1, sh2, sc2, ga2 = [m.reshape(b, 1, D_MODEL) for m in jnp.split(c_mod, 6, axis=-1)]
    q, k, v, lf_steps, rec, h_t, ctail = _inproj(
        x, 1.0 + sc1, sh1, layer["n1"], layer["w_all"], layer["b_f"], layer["conv_w"], layer["conv_b"],
        layer["wa_bd"], layer["b_a"], layer["wx_bd"], layer["b_x"], layer["lam"], layer["g_rec"], conv_past, h0)
    lf_t = _untile_rows(lf_steps, b, t).transpose(1, 0, 2)
    if cache is None:
        attn = _attention(q, k, v, _tiles(_cumsum_time(lf_t), min(t, ATTN_CHUNK)))
    else:
        k_t, v_t, lf_past_t = cache
        total = k_t.shape[-1] + t
        padded = -(-total // CUMSUM_CHUNK) * CUMSUM_CHUNK
        lf_all = jnp.concatenate([lf_past_t, lf_t, jnp.zeros((b, N_HEADS, padded - total), F32)], axis=2)
        attn = _attention_cached(q, k, v, _cumsum_time(lf_all), k_t, v_t)
    bb, tm = (1, ROW_TILE) if t >= ROW_TILE else (ROW_TILE // t, t)
    x1, u2p, route = _outproj(attn, rec, x, ga1, 1.0 + sc2, sh2, layer["n2"], layer["g_att"],
                              layer["wo_a"], layer["wo_r"], layer["wr_hi"], layer["wr_mid"], layer["b_r"], bb, tm)
    route = _untile_rows(route, b, t).reshape(SUBLANES, b * t)
    leaves = (k.reshape(1, b, t, N_HEADS, HEAD_DIM), v.reshape(1, b, t, N_HEADS, HEAD_DIM),
              lf_t.transpose(0, 2, 1)[None], h_t[None], ctail[None, :, SUBLANES - (CONV_W - 1):, :])
    combine_tile = (1, COMBINE_TILE) if t >= COMBINE_TILE else (bb, tm)
    return x1, u2p.reshape(N_PIECES, b * t, PIECE_COLS), route, ga2, combine_tile, leaves


def kernel(x_prompt, x_sample, c_prompt, c_sample, cache_k, cache_v, cache_logf, state_h, state_conv, norm1_g, norm2_g, w_ada, b_ada, w_in, b_f, conv_w, conv_b, w_rg_a, b_rg_a, w_rg_x, b_rg_x, rg_lambda, out_g_att, out_g_rec, w_out, w_route_group, b_route_group, w_route_expert, b_route_expert, w_exp_gate, w_exp_up, w_exp_down, final_g):
    bp, tp, _ = x_prompt.shape
    bs, ts, _ = x_sample.shape
    l = 0
    o1, o2, o3 = W_ATT, 2 * W_ATT, 3 * W_ATT
    o4 = o3 + N_HEADS
    o5 = o4 + W_REC
    w_in_l = w_in[l]
    w_all = jnp.concatenate([w_in_l[:, :o3], w_in_l[:, o4:], w_in_l[:, o3:o4],
                             jnp.zeros((D_MODEL, LANES - N_HEADS), F32)], axis=1).astype(BF16)
    w_r = jnp.zeros((D_MODEL, ROUTE_COLS), F32)
    w_r = w_r.at[:, 0:N_GROUPS].set(w_route_group[l]).at[:, EXPERT_COL0:EXPERT_COL0 + N_EXPERTS].set(w_route_expert[l])
    wr_hi = w_r.astype(BF16)
    wr_mid = (w_r - wr_hi.astype(F32)).astype(BF16)
    b_r = jnp.zeros((1, ROUTE_COLS), F32)
    b_r = b_r.at[0, 0:N_GROUPS].set(b_route_group[l]).at[0, N_GROUPS:SUBLANES].set(NEG_BIG)
    b_r = b_r.at[0, EXPERT_COL0:EXPERT_COL0 + N_EXPERTS].set(b_route_expert[l])
    layer = {
        "n1": norm1_g[l].reshape(1, D_MODEL), "n2": norm2_g[l].reshape(1, D_MODEL),
        "w_all": w_all,
        "b_f": jnp.concatenate([b_f[l], jnp.zeros((LANES - N_HEADS,), F32)]).reshape(1, LANES),
        "conv_w": conv_w[l], "conv_b": conv_b[l].reshape(1, W_REC),
        "wa_bd": _block_diag(w_rg_a[l]).astype(BF16), "b_a": b_rg_a[l].reshape(1, W_REC),
        "wx_bd": _block_diag(w_rg_x[l]).astype(BF16), "b_x": b_rg_x[l].reshape(1, W_REC),
        "lam": rg_lambda[l].reshape(1, W_REC), "g_rec": out_g_rec[l].reshape(1, W_REC),
        "g_att": out_g_att[l].reshape(1, W_ATT),
        "wo_a": w_out[l, :W_ATT].astype(BF16), "wo_r": w_out[l, W_ATT:].astype(BF16),
        "wr_hi": wr_hi, "wr_mid": wr_mid, "b_r": b_r,
    }

    ada = _ada(jnp.concatenate([c_prompt, c_sample], axis=0), w_ada[l], b_ada[l])
    pad_rows = SUBLANES - (CONV_W - 1)
    x1p, u2p_p, route_p, ga2p, tile_p, leaves_p = _group_front(
        x_prompt, ada[:bp], layer, jnp.zeros((bp, SUBLANES, W_REC), F32), jnp.zeros((bp, W_REC), F32), None)
    conv_past_s = jnp.concatenate([jnp.zeros((bs, pad_rows, W_REC), F32), state_conv[l]], axis=1)
    x1s, u2p_s, route_s, ga2s, tile_s, leaves_s = _group_front(
        x_sample, ada[bp:], layer, conv_past_s, state_h[l],
        (cache_k[l].transpose(0, 2, 3, 1), cache_v[l].transpose(0, 2, 3, 1), cache_logf[l].transpose(0, 2, 1)))

    fg = final_g.reshape(1, D_MODEL)
    weights = (w_exp_gate[l], w_exp_up[l], w_exp_down[l])
    yg_p = _moe(u2p_p, route_p, weights, EXPERT_BLOCK)
    yg_s = _moe(u2p_s, route_s, weights, EXPERT_BLOCK_SMALL)
    y_prompt = _combine(x1p, yg_p, route_p, ga2p, fg, *tile_p)
    y_sample = _combine(x1s, yg_s, route_s, ga2s, fg, *tile_s)
    return (y_prompt, y_sample) + leaves_p + leaves_s


def _moe(u2p, route, weights, tb):
    n_tok = route.shape[1]
    eid = route[0:2].astype(jnp.int32).reshape(-1)
    n_slot = 2 * n_tok
    experts = jnp.arange(N_EXPERTS, dtype=jnp.int32)
    onehot = (eid[:, None] == experts[None, :]).astype(jnp.int32)
    csum = jnp.cumsum(onehot, axis=0)
    counts = csum[-1]
    rank = jnp.sum(csum * onehot, axis=1) - 1
    padded = ((counts + tb - 1) // tb) * tb
    pend = jnp.cumsum(padded)
    pstart = pend - padded
    dest = jnp.sum(onehot * pstart[None, :], axis=1) + rank
    nblk = -(-n_slot // tb) + N_EXPERTS
    n_rows = nblk * tb
    blk_row0 = jnp.arange(nblk, dtype=jnp.int32) * tb
    blk_e = jnp.minimum(jnp.sum((pend[None, :] <= blk_row0[:, None]).astype(jnp.int32), axis=1), N_EXPERTS - 1)
    blk_first = jnp.concatenate([jnp.ones((1,), jnp.int32), (blk_e[1:] != blk_e[:-1]).astype(jnp.int32)])
    n_used = (pend[-1:] // tb).astype(jnp.int32)
    pad_e = padded - counts
    cpad = jnp.cumsum(pad_e)
    j = jnp.arange(n_rows - n_slot, dtype=jnp.int32)
    owner = (cpad[None, :] <= j[:, None]).astype(jnp.int32)
    e_j = jnp.minimum(jnp.sum(owner, axis=1), N_EXPERTS - 1)
    sel = (e_j[:, None] == experts[None, :]).astype(jnp.int32)
    in_expert = jnp.sum(sel * (pstart + counts - (cpad - pad_e))[None, :], axis=1) + j
    dest_pad = jnp.where(j < cpad[-1], in_expert, pend[-1] + (j - cpad[-1])).reshape(1, -1)

    dests = [dest[k * n_tok:(k + 1) * n_tok].reshape(1, n_tok) for k in range(2)]
    xs = _dispatch(u2p, dests, dest_pad, n_rows)
    yo = _experts(xs, blk_e, blk_first, n_used, *weights)
    return _collect(yo, dests)
```

```python
import functools
import math

import jax
import jax.numpy as jnp
from jax import lax
from jax.experimental import pallas as pl
from jax.experimental.pallas import tpu as pltpu
from jax.experimental.pallas import tpu_sc as plsc

F32 = jnp.float32
BF16 = jnp.bfloat16
U32 = jnp.uint32

D_MODEL = 1024
N_HEADS = 8
HEAD_DIM = 64
W_ATT = N_HEADS * HEAD_DIM
W_REC = D_MODEL - W_ATT
N_REC_BLOCKS = 8
REC_BLOCK = W_REC // N_REC_BLOCKS
CONV_W = 4
RG_C = 8.0
N_GROUPS = 4
EXPERTS_PER_GROUP = 8
N_EXPERTS = N_GROUPS * EXPERTS_PER_GROUP
D_EXPERT = 256
EPS = 1e-6

LANES = 128
SUBLANES = 8
LOG2E = math.log2(math.e)
Q_SCALE = LOG2E * HEAD_DIM ** -0.5
NEG_BIG = -1e30
VMEM_LIMIT = 48 * 1024 * 1024

SEQ_ROWS = 64
BATCH_ROWS = 8
CUMSUM_CHUNK = 512
ATTN_TILE = 2048
ATTN_CHUNK = 512
ATTN_UNROLL = 2
ROW_TILE = 512
EXPERT_BLOCK = 512
EXPERT_BLOCK_SMALL = 128
COMBINE_TILE = 1024
GATHER_WINDOW = 128
PIECE_COLS = 256
N_PIECES = D_MODEL // (2 * PIECE_COLS)
ROUTE_COLS = 128
EXPERT_COL0 = 8


def _split3(x):
    hi = x.astype(BF16)
    r1 = x - hi.astype(F32)
    mid = r1.astype(BF16)
    lo = (r1 - mid.astype(F32)).astype(BF16)
    return hi, mid, lo


def _dot(a, b):
    return jnp.dot(a, b, preferred_element_type=F32)


def _rms(x):
    return x * lax.rsqrt(jnp.mean(x * x, axis=-1, keepdims=True) + EPS)


def _sigmoid(x):
    return 0.5 * jnp.tanh(0.5 * x) + 0.5


def _pack_bf16_pairs(y):
    n = y.shape[-1] // 2
    yb = y.astype(BF16).astype(F32)
    lo = pltpu.bitcast(yb[:, :n], U32)
    hi = pltpu.bitcast(yb[:, n:], U32)
    return (lo >> 16) | hi


def _unpack_bf16_pairs(p):
    lo = pltpu.bitcast(p << 16, F32)
    hi = pltpu.bitcast(p & jnp.uint32(0xFFFF0000), F32)
    return lo, hi


def _pack_pieces(y):
    w = 2 * PIECE_COLS
    return [_pack_bf16_pairs(y[:, i * w:(i + 1) * w]) for i in range(N_PIECES)]


def _unpack_pieces(pieces):
    chunks = []
    for p in pieces:
        chunks.extend(_unpack_bf16_pairs(p))
    return chunks


def _ada_kernel(c_ref, w_ref, b_ref, o_ref):
    c = c_ref[...]
    a = c * jax.nn.sigmoid(c)
    a_hi, a_mid, _ = _split3(a)
    w_hi, w_mid, _ = _split3(w_ref[...])
    o_ref[...] = _dot(a_hi, w_hi) + _dot(a_mid, w_hi) + _dot(a_hi, w_mid) + b_ref[...]


def _ada(c, w_ada, b_ada):
    rows = c.shape[0]
    n = w_ada.shape[1]
    tn = 1536
    return pl.pallas_call(
        _ada_kernel,
        grid=(n // tn,),
        in_specs=[pl.BlockSpec((rows, D_MODEL), lambda j: (0, 0)),
                  pl.BlockSpec((D_MODEL, tn), lambda j: (0, j)),
                  pl.BlockSpec((1, tn), lambda j: (0, j))],
        out_specs=pl.BlockSpec((rows, tn), lambda j: (0, j)),
        out_shape=jax.ShapeDtypeStruct((rows, n), F32),
        compiler_params=pltpu.CompilerParams(vmem_limit_bytes=VMEM_LIMIT),
        name="ada",
    )(c, w_ada, b_ada.reshape(1, n))


def _inproj_kernel(x_ref, g1_ref, sh1_ref, n1_ref, w_ref, bf_ref, cw_ref, cb_ref, wa_ref, ba_ref, wx_ref, bx_ref,
                   lam_ref, grec_ref, cpast_ref, h0_ref,
                   q_ref, k_ref, v_ref, lf_ref, rec_ref, ht_ref, ctail_ref,
                   xbuf, ub_scr, z_scr, xc_scr, pr_scr, pi_scr, a_scr, u_scr, hs_scr, gl_scr, h_scr):
    bb, tm, _ = x_ref.shape
    rows = bb * tm
    n_lane_tiles = W_REC // LANES
    half = W_REC // 2
    o_x = 3 * W_ATT

    @pl.when(pl.program_id(1) == 0)
    def _():
        xbuf[:, 0:SUBLANES, :] = cpast_ref[...]
        h_scr[...] = h0_ref[...]

    mod = n1_ref[...] * g1_ref[...]
    for b in range(bb):
        rb = slice(b * tm, (b + 1) * tm)
        ub_scr[rb, :] = (_rms(x_ref[b]) * mod[b] + sh1_ref[b]).astype(BF16)

    z_scr[...] = _dot(ub_scr[...], w_ref[:, o_x:])

    cw = cw_ref[...]
    for b in range(bb):
        rb = slice(b * tm, (b + 1) * tm)
        xb = z_scr[rb, 0:W_REC]
        xbuf[b, SUBLANES:SUBLANES + tm, :] = xb
        xc = cb_ref[...] + cw[3:4, :] * xb
        for j in range(CONV_W - 1):
            off = SUBLANES - (CONV_W - 1) + j
            xc = xc + cw[j:j + 1, :] * xbuf[b, off:off + tm, :]
        tail = xbuf[b, tm:tm + SUBLANES, :]
        ctail_ref[b] = tail
        xbuf[b, 0:SUBLANES, :] = tail
        xc_scr[rb, :] = xc

    xcb = xc_scr[...].astype(BF16)
    for w_gate, pre in ((wa_ref, pr_scr), (wx_ref, pi_scr)):
        pre[:, :half] = _dot(xcb[:, :half], w_gate[:half, :half])
        pre[:, half:] = _dot(xcb[:, half:], w_gate[half:, half:])

    zq = _dot(ub_scr[...], w_ref[:, :o_x])
    q_ref[...] = (zq[:, 0:W_ATT] * Q_SCALE).reshape(bb, tm, W_ATT).astype(BF16)
    k_ref[...] = zq[:, W_ATT:2 * W_ATT].reshape(bb, tm, W_ATT)
    v_ref[...] = zq[:, 2 * W_ATT:3 * W_ATT].reshape(bb, tm, W_ATT)

    zf = z_scr[:, 2 * W_REC:2 * W_REC + LANES] + bf_ref[...]
    lf = jnp.minimum(zf, 0.0) - jnp.log1p(jnp.exp(-jnp.abs(zf)))
    lf_ref[0, 0] = lf.T[0:N_HEADS, :]

    nlam = -lam_ref[...]
    decay = -RG_C * (jnp.maximum(nlam, 0.0) + jnp.log1p(jnp.exp(-jnp.abs(nlam))))
    for b in range(bb):
        rb = slice(b * tm, (b + 1) * tm)
        r = _sigmoid(pr_scr[rb, :] + ba_ref[...])
        i = _sigmoid(pi_scr[rb, :] + bx_ref[...])
        log_a = decay * r
        t = jnp.tanh(log_a)
        neg_expm1 = -2.0 * t / (1.0 - t)
        root = jnp.where(neg_expm1 > 0.0, neg_expm1 * lax.rsqrt(neg_expm1), 0.0)
        pr_scr[rb, :] = jnp.exp(log_a)
        pi_scr[rb, :] = root * (i * xc_scr[rb, :])
        gb = z_scr[rb, W_REC:2 * W_REC]
        gl_scr[rb, :] = 0.5 * gb * (1.0 + jnp.tanh(math.sqrt(2.0 / math.pi) * (gb + 0.044715 * (gb * gb * gb))))

    def lane_tile(ref, c):
        return ref[:, c * LANES:(c + 1) * LANES].reshape(bb, tm, LANES)

    for c in range(n_lane_tiles):
        a_scr[c] = jnp.swapaxes(lane_tile(pr_scr, c), 0, 1)
        u_scr[c] = jnp.swapaxes(lane_tile(pi_scr, c), 0, 1)
    for c in range(n_lane_tiles):
        h = h_scr[:, c * LANES:(c + 1) * LANES]
        for step in range(tm):
            h = a_scr[c, step] * h + u_scr[c, step]
            hs_scr[c, step] = h
        h_scr[:, c * LANES:(c + 1) * LANES] = h
    ht_ref[...] = h_scr[...]
    for c in range(n_lane_tiles):
        pr_scr[:, c * LANES:(c + 1) * LANES] = jnp.swapaxes(hs_scr[c], 0, 1).reshape(rows, LANES)

    for b in range(bb):
        rb = slice(b * tm, (b + 1) * tm)
        y = gl_scr[rb, :] * pr_scr[rb, :]
        rec_ref[b] = (_rms(y) * grec_ref[...]).astype(BF16)


def _inproj(x, g1, sh1, n1, w_all, b_f, conv_w, conv_b, wa_bd, b_a, wx_bd, b_x, lam, g_rec, conv_past, h0):
    b, t, _ = x.shape
    bb, tm = BATCH_ROWS, SEQ_ROWS
    nb, nt = b // bb, t // tm
    rows = bb * tm
    wcols = w_all.shape[1]
    const2 = lambda shape: pl.BlockSpec(shape, lambda i, j: (0, 0))
    per_b = lambda last: pl.BlockSpec((bb, 1, last), lambda i, j: (i, 0, 0))
    seq = lambda last: pl.BlockSpec((bb, tm, last), lambda i, j: (i, j, 0))
    return pl.pallas_call(
        _inproj_kernel,
        grid=(nb, nt),
        in_specs=[seq(D_MODEL), per_b(D_MODEL), per_b(D_MODEL), const2((1, D_MODEL)),
                  const2((D_MODEL, wcols)), const2((1, LANES)),
                  const2((CONV_W, W_REC)), const2((1, W_REC)),
                  const2((W_REC, W_REC)), const2((1, W_REC)), const2((W_REC, W_REC)), const2((1, W_REC)),
                  const2((1, W_REC)), const2((1, W_REC)),
                  pl.BlockSpec((bb, SUBLANES, W_REC), lambda i, j: (i, 0, 0)),
                  pl.BlockSpec((bb, W_REC), lambda i, j: (i, 0))],
        out_specs=[seq(W_ATT), seq(W_ATT), seq(W_ATT),
                   pl.BlockSpec((1, 1, N_HEADS, rows), lambda i, j: (i, j, 0, 0)),
                   seq(W_REC),
                   pl.BlockSpec((bb, W_REC), lambda i, j: (i, 0)),
                   pl.BlockSpec((bb, SUBLANES, W_REC), lambda i, j: (i, 0, 0))],
        out_shape=[jax.ShapeDtypeStruct((b, t, W_ATT), BF16),
                   jax.ShapeDtypeStruct((b, t, W_ATT), F32),
                   jax.ShapeDtypeStruct((b, t, W_ATT), F32),
                   jax.ShapeDtypeStruct((nb, nt, N_HEADS, rows), F32),
                   jax.ShapeDtypeStruct((b, t, W_REC), BF16),
                   jax.ShapeDtypeStruct((b, W_REC), F32),
                   jax.ShapeDtypeStruct((b, SUBLANES, W_REC), F32)],
        scratch_shapes=[pltpu.VMEM((bb, tm + SUBLANES, W_REC), F32),
                        pltpu.VMEM((rows, D_MODEL), BF16),
                        pltpu.VMEM((rows, 2 * W_REC + LANES), F32),
                        pltpu.VMEM((rows, W_REC), F32),
                        pltpu.VMEM((rows, W_REC), F32),
                        pltpu.VMEM((rows, W_REC), F32),
                        pltpu.VMEM((W_REC // LANES, tm, bb, LANES), F32),
                        pltpu.VMEM((W_REC // LANES, tm, bb, LANES), F32),
                        pltpu.VMEM((W_REC // LANES, tm, bb, LANES), F32),
                        pltpu.VMEM((rows, W_REC), F32),
                        pltpu.VMEM((bb, W_REC), F32)],
        compiler_params=pltpu.CompilerParams(dimension_semantics=("arbitrary", "arbitrary"),
                                             vmem_limit_bytes=VMEM_LIMIT),
        name="inproj",
    )(x, g1, sh1, n1, w_all, b_f, conv_w, conv_b, wa_bd, b_a, wx_bd, b_x, lam, g_rec, conv_past, h0)


def _cumsum_kernel(x_ref, o_ref):
    length = x_ref.shape[2]
    c = CUMSUM_CHUNK
    upper = (lax.broadcasted_iota(jnp.int32, (c, c), 0) <= lax.broadcasted_iota(jnp.int32, (c, c), 1)).astype(BF16)
    carry = jnp.zeros((N_HEADS, 1), F32)
    for j in range(length // c):
        x = x_ref[0, :, j * c:(j + 1) * c]
        hi, mid, lo = _split3(x)
        parts = jnp.concatenate([hi.astype(F32), mid.astype(F32), lo.astype(F32), jnp.zeros_like(x)], axis=0)
        sums = _dot(parts.astype(BF16), upper)
        out = sums[0:8] + sums[8:16] + sums[16:24] + carry
        o_ref[0, :, j * c:(j + 1) * c] = out
        carry = out[:, c - 1:c]


def _cumsum_time(lf_t):
    b, h, length = lf_t.shape
    return pl.pallas_call(
        _cumsum_kernel,
        grid=(b,),
        in_specs=[pl.BlockSpec((1, h, length), lambda i: (i, 0, 0))],
        out_specs=pl.BlockSpec((1, h, length), lambda i: (i, 0, 0)),
        out_shape=jax.ShapeDtypeStruct((b, h, length), F32),
        name="cumsum",
    )(lf_t)


def _attn_tile(qh, kt, vt, bias, m_ref, acc_ref, mask):
    s = lax.dot_general(qh, kt, (((1,), (1,)), ((), ())), preferred_element_type=F32) + bias
    if mask is not None:
        s = jnp.where(mask, s, NEG_BIG)
    width = min(LANES, s.shape[1])
    slabs = [s[:, c * width:(c + 1) * width] for c in range(s.shape[1] // width)]
    part = slabs[0]
    for sl in slabs[1:]:
        part = jnp.maximum(part, sl)
    m = m_ref[...]
    m_new = jnp.maximum(m, jnp.broadcast_to(jnp.max(part, axis=1, keepdims=True), m.shape))
    p = jnp.concatenate([jnp.exp2(sl - m_new[:, :width]) for sl in slabs], axis=1)
    acc_ref[...] = jnp.exp2(m - m_new) * acc_ref[...] + _dot(p.astype(BF16), vt)
    m_ref[...] = m_new


def _attn_kernel(q_ref, k_ref, v_ref, f_ref, o_ref, kb, vb, m_scr, acc_scr, *, tq, ck):
    qi = pl.program_id(2)
    chunks_per_tile = tq // ck

    @pl.when(qi == 0)
    def _():
        v_f32 = v_ref[0]
        first = lax.broadcasted_iota(jnp.int32, v_f32.shape, 1) < HEAD_DIM
        kb[...] = k_ref[0].astype(BF16)
        vb[0] = jnp.where(first, v_f32, 1.0).astype(BF16)
        vb[1] = jnp.where(first, 1.0, v_f32).astype(BF16)

    q = q_ref[0]
    first_head = lax.broadcasted_iota(jnp.int32, q.shape, 1) < HEAD_DIM
    qh = (jnp.where(first_head, q, jnp.zeros_like(q)), jnp.where(first_head, jnp.zeros_like(q), q))

    ref_f = f_ref[0, 0, qi * chunks_per_tile][:, 0:1]
    m_scr[...] = jnp.full(m_scr.shape, NEG_BIG, F32)
    acc_scr[...] = jnp.zeros(acc_scr.shape, F32)

    def chunk(j, row0=0, masked=False):
        start = pl.multiple_of(j * ck, ck)
        kt = kb[pl.ds(start, ck), :]
        f_tile = f_ref[0, 0, j]
        mask = None
        if masked:
            shape = (tq - row0, ck)
            mask = lax.broadcasted_iota(jnp.int32, shape, 1) <= lax.broadcasted_iota(jnp.int32, shape, 0)
        for hd in range(2):
            bias = (ref_f[hd:hd + 1, :] - f_tile[hd:hd + 1, :]) * LOG2E
            _attn_tile(qh[hd][row0:, :], kt, vb[hd, pl.ds(start, ck), :], bias,
                       m_scr.at[hd, row0:tq], acc_scr.at[hd, row0:tq], mask)

    n_chunks = qi * chunks_per_tile

    def chunk_body(jj, carry):
        for u in range(ATTN_UNROLL):
            chunk(jj * ATTN_UNROLL + u)
        return carry

    lax.fori_loop(0, n_chunks // ATTN_UNROLL, chunk_body, 0)
    for r in range(ATTN_UNROLL - 1):
        @pl.when(r < n_chunks % ATTN_UNROLL)
        def _():
            chunk((n_chunks // ATTN_UNROLL) * ATTN_UNROLL + r)

    for c in range(chunks_per_tile):
        chunk(n_chunks + c, row0=c * ck, masked=True)

    outs = []
    for hd in range(2):
        acc = acc_scr[hd]
        outs.append(acc / pltpu.roll(acc, HEAD_DIM, axis=1))
    o_ref[0] = jnp.where(first_head, outs[0], outs[1]).astype(BF16)


def _attention(q, k, v, f_tiles):
    b, t, _ = q.shape
    tq = min(t, ATTN_TILE)
    ck = f_tiles.shape[-1]
    q_spec = pl.BlockSpec((1, tq, LANES), lambda i, p, j: (i, j, p))
    full = pl.BlockSpec((1, t, LANES), lambda i, p, j: (i, 0, p))
    return pl.pallas_call(
        functools.partial(_attn_kernel, tq=tq, ck=ck),
        grid=(b, N_HEADS // 2, t // tq),
        in_specs=[q_spec, full, full,
                  pl.BlockSpec((1, 1) + f_tiles.shape[2:], lambda i, p, j: (i, p, 0, 0, 0))],
        out_specs=q_spec,
        out_shape=jax.ShapeDtypeStruct((b, t, W_ATT), BF16),
        scratch_shapes=[pltpu.VMEM((t, LANES), BF16), pltpu.VMEM((2, t, LANES), BF16),
                        pltpu.VMEM((2, tq, LANES), F32), pltpu.VMEM((2, tq, LANES), F32)],
        compiler_params=pltpu.CompilerParams(dimension_semantics=("arbitrary", "arbitrary", "arbitrary"),
                                             vmem_limit_bytes=VMEM_LIMIT),
        name="attn",
    )(q, k, v, f_tiles)


def _attn_cached_kernel(q_ref, kn_ref, vn_ref, f_ref, kt_ref, vt_ref, o_ref, *, past):
    t = q_ref.shape[1]
    q = q_ref[0]
    kn = kn_ref[0].astype(BF16)
    vn = vn_ref[0].astype(BF16)
    f = f_ref[0]
    causal = lax.broadcasted_iota(jnp.int32, (t, t), 1) <= lax.broadcasted_iota(jnp.int32, (t, t), 0)
    last_dims = (((1,), (1,)), ((), ()))
    outs = []
    for h in range(N_HEADS):
        cols = slice(h * HEAD_DIM, (h + 1) * HEAD_DIM)
        qh = q[:, cols]
        ref_f = f[h:h + 1, past:past + 1]
        s_p = _dot(qh, kt_ref[0, h].astype(BF16)) + (ref_f - f[h:h + 1, 0:past]) * LOG2E
        s_n = lax.dot_general(qh, kn[:, cols], last_dims, preferred_element_type=F32)
        s_n = jnp.where(causal, s_n + (ref_f - f[h:h + 1, past:past + t]) * LOG2E, NEG_BIG)
        m = jnp.maximum(jnp.max(s_p, axis=1, keepdims=True), jnp.max(s_n, axis=1, keepdims=True))
        p_p = jnp.exp2(s_p - m)
        p_n = jnp.exp2(s_n - m)
        l = jnp.sum(p_p, axis=1, keepdims=True) + jnp.sum(p_n, axis=1, keepdims=True)
        o = lax.dot_general(p_p.astype(BF16), vt_ref[0, h].astype(BF16), last_dims, preferred_element_type=F32)
        o = o + _dot(p_n.astype(BF16), vn[:, cols])
        outs.append(o / l)
    o_ref[0] = jnp.concatenate(outs, axis=1).astype(BF16)


def _attention_cached(q, k_new, v_new, f_all, k_t, v_t):
    b, t, _ = q.shape
    past = k_t.shape[-1]
    rows = pl.BlockSpec((1, t, W_ATT), lambda i: (i, 0, 0))
    cache = pl.BlockSpec((1, N_HEADS, HEAD_DIM, past), lambda i: (i, 0, 0, 0))
    return pl.pallas_call(
        functools.partial(_attn_cached_kernel, past=past),
        grid=(b,),
        in_specs=[rows, rows, rows, pl.BlockSpec((1,) + f_all.shape[1:], lambda i: (i, 0, 0)), cache, cache],
        out_specs=rows,
        out_shape=jax.ShapeDtypeStruct((b, t, W_ATT), BF16),
        compiler_params=pltpu.CompilerParams(dimension_semantics=("arbitrary",), vmem_limit_bytes=VMEM_LIMIT),
        name="attn_cached",
    )(q, k_new, v_new, f_all, k_t, v_t)


def _outproj_kernel(attn_ref, rec_ref, x_ref, ga1_ref, g2_ref, sh2_ref, n2_ref, gatt_ref, woa_ref, wor_ref,
                    wrh_ref, wrm_ref, br_ref,
                    x1_ref, u2p_ref, route_ref):
    bb, tm, _ = x_ref.shape
    rows = bb * tm
    attn = attn_ref[...].astype(F32)
    an = (_rms(attn) * gatt_ref[...]).reshape(rows, W_ATT).astype(BF16)
    mix = _dot(an, woa_ref[...]) + _dot(rec_ref[...].reshape(rows, W_REC), wor_ref[...])
    x1 = x_ref[...] + ga1_ref[...] * mix.reshape(bb, tm, D_MODEL)
    x1_ref[...] = x1
    u2 = (_rms(x1) * (n2_ref[...] * g2_ref[...]) + sh2_ref[...]).reshape(rows, D_MODEL)
    for i, piece in enumerate(_pack_pieces(u2)):
        u2p_ref[i] = piece.reshape(bb, tm, PIECE_COLS)

    u_hi, u_mid, _ = _split3(u2)
    logits = _dot(u_hi, wrh_ref[...]) + _dot(u_mid, wrh_ref[...]) + _dot(u_hi, wrm_ref[...]) + br_ref[...]
    lt = logits.T
    row8 = lax.broadcasted_iota(jnp.int32, (SUBLANES, rows), 0).astype(F32)
    lg = lt[0:SUBLANES]
    m_g = jnp.max(lg, axis=0, keepdims=True)
    gidx = jnp.min(jnp.where(lg == m_g, row8, float(SUBLANES)), axis=0, keepdims=True)
    p_top = 1.0 / jnp.sum(jnp.exp(lg - m_g), axis=0, keepdims=True)
    leg = jnp.zeros((EXPERTS_PER_GROUP, rows), F32)
    for g in range(N_GROUPS):
        lo = EXPERT_COL0 + g * EXPERTS_PER_GROUP
        leg = jnp.where(gidx == float(g), lt[lo:lo + EXPERTS_PER_GROUP], leg)
    v1 = jnp.max(leg, axis=0, keepdims=True)
    i1 = jnp.min(jnp.where(leg == v1, row8, float(SUBLANES)), axis=0, keepdims=True)
    leg2 = jnp.where(row8 == i1, -jnp.inf, leg)
    v2 = jnp.max(leg2, axis=0, keepdims=True)
    i2 = jnp.min(jnp.where(leg2 == v2, row8, float(SUBLANES)), axis=0, keepdims=True)
    e21 = jnp.exp(v2 - v1)
    w1 = p_top / (1.0 + e21)
    w2 = w1 * e21
    base = gidx * float(EXPERTS_PER_GROUP)
    out = jnp.where(row8 == 3.0, w2, 0.0)
    for r_idx, val in ((2.0, w1), (1.0, base + i2), (0.0, base + i1)):
        out = jnp.where(row8 == r_idx, val, out)
    route_ref[0, 0] = out


def _outproj(attn, rec, x, ga1, g2, sh2, n2, g_att, wo_a, wo_r, wr_hi, wr_mid, b_r, bb, tm):
    b, t, _ = x.shape
    nb, nt = b // bb, t // tm
    rows = bb * tm
    const2 = lambda shape: pl.BlockSpec(shape, lambda i, j: (0, 0))
    per_b = pl.BlockSpec((bb, 1, D_MODEL), lambda i, j: (i, 0, 0))
    seq = lambda last: pl.BlockSpec((bb, tm, last), lambda i, j: (i, j, 0))
    return pl.pallas_call(
        _outproj_kernel,
        grid=(nb, nt),
        in_specs=[seq(W_ATT), seq(W_REC), seq(D_MODEL), per_b, per_b, per_b, const2((1, D_MODEL)),
                  const2((1, W_ATT)), const2((W_ATT, D_MODEL)), const2((W_REC, D_MODEL)),
                  const2((D_MODEL, ROUTE_COLS)), const2((D_MODEL, ROUTE_COLS)), const2((1, ROUTE_COLS))],
        out_specs=[seq(D_MODEL),
                   pl.BlockSpec((N_PIECES, bb, tm, PIECE_COLS), lambda i, j: (0, i, j, 0)),
                   pl.BlockSpec((1, 1, SUBLANES, rows), lambda i, j: (i, j, 0, 0))],
        out_shape=[jax.ShapeDtypeStruct((b, t, D_MODEL), F32),
                   jax.ShapeDtypeStruct((N_PIECES, b, t, PIECE_COLS), U32),
                   jax.ShapeDtypeStruct((nb, nt, SUBLANES, rows), F32)],
        compiler_params=pltpu.CompilerParams(dimension_semantics=("arbitrary", "arbitrary"),
                                             vmem_limit_bytes=VMEM_LIMIT),
        name="outproj",
    )(attn, rec, x, ga1, g2, sh2, n2, g_att, wo_a, wo_r, wr_hi, wr_mid, b_r)


_SC_AXES = ("core", "subcore")


def _sc_mesh():
    return plsc.VectorSubcoreMesh(core_axis_name=_SC_AXES[0], subcore_axis_name=_SC_AXES[1])


def _sc_scatter_rows(src_hbm, idx_hbm, n, src_rows, dst_hbm):
    src_blocks = src_rows // GATHER_WINDOW

    def body(x_vmem, i_vmem):
        pltpu.sync_copy(x_vmem, dst_hbm.at[i_vmem.at[0]])

    pltpu.emit_pipeline(
        body,
        grid=(n // GATHER_WINDOW,),
        in_specs=[pl.BlockSpec((GATHER_WINDOW, PIECE_COLS), lambda i: (i % src_blocks, 0)),
                  pl.BlockSpec((1, GATHER_WINDOW), lambda i: (0, i))],
        out_specs=[],
        core_axis_name=_SC_AXES,
        dimension_semantics=(pltpu.PARALLEL,),
    )(src_hbm, idx_hbm)


def _sc_gather_rows(tab_hbm, idx_hbm, n, dst_hbm):
    def body(i_vmem, o_vmem):
        pltpu.sync_copy(tab_hbm.at[i_vmem.at[0]], o_vmem)

    pltpu.emit_pipeline(
        body,
        grid=(n // GATHER_WINDOW,),
        in_specs=[pl.BlockSpec((1, GATHER_WINDOW), lambda i: (0, i))],
        out_specs=[pl.BlockSpec((GATHER_WINDOW, PIECE_COLS), lambda i: (i, 0))],
        core_axis_name=_SC_AXES,
        dimension_semantics=(pltpu.PARALLEL,),
    )(idx_hbm, dst_hbm)


def _dispatch(table, dests, dest_pad, n_rows):
    n = table.shape[1]
    n_pad = dest_pad.shape[1]

    @pl.kernel(out_type=jax.ShapeDtypeStruct((N_PIECES, n_rows, PIECE_COLS), U32), mesh=_sc_mesh(), scratch_types=[])
    def scatter(tab_hbm, d0_hbm, d1_hbm, pad_hbm, o_hbm):
        for piece in range(N_PIECES):
            dst = o_hbm.at[piece]
            for idx_hbm in (d0_hbm, d1_hbm):
                _sc_scatter_rows(tab_hbm.at[piece], idx_hbm, n, n, dst)
            _sc_scatter_rows(tab_hbm.at[piece], pad_hbm, n_pad, n, dst)

    return scatter(table, dests[0], dests[1], dest_pad)


def _collect(yo_pieces, dests):
    n = dests[0].shape[1]

    @pl.kernel(out_type=jax.ShapeDtypeStruct((2, N_PIECES, n, PIECE_COLS), U32), mesh=_sc_mesh(), scratch_types=[])
    def gather(*refs):
        tabs = refs[:N_PIECES]
        d_hbm = refs[N_PIECES:N_PIECES + 2]
        o_hbm = refs[N_PIECES + 2]
        for piece in range(N_PIECES):
            for k in range(2):
                _sc_gather_rows(tabs[piece], d_hbm[k], n, o_hbm.at[k, piece])

    return gather(*yo_pieces, dests[0], dests[1])


N_WCHUNKS = D_MODEL // PIECE_COLS


def _expert_kernel(blk_e_ref, blk_first_ref, n_used_ref, *refs):
    xs_refs = refs[:N_PIECES]
    wg_refs, wu_refs, wd_refs = (refs[N_PIECES + i * N_WCHUNKS:N_PIECES + (i + 1) * N_WCHUNKS] for i in range(3))
    yo_refs = refs[N_PIECES + 3 * N_WCHUNKS:2 * N_PIECES + 3 * N_WCHUNKS]
    wg_b, wu_b, wd_b = refs[2 * N_PIECES + 3 * N_WCHUNKS:]
    b = pl.program_id(0)

    @pl.when(blk_first_ref[b] == 1)
    def _():
        for c in range(N_WCHUNKS):
            wg_b[c] = wg_refs[c][0].astype(BF16)
            wu_b[c] = wu_refs[c][0].astype(BF16)
            wd_b[c] = wd_refs[c][0].astype(BF16)

    @pl.when(b < n_used_ref[0])
    def _():
        chunks = [c.astype(BF16) for c in _unpack_pieces([r[0] for r in xs_refs])]
        g = None
        u = None
        for c, xc in enumerate(chunks):
            gi = _dot(xc, wg_b[c])
            ui = _dot(xc, wu_b[c])
            g = gi if g is None else g + gi
            u = ui if u is None else u + ui
        h = ((g * _sigmoid(g)) * u).astype(BF16)
        y = jnp.concatenate([_dot(h, wd_b[c]) for c in range(N_WCHUNKS)], axis=1)
        for i, piece in enumerate(_pack_pieces(y)):
            yo_refs[i][...] = piece

    @pl.when(b >= n_used_ref[0])
    def _():
        for r in yo_refs:
            r[...] = jnp.zeros(r.shape, U32)


def _experts(xs, blk_e, blk_first, n_used, w_gate, w_up, w_down):
    p = xs.shape[1]
    nblk = blk_e.shape[0]
    tb = p // nblk
    x_specs = [pl.BlockSpec((1, tb, PIECE_COLS), lambda i, e, f, n, c=c: (c, i, 0)) for c in range(N_PIECES)]
    row_chunks = [pl.BlockSpec((1, PIECE_COLS, D_EXPERT), lambda i, e, f, n, c=c: (e[i], c, 0))
                  for c in range(N_WCHUNKS)]
    col_chunks = [pl.BlockSpec((1, D_EXPERT, PIECE_COLS), lambda i, e, f, n, c=c: (e[i], 0, c))
                  for c in range(N_WCHUNKS)]
    grid_spec = pltpu.PrefetchScalarGridSpec(
        num_scalar_prefetch=3,
        grid=(nblk,),
        in_specs=x_specs + row_chunks + row_chunks + col_chunks,
        out_specs=[pl.BlockSpec((tb, PIECE_COLS), lambda i, e, f, n: (i, 0)) for _ in range(N_PIECES)],
        scratch_shapes=[pltpu.VMEM((N_WCHUNKS, PIECE_COLS, D_EXPERT), BF16),
                        pltpu.VMEM((N_WCHUNKS, PIECE_COLS, D_EXPERT), BF16),
                        pltpu.VMEM((N_WCHUNKS, D_EXPERT, PIECE_COLS), BF16)],
    )
    return pl.pallas_call(
        _expert_kernel,
        grid_spec=grid_spec,
        out_shape=[jax.ShapeDtypeStruct((p, PIECE_COLS), U32) for _ in range(N_PIECES)],
        compiler_params=pltpu.CompilerParams(dimension_semantics=("arbitrary",), vmem_limit_bytes=VMEM_LIMIT),
        name="experts",
    )(blk_e, blk_first, n_used, *([xs] * N_PIECES), *([w_gate] * N_WCHUNKS), *([w_up] * N_WCHUNKS),
      *([w_down] * N_WCHUNKS))


COMBINE_CHUNKS = 4


def _combine_kernel(*refs, bb, tm):
    x1_refs = refs[:COMBINE_CHUNKS]
    yg_refs = refs[COMBINE_CHUNKS:COMBINE_CHUNKS + 2 * N_PIECES]
    route_ref, ga2_ref, fg_ref, o_hbm, obuf, sems = refs[COMBINE_CHUNKS + 2 * N_PIECES:]
    rows = bb * tm
    ch = rows // COMBINE_CHUNKS
    step = pl.program_id(0) * pl.num_programs(1) + pl.program_id(1)
    n_steps = pl.num_programs(0) * pl.num_programs(1)

    def out_copy(st, c):
        return pltpu.make_async_copy(obuf.at[st % 2, pl.ds(c * ch, ch), :],
                                     o_hbm.at[pl.ds(st * rows + c * ch, ch), :], sems.at[st % 2, c])

    @pl.when(step >= 2)
    def _():
        for c in range(COMBINE_CHUNKS):
            out_copy(step - 2, c).wait()

    route = jnp.concatenate([route_ref[...], jnp.zeros((LANES - SUBLANES, rows), F32)], axis=0)
    w = route.T
    y = None
    for k in range(2):
        chunks = _unpack_pieces([yg_refs[k * N_PIECES + i][0, 0] for i in range(N_PIECES)])
        yk = w[:, 2 + k:3 + k] * jnp.concatenate(chunks, axis=1)
        y = yk if y is None else y + yk
    x1 = jnp.concatenate([r[...] for r in x1_refs], axis=0)
    out = x1.reshape(bb, tm, D_MODEL) + ga2_ref[...] * y.reshape(bb, tm, D_MODEL)
    obuf[step % 2] = (_rms(out) * fg_ref[...]).reshape(rows, D_MODEL)
    for c in range(COMBINE_CHUNKS):
        out_copy(step, c).start()

    @pl.when(step == n_steps - 1)
    def _():
        for c in range(COMBINE_CHUNKS):
            out_copy(step, c).wait()

        @pl.when(step >= 1)
        def _():
            for c in range(COMBINE_CHUNKS):
                out_copy(step - 1, c).wait()


def _combine(x1, yg, route, ga2, final_g, bb, tm):
    b, t, _ = x1.shape
    nb, nt = b // bb, t // tm
    rows = bb * tm
    ch = rows // COMBINE_CHUNKS
    x1_specs = [pl.BlockSpec((ch, D_MODEL), lambda i, j, c=c: ((i * nt + j) * COMBINE_CHUNKS + c, 0))
                for c in range(COMBINE_CHUNKS)]
    yg_specs = [pl.BlockSpec((1, 1, rows, PIECE_COLS), lambda i, j, k=k, c=c: (k, c, i * nt + j, 0))
                for k in range(2) for c in range(N_PIECES)]
    out = pl.pallas_call(
        functools.partial(_combine_kernel, bb=bb, tm=tm),
        grid=(nb, nt),
        in_specs=x1_specs + yg_specs + [
                  pl.BlockSpec((SUBLANES, rows), lambda i, j: (0, i * nt + j)),
                  pl.BlockSpec((bb, 1, D_MODEL), lambda i, j: (i, 0, 0)),
                  pl.BlockSpec((1, D_MODEL), lambda i, j: (0, 0))],
        out_specs=pl.BlockSpec(memory_space=pl.ANY),
        out_shape=jax.ShapeDtypeStruct((b * t, D_MODEL), F32),
        scratch_shapes=[pltpu.VMEM((2, rows, D_MODEL), F32), pltpu.SemaphoreType.DMA((2, COMBINE_CHUNKS))],
        compiler_params=pltpu.CompilerParams(dimension_semantics=("arbitrary", "arbitrary"),
                                             vmem_limit_bytes=VMEM_LIMIT),
        name="combine",
    )(*([x1.reshape(b * t, D_MODEL)] * COMBINE_CHUNKS), *([yg] * (2 * N_PIECES)), route, ga2, final_g)
    return out.reshape(b, t, D_MODEL)


def _block_diag(w):
    n, k, _ = w.shape
    eye = jnp.eye(n, dtype=w.dtype)
    return (eye[:, None, :, None] * w[:, :, None, :]).reshape(n * k, n * k)


def _tiles(f, tile):
    b, _, length = f.shape
    return f.reshape(b, N_HEADS // 2, 2, length // tile, tile).transpose(0, 1, 3, 2, 4)


def _untile_rows(a, b, t):
    nb, nt, r, rows = a.shape
    bb = b // nb
    tm = t // nt
    return a.reshape(nb, nt, r, bb, tm).transpose(2, 0, 3, 1, 4).reshape(r, b, t)


def _group_front(x, c_mod, layer, conv_past, h0, cache):
    b, t, _ = x.shape
    sh1, sc1, ga1, sh2, sc2, ga2 = [m.reshape(b, 1, D_MODEL) for m in jnp.split(c_mod, 6, axis=-1)]
    q, k, v, lf_steps, rec, h_t, ctail = _inproj(
        x, 1.0 + sc1, sh1, layer["n1"], layer["w_all"], layer["b_f"], layer["conv_w"], layer["conv_b"],
        layer["wa_bd"], layer["b_a"], layer["wx_bd"], layer["b_x"], layer["lam"], layer["g_rec"], conv_past, h0)
    lf_t = _untile_rows(lf_steps, b, t).transpose(1, 0, 2)
    if cache is None:
        attn = _attention(q, k, v, _tiles(_cumsum_time(lf_t), min(t, ATTN_CHUNK)))
    else:
        k_t, v_t, lf_past_t = cache
        total = k_t.shape[-1] + t
        padded = -(-total // CUMSUM_CHUNK) * CUMSUM_CHUNK
        lf_all = jnp.concatenate([lf_past_t, lf_t, jnp.zeros((b, N_HEADS, padded - total), F32)], axis=2)
        attn = _attention_cached(q, k, v, _cumsum_time(lf_all), k_t, v_t)
    bb, tm = (1, ROW_TILE) if t >= ROW_TILE else (ROW_TILE // t, t)
    x1, u2p, route = _outproj(attn, rec, x, ga1, 1.0 + sc2, sh2, layer["n2"], layer["g_att"],
                              layer["wo_a"], layer["wo_r"], layer["wr_hi"], layer["wr_mid"], layer["b_r"], bb, tm)
    route = _untile_rows(route, b, t).reshape(SUBLANES, b * t)
    leaves = (k.reshape(1, b, t, N_HEADS, HEAD_DIM), v.reshape(1, b, t, N_HEADS, HEAD_DIM),
              lf_t.transpose(0, 2, 1)[None], h_t[None], ctail[None, :, SUBLANES - (CONV_W - 1):, :])
    combine_tile = (1, COMBINE_TILE) if t >= COMBINE_TILE else (bb, tm)
    return x1, u2p.reshape(N_PIECES, b * t, PIECE_COLS), route, ga2, combine_tile, leaves


def kernel(x_prompt, x_sample, c_prompt, c_sample, cache_k, cache_v, cache_logf, state_h, state_conv, norm1_g, norm2_g, w_ada, b_ada, w_in, b_f, conv_w, conv_b, w_rg_a, b_rg_a, w_rg_x, b_rg_x, rg_lambda, out_g_att, out_g_rec, w_out, w_route_group, b_route_group, w_route_expert, b_route_expert, w_exp_gate, w_exp_up, w_exp_down, final_g):
    bp, tp, _ = x_prompt.shape
    bs, ts, _ = x_sample.shape
    l = 0
    o1, o2, o3 = W_ATT, 2 * W_ATT, 3 * W_ATT
    o4 = o3 + N_HEADS
    o5 = o4 + W_REC
    w_in_l = w_in[l]
    w_all = jnp.concatenate([w_in_l[:, :o3], w_in_l[:, o4:], w_in_l[:, o3:o4],
                             jnp.zeros((D_MODEL, LANES - N_HEADS), F32)], axis=1).astype(BF16)
    w_r = jnp.zeros((D_MODEL, ROUTE_COLS), F32)
    w_r = w_r.at[:, 0:N_GROUPS].set(w_route_group[l]).at[:, EXPERT_COL0:EXPERT_COL0 + N_EXPERTS].set(w_route_expert[l])
    wr_hi = w_r.astype(BF16)
    wr_mid = (w_r - wr_hi.astype(F32)).astype(BF16)
    b_r = jnp.zeros((1, ROUTE_COLS), F32)
    b_r = b_r.at[0, 0:N_GROUPS].set(b_route_group[l]).at[0, N_GROUPS:SUBLANES].set(NEG_BIG)
    b_r = b_r.at[0, EXPERT_COL0:EXPERT_COL0 + N_EXPERTS].set(b_route_expert[l])
    layer = {
        "n1": norm1_g[l].reshape(1, D_MODEL), "n2": norm2_g[l].reshape(1, D_MODEL),
        "w_all": w_all,
        "b_f": jnp.concatenate([b_f[l], jnp.zeros((LANES - N_HEADS,), F32)]).reshape(1, LANES),
        "conv_w": conv_w[l], "conv_b": conv_b[l].reshape(1, W_REC),
        "wa_bd": _block_diag(w_rg_a[l]).astype(BF16), "b_a": b_rg_a[l].reshape(1, W_REC),
        "wx_bd": _block_diag(w_rg_x[l]).astype(BF16), "b_x": b_rg_x[l].reshape(1, W_REC),
        "lam": rg_lambda[l].reshape(1, W_REC), "g_rec": out_g_rec[l].reshape(1, W_REC),
        "g_att": out_g_att[l].reshape(1, W_ATT),
        "wo_a": w_out[l, :W_ATT].astype(BF16), "wo_r": w_out[l, W_ATT:].astype(BF16),
        "wr_hi": wr_hi, "wr_mid": wr_mid, "b_r": b_r,
    }

    ada = _ada(jnp.concatenate([c_prompt, c_sample], axis=0), w_ada[l], b_ada[l])
    pad_rows = SUBLANES - (CONV_W - 1)
    x1p, u2p_p, route_p, ga2p, tile_p, leaves_p = _group_front(
        x_prompt, ada[:bp], layer, jnp.zeros((bp, SUBLANES, W_REC), F32), jnp.zeros((bp, W_REC), F32), None)
    conv_past_s = jnp.concatenate([jnp.zeros((bs, pad_rows, W_REC), F32), state_conv[l]], axis=1)
    x1s, u2p_s, route_s, ga2s, tile_s, leaves_s = _group_front(
        x_sample, ada[bp:], layer, conv_past_s, state_h[l],
        (cache_k[l].transpose(0, 2, 3, 1), cache_v[l].transpose(0, 2, 3, 1), cache_logf[l].transpose(0, 2, 1)))

    fg = final_g.reshape(1, D_MODEL)
    weights = (w_exp_gate[l], w_exp_up[l], w_exp_down[l])
    yg_p = _moe(u2p_p, route_p, weights, EXPERT_BLOCK)
    yg_s = _moe(u2p_s, route_s, weights, EXPERT_BLOCK_SMALL)
    y_prompt = _combine(x1p, yg_p, route_p, ga2p, fg, *tile_p)
    y_sample = _combine(x1s, yg_s, route_s, ga2s, fg, *tile_s)
    return (y_prompt, y_sample) + leaves_p + leaves_s


def _moe(u2p, route, weights, tb):
    n_tok = route.shape[1]
    eid = route[0:2].astype(jnp.int32).reshape(-1)
    n_slot = 2 * n_tok
    experts = jnp.arange(N_EXPERTS, dtype=jnp.int32)
    onehot = (eid[:, None] == experts[None, :]).astype(jnp.int32)
    csum = jnp.cumsum(onehot, axis=0)
    counts = csum[-1]
    rank = jnp.sum(csum * onehot, axis=1) - 1
    padded = ((counts + tb - 1) // tb) * tb
    pend = jnp.cumsum(padded)
    pstart = pend - padded
    dest = jnp.sum(onehot * pstart[None, :], axis=1) + rank
    nblk = -(-n_slot // tb) + N_EXPERTS
    n_rows = nblk * tb
    blk_row0 = jnp.arange(nblk, dtype=jnp.int32) * tb
    blk_e = jnp.minimum(jnp.sum((pend[None, :] <= blk_row0[:, None]).astype(jnp.int32), axis=1), N_EXPERTS - 1)
    blk_first = jnp.concatenate([jnp.ones((1,), jnp.int32), (blk_e[1:] != blk_e[:-1]).astype(jnp.int32)])
    n_used = (pend[-1:] // tb).astype(jnp.int32)
    pad_e = padded - counts
    cpad = jnp.cumsum(pad_e)
    j = jnp.arange(n_rows - n_slot, dtype=jnp.int32)
    owner = (cpad[None, :] <= j[:, None]).astype(jnp.int32)
    e_j = jnp.minimum(jnp.sum(owner, axis=1), N_EXPERTS - 1)
    sel = (e_j[:, None] == experts[None, :]).astype(jnp.int32)
    in_expert = jnp.sum(sel * (pstart + counts - (cpad - pad_e))[None, :], axis=1) + j
    dest_pad = jnp.where(j < cpad[-1], in_expert, pend[-1] + (j - cpad[-1])).reshape(1, -1)

    dests = [dest[k * n_tok:(k + 1) * n_tok].reshape(1, n_tok) for k in range(2)]
    xs = _dispatch(u2p, dests, dest_pad, n_rows)
    yo = _experts(xs, blk_e, blk_first, n_used, *weights)
    return _collect(yo, dests)
```

```python
import functools
import math

import jax
import jax.numpy as jnp
from jax import lax
from jax.experimental import pallas as pl
from jax.experimental.pallas import tpu as pltpu
from jax.experimental.pallas import tpu_sc as plsc

F32 = jnp.float32
BF16 = jnp.bfloat16
U32 = jnp.uint32

D_MODEL = 1024
N_HEADS = 8
HEAD_DIM = 64
W_ATT = N_HEADS * HEAD_DIM
W_REC = D_MODEL - W_ATT
N_REC_BLOCKS = 8
REC_BLOCK = W_REC // N_REC_BLOCKS
CONV_W = 4
RG_C = 8.0
N_GROUPS = 4
EXPERTS_PER_GROUP = 8
N_EXPERTS = N_GROUPS * EXPERTS_PER_GROUP
D_EXPERT = 256
EPS = 1e-6

LANES = 128
SUBLANES = 8
LOG2E = math.log2(math.e)
Q_SCALE = LOG2E * HEAD_DIM ** -0.5
NEG_BIG = -1e30
VMEM_LIMIT = 48 * 1024 * 1024

SEQ_ROWS = 64
BATCH_ROWS = 8
CUMSUM_CHUNK = 512
ATTN_TILE = 2048
ATTN_CHUNK = 512
ATTN_UNROLL = 2
ROW_TILE = 512
EXPERT_BLOCK = 512
COMBINE_TILE = 1024
GATHER_WINDOW = 128
PIECE_COLS = 256
N_PIECES = D_MODEL // (2 * PIECE_COLS)
ROUTE_COLS = 128
EXPERT_COL0 = 8


def _split3(x):
    hi = x.astype(BF16)
    r1 = x - hi.astype(F32)
    mid = r1.astype(BF16)
    lo = (r1 - mid.astype(F32)).astype(BF16)
    return hi, mid, lo


def _dot(a, b):
    return jnp.dot(a, b, preferred_element_type=F32)


def _rms(x):
    return x * lax.rsqrt(jnp.mean(x * x, axis=-1, keepdims=True) + EPS)


def _sigmoid(x):
    return 0.5 * jnp.tanh(0.5 * x) + 0.5


def _pack_bf16_pairs(y):
    n = y.shape[-1] // 2
    yb = y.astype(BF16).astype(F32)
    lo = pltpu.bitcast(yb[:, :n], U32)
    hi = pltpu.bitcast(yb[:, n:], U32)
    return (lo >> 16) | hi


def _unpack_bf16_pairs(p):
    lo = pltpu.bitcast(p << 16, F32)
    hi = pltpu.bitcast(p & jnp.uint32(0xFFFF0000), F32)
    return lo, hi


def _pack_pieces(y):
    w = 2 * PIECE_COLS
    return [_pack_bf16_pairs(y[:, i * w:(i + 1) * w]) for i in range(N_PIECES)]


def _unpack_pieces(pieces):
    chunks = []
    for p in pieces:
        chunks.extend(_unpack_bf16_pairs(p))
    return chunks


def _ada_kernel(c_ref, w_ref, b_ref, o_ref):
    c = c_ref[...]
    a = c * jax.nn.sigmoid(c)
    a_hi, a_mid, _ = _split3(a)
    w_hi, w_mid, _ = _split3(w_ref[...])
    o_ref[...] = _dot(a_hi, w_hi) + _dot(a_mid, w_hi) + _dot(a_hi, w_mid) + b_ref[...]


def _ada(c, w_ada, b_ada):
    rows = c.shape[0]
    n = w_ada.shape[1]
    tn = 1536
    return pl.pallas_call(
        _ada_kernel,
        grid=(n // tn,),
        in_specs=[pl.BlockSpec((rows, D_MODEL), lambda j: (0, 0)),
                  pl.BlockSpec((D_MODEL, tn), lambda j: (0, j)),
                  pl.BlockSpec((1, tn), lambda j: (0, j))],
        out_specs=pl.BlockSpec((rows, tn), lambda j: (0, j)),
        out_shape=jax.ShapeDtypeStruct((rows, n), F32),
        compiler_params=pltpu.CompilerParams(vmem_limit_bytes=VMEM_LIMIT),
        name="ada",
    )(c, w_ada, b_ada.reshape(1, n))


def _inproj_kernel(x_ref, g1_ref, sh1_ref, n1_ref, w_ref, bf_ref, cw_ref, cb_ref, wa_ref, ba_ref, wx_ref, bx_ref,
                   lam_ref, grec_ref, cpast_ref, h0_ref,
                   q_ref, k_ref, v_ref, lf_ref, rec_ref, ht_ref, ctail_ref,
                   xbuf, ub_scr, z_scr, xc_scr, pr_scr, pi_scr, a_scr, u_scr, hs_scr, gl_scr, h_scr):
    bb, tm, _ = x_ref.shape
    rows = bb * tm
    n_lane_tiles = W_REC // LANES
    half = W_REC // 2
    o_x = 3 * W_ATT

    @pl.when(pl.program_id(1) == 0)
    def _():
        xbuf[:, 0:SUBLANES, :] = cpast_ref[...]
        h_scr[...] = h0_ref[...]

    mod = n1_ref[...] * g1_ref[...]
    for b in range(bb):
        rb = slice(b * tm, (b + 1) * tm)
        ub_scr[rb, :] = (_rms(x_ref[b]) * mod[b] + sh1_ref[b]).astype(BF16)

    z_scr[...] = _dot(ub_scr[...], w_ref[:, o_x:])

    cw = cw_ref[...]
    for b in range(bb):
        rb = slice(b * tm, (b + 1) * tm)
        xb = z_scr[rb, 0:W_REC]
        xbuf[b, SUBLANES:SUBLANES + tm, :] = xb
        xc = cb_ref[...] + cw[3:4, :] * xb
        for j in range(CONV_W - 1):
            off = SUBLANES - (CONV_W - 1) + j
            xc = xc + cw[j:j + 1, :] * xbuf[b, off:off + tm, :]
        tail = xbuf[b, tm:tm + SUBLANES, :]
        ctail_ref[b] = tail
        xbuf[b, 0:SUBLANES, :] = tail
        xc_scr[rb, :] = xc

    xcb = xc_scr[...].astype(BF16)
    for w_gate, pre in ((wa_ref, pr_scr), (wx_ref, pi_scr)):
        pre[:, :half] = _dot(xcb[:, :half], w_gate[:half, :half])
        pre[:, half:] = _dot(xcb[:, half:], w_gate[half:, half:])

    zq = _dot(ub_scr[...], w_ref[:, :o_x])
    q_ref[...] = (zq[:, 0:W_ATT] * Q_SCALE).reshape(bb, tm, W_ATT).astype(BF16)
    k_ref[...] = zq[:, W_ATT:2 * W_ATT].reshape(bb, tm, W_ATT)
    v_ref[...] = zq[:, 2 * W_ATT:3 * W_ATT].reshape(bb, tm, W_ATT)

    zf = z_scr[:, 2 * W_REC:2 * W_REC + LANES] + bf_ref[...]
    lf = jnp.minimum(zf, 0.0) - jnp.log1p(jnp.exp(-jnp.abs(zf)))
    lf_ref[0, 0] = lf.T[0:N_HEADS, :]

    nlam = -lam_ref[...]
    decay = -RG_C * (jnp.maximum(nlam, 0.0) + jnp.log1p(jnp.exp(-jnp.abs(nlam))))
    for b in range(bb):
        rb = slice(b * tm, (b + 1) * tm)
        r = _sigmoid(pr_scr[rb, :] + ba_ref[...])
        i = _sigmoid(pi_scr[rb, :] + bx_ref[...])
        log_a = decay * r
        t = jnp.tanh(log_a)
        neg_expm1 = -2.0 * t / (1.0 - t)
        root = jnp.where(neg_expm1 > 0.0, neg_expm1 * lax.rsqrt(neg_expm1), 0.0)
        pr_scr[rb, :] = jnp.exp(log_a)
        pi_scr[rb, :] = root * (i * xc_scr[rb, :])
        gb = z_scr[rb, W_REC:2 * W_REC]
        gl_scr[rb, :] = 0.5 * gb * (1.0 + jnp.tanh(math.sqrt(2.0 / math.pi) * (gb + 0.044715 * (gb * gb * gb))))

    def lane_tile(ref, c):
        return ref[:, c * LANES:(c + 1) * LANES].reshape(bb, tm, LANES)

    for c in range(n_lane_tiles):
        a_scr[c] = jnp.swapaxes(lane_tile(pr_scr, c), 0, 1)
        u_scr[c] = jnp.swapaxes(lane_tile(pi_scr, c), 0, 1)
    for c in range(n_lane_tiles):
        h = h_scr[:, c * LANES:(c + 1) * LANES]
        for step in range(tm):
            h = a_scr[c, step] * h + u_scr[c, step]
            hs_scr[c, step] = h
        h_scr[:, c * LANES:(c + 1) * LANES] = h
    ht_ref[...] = h_scr[...]
    for c in range(n_lane_tiles):
        pr_scr[:, c * LANES:(c + 1) * LANES] = jnp.swapaxes(hs_scr[c], 0, 1).reshape(rows, LANES)

    for b in range(bb):
        rb = slice(b * tm, (b + 1) * tm)
        y = gl_scr[rb, :] * pr_scr[rb, :]
        rec_ref[b] = (_rms(y) * grec_ref[...]).astype(BF16)


def _inproj(x, g1, sh1, n1, w_all, b_f, conv_w, conv_b, wa_bd, b_a, wx_bd, b_x, lam, g_rec, conv_past, h0):
    b, t, _ = x.shape
    bb, tm = BATCH_ROWS, SEQ_ROWS
    nb, nt = b // bb, t // tm
    rows = bb * tm
    wcols = w_all.shape[1]
    const2 = lambda shape: pl.BlockSpec(shape, lambda i, j: (0, 0))
    per_b = lambda last: pl.BlockSpec((bb, 1, last), lambda i, j: (i, 0, 0))
    seq = lambda last: pl.BlockSpec((bb, tm, last), lambda i, j: (i, j, 0))
    return pl.pallas_call(
        _inproj_kernel,
        grid=(nb, nt),
        in_specs=[seq(D_MODEL), per_b(D_MODEL), per_b(D_MODEL), const2((1, D_MODEL)),
                  const2((D_MODEL, wcols)), const2((1, LANES)),
                  const2((CONV_W, W_REC)), const2((1, W_REC)),
                  const2((W_REC, W_REC)), const2((1, W_REC)), const2((W_REC, W_REC)), const2((1, W_REC)),
                  const2((1, W_REC)), const2((1, W_REC)),
                  pl.BlockSpec((bb, SUBLANES, W_REC), lambda i, j: (i, 0, 0)),
                  pl.BlockSpec((bb, W_REC), lambda i, j: (i, 0))],
        out_specs=[seq(W_ATT), seq(W_ATT), seq(W_ATT),
                   pl.BlockSpec((1, 1, N_HEADS, rows), lambda i, j: (i, j, 0, 0)),
                   seq(W_REC),
                   pl.BlockSpec((bb, W_REC), lambda i, j: (i, 0)),
                   pl.BlockSpec((bb, SUBLANES, W_REC), lambda i, j: (i, 0, 0))],
        out_shape=[jax.ShapeDtypeStruct((b, t, W_ATT), BF16),
                   jax.ShapeDtypeStruct((b, t, W_ATT), F32),
                   jax.ShapeDtypeStruct((b, t, W_ATT), F32),
                   jax.ShapeDtypeStruct((nb, nt, N_HEADS, rows), F32),
                   jax.ShapeDtypeStruct((b, t, W_REC), BF16),
                   jax.ShapeDtypeStruct((b, W_REC), F32),
                   jax.ShapeDtypeStruct((b, SUBLANES, W_REC), F32)],
        scratch_shapes=[pltpu.VMEM((bb, tm + SUBLANES, W_REC), F32),
                        pltpu.VMEM((rows, D_MODEL), BF16),
                        pltpu.VMEM((rows, 2 * W_REC + LANES), F32),
                        pltpu.VMEM((rows, W_REC), F32),
                        pltpu.VMEM((rows, W_REC), F32),
                        pltpu.VMEM((rows, W_REC), F32),
                        pltpu.VMEM((W_REC // LANES, tm, bb, LANES), F32),
                        pltpu.VMEM((W_REC // LANES, tm, bb, LANES), F32),
                        pltpu.VMEM((W_REC // LANES, tm, bb, LANES), F32),
                        pltpu.VMEM((rows, W_REC), F32),
                        pltpu.VMEM((bb, W_REC), F32)],
        compiler_params=pltpu.CompilerParams(dimension_semantics=("arbitrary", "arbitrary"),
                                             vmem_limit_bytes=VMEM_LIMIT),
        name="inproj",
    )(x, g1, sh1, n1, w_all, b_f, conv_w, conv_b, wa_bd, b_a, wx_bd, b_x, lam, g_rec, conv_past, h0)


def _cumsum_kernel(x_ref, o_ref):
    length = x_ref.shape[2]
    c = CUMSUM_CHUNK
    upper = (lax.broadcasted_iota(jnp.int32, (c, c), 0) <= lax.broadcasted_iota(jnp.int32, (c, c), 1)).astype(BF16)
    carry = jnp.zeros((N_HEADS, 1), F32)
    for j in range(length // c):
        x = x_ref[0, :, j * c:(j + 1) * c]
        hi, mid, lo = _split3(x)
        parts = jnp.concatenate([hi.astype(F32), mid.astype(F32), lo.astype(F32), jnp.zeros_like(x)], axis=0)
        sums = _dot(parts.astype(BF16), upper)
        out = sums[0:8] + sums[8:16] + sums[16:24] + carry
        o_ref[0, :, j * c:(j + 1) * c] = out
        carry = out[:, c - 1:c]


def _cumsum_time(lf_t):
    b, h, length = lf_t.shape
    return pl.pallas_call(
        _cumsum_kernel,
        grid=(b,),
        in_specs=[pl.BlockSpec((1, h, length), lambda i: (i, 0, 0))],
        out_specs=pl.BlockSpec((1, h, length), lambda i: (i, 0, 0)),
        out_shape=jax.ShapeDtypeStruct((b, h, length), F32),
        name="cumsum",
    )(lf_t)


def _attn_tile(qh, kt, vt, bias, m_ref, acc_ref, mask):
    s = lax.dot_general(qh, kt, (((1,), (1,)), ((), ())), preferred_element_type=F32) + bias
    if mask is not None:
        s = jnp.where(mask, s, NEG_BIG)
    width = min(LANES, s.shape[1])
    slabs = [s[:, c * width:(c + 1) * width] for c in range(s.shape[1] // width)]
    part = slabs[0]
    for sl in slabs[1:]:
        part = jnp.maximum(part, sl)
    m = m_ref[...]
    m_new = jnp.maximum(m, jnp.broadcast_to(jnp.max(part, axis=1, keepdims=True), m.shape))
    p = jnp.concatenate([jnp.exp2(sl - m_new[:, :width]) for sl in slabs], axis=1)
    acc_ref[...] = jnp.exp2(m - m_new) * acc_ref[...] + _dot(p.astype(BF16), vt)
    m_ref[...] = m_new


def _attn_kernel(q_ref, k_ref, v_ref, f_ref, o_ref, kb, vb, m_scr, acc_scr, *, tq, ck):
    qi = pl.program_id(2)
    chunks_per_tile = tq // ck

    @pl.when(qi == 0)
    def _():
        v_f32 = v_ref[0]
        first = lax.broadcasted_iota(jnp.int32, v_f32.shape, 1) < HEAD_DIM
        kb[...] = k_ref[0].astype(BF16)
        vb[0] = jnp.where(first, v_f32, 1.0).astype(BF16)
        vb[1] = jnp.where(first, 1.0, v_f32).astype(BF16)

    q = q_ref[0]
    first_head = lax.broadcasted_iota(jnp.int32, q.shape, 1) < HEAD_DIM
    qh = (jnp.where(first_head, q, jnp.zeros_like(q)), jnp.where(first_head, jnp.zeros_like(q), q))

    ref_f = f_ref[0, 0, qi * chunks_per_tile][:, 0:1]
    m_scr[...] = jnp.full(m_scr.shape, NEG_BIG, F32)
    acc_scr[...] = jnp.zeros(acc_scr.shape, F32)

    def chunk(j, row0=0, masked=False):
        start = pl.multiple_of(j * ck, ck)
        kt = kb[pl.ds(start, ck), :]
        f_tile = f_ref[0, 0, j]
        mask = None
        if masked:
            shape = (tq - row0, ck)
            mask = lax.broadcasted_iota(jnp.int32, shape, 1) <= lax.broadcasted_iota(jnp.int32, shape, 0)
        for hd in range(2):
            bias = (ref_f[hd:hd + 1, :] - f_tile[hd:hd + 1, :]) * LOG2E
            _attn_tile(qh[hd][row0:, :], kt, vb[hd, pl.ds(start, ck), :], bias,
                       m_scr.at[hd, row0:tq], acc_scr.at[hd, row0:tq], mask)

    n_chunks = qi * chunks_per_tile

    def chunk_body(jj, carry):
        for u in range(ATTN_UNROLL):
            chunk(jj * ATTN_UNROLL + u)
        return carry

    lax.fori_loop(0, n_chunks // ATTN_UNROLL, chunk_body, 0)
    for r in range(ATTN_UNROLL - 1):
        @pl.when(r < n_chunks % ATTN_UNROLL)
        def _():
            chunk((n_chunks // ATTN_UNROLL) * ATTN_UNROLL + r)

    for c in range(chunks_per_tile):
        chunk(n_chunks + c, row0=c * ck, masked=True)

    outs = []
    for hd in range(2):
        acc = acc_scr[hd]
        outs.append(acc / pltpu.roll(acc, HEAD_DIM, axis=1))
    o_ref[0] = jnp.where(first_head, outs[0], outs[1]).astype(BF16)


def _attention(q, k, v, f_tiles):
    b, t, _ = q.shape
    tq = min(t, ATTN_TILE)
    ck = f_tiles.shape[-1]
    q_spec = pl.BlockSpec((1, tq, LANES), lambda i, p, j: (i, j, p))
    full = pl.BlockSpec((1, t, LANES), lambda i, p, j: (i, 0, p))
    return pl.pallas_call(
        functools.partial(_attn_kernel, tq=tq, ck=ck),
        grid=(b, N_HEADS // 2, t // tq),
        in_specs=[q_spec, full, full,
                  pl.BlockSpec((1, 1) + f_tiles.shape[2:], lambda i, p, j: (i, p, 0, 0, 0))],
        out_specs=q_spec,
        out_shape=jax.ShapeDtypeStruct((b, t, W_ATT), BF16),
        scratch_shapes=[pltpu.VMEM((t, LANES), BF16), pltpu.VMEM((2, t, LANES), BF16),
                        pltpu.VMEM((2, tq, LANES), F32), pltpu.VMEM((2, tq, LANES), F32)],
        compiler_params=pltpu.CompilerParams(dimension_semantics=("arbitrary", "arbitrary", "arbitrary"),
                                             vmem_limit_bytes=VMEM_LIMIT),
        name="attn",
    )(q, k, v, f_tiles)


def _attn_cached_kernel(q_ref, kn_ref, vn_ref, f_ref, kt_ref, vt_ref, o_ref, *, past):
    t = q_ref.shape[1]
    q = q_ref[0]
    kn = kn_ref[0].astype(BF16)
    vn = vn_ref[0].astype(BF16)
    f = f_ref[0]
    causal = lax.broadcasted_iota(jnp.int32, (t, t), 1) <= lax.broadcasted_iota(jnp.int32, (t, t), 0)
    last_dims = (((1,), (1,)), ((), ()))
    outs = []
    for h in range(N_HEADS):
        cols = slice(h * HEAD_DIM, (h + 1) * HEAD_DIM)
        qh = q[:, cols]
        ref_f = f[h:h + 1, past:past + 1]
        s_p = _dot(qh, kt_ref[0, h].astype(BF16)) + (ref_f - f[h:h + 1, 0:past]) * LOG2E
        s_n = lax.dot_general(qh, kn[:, cols], last_dims, preferred_element_type=F32)
        s_n = jnp.where(causal, s_n + (ref_f - f[h:h + 1, past:past + t]) * LOG2E, NEG_BIG)
        m = jnp.maximum(jnp.max(s_p, axis=1, keepdims=True), jnp.max(s_n, axis=1, keepdims=True))
        p_p = jnp.exp2(s_p - m)
        p_n = jnp.exp2(s_n - m)
        l = jnp.sum(p_p, axis=1, keepdims=True) + jnp.sum(p_n, axis=1, keepdims=True)
        o = lax.dot_general(p_p.astype(BF16), vt_ref[0, h].astype(BF16), last_dims, preferred_element_type=F32)
        o = o + _dot(p_n.astype(BF16), vn[:, cols])
        outs.append(o / l)
    o_ref[0] = jnp.concatenate(outs, axis=1).astype(BF16)


def _attention_cached(q, k_new, v_new, f_all, k_t, v_t):
    b, t, _ = q.shape
    past = k_t.shape[-1]
    rows = pl.BlockSpec((1, t, W_ATT), lambda i: (i, 0, 0))
    cache = pl.BlockSpec((1, N_HEADS, HEAD_DIM, past), lambda i: (i, 0, 0, 0))
    return pl.pallas_call(
        functools.partial(_attn_cached_kernel, past=past),
        grid=(b,),
        in_specs=[rows, rows, rows, pl.BlockSpec((1,) + f_all.shape[1:], lambda i: (i, 0, 0)), cache, cache],
        out_specs=rows,
        out_shape=jax.ShapeDtypeStruct((b, t, W_ATT), BF16),
        compiler_params=pltpu.CompilerParams(dimension_semantics=("arbitrary",), vmem_limit_bytes=VMEM_LIMIT),
        name="attn_cached",
    )(q, k_new, v_new, f_all, k_t, v_t)


def _outproj_kernel(attn_ref, rec_ref, x_ref, ga1_ref, g2_ref, sh2_ref, n2_ref, gatt_ref, woa_ref, wor_ref,
                    wrh_ref, wrm_ref, br_ref,
                    x1_ref, u2p_ref, route_ref):
    bb, tm, _ = x_ref.shape
    rows = bb * tm
    attn = attn_ref[...].astype(F32)
    an = (_rms(attn) * gatt_ref[...]).reshape(rows, W_ATT).astype(BF16)
    mix = _dot(an, woa_ref[...]) + _dot(rec_ref[...].reshape(rows, W_REC), wor_ref[...])
    x1 = x_ref[...] + ga1_ref[...] * mix.reshape(bb, tm, D_MODEL)
    x1_ref[...] = x1
    u2 = (_rms(x1) * (n2_ref[...] * g2_ref[...]) + sh2_ref[...]).reshape(rows, D_MODEL)
    for i, piece in enumerate(_pack_pieces(u2)):
        u2p_ref[i] = piece.reshape(bb, tm, PIECE_COLS)

    u_hi, u_mid, _ = _split3(u2)
    logits = _dot(u_hi, wrh_ref[...]) + _dot(u_mid, wrh_ref[...]) + _dot(u_hi, wrm_ref[...]) + br_ref[...]
    lt = logits.T
    row8 = lax.broadcasted_iota(jnp.int32, (SUBLANES, rows), 0).astype(F32)
    lg = lt[0:SUBLANES]
    m_g = jnp.max(lg, axis=0, keepdims=True)
    gidx = jnp.min(jnp.where(lg == m_g, row8, float(SUBLANES)), axis=0, keepdims=True)
    p_top = 1.0 / jnp.sum(jnp.exp(lg - m_g), axis=0, keepdims=True)
    leg = jnp.zeros((EXPERTS_PER_GROUP, rows), F32)
    for g in range(N_GROUPS):
        lo = EXPERT_COL0 + g * EXPERTS_PER_GROUP
        leg = jnp.where(gidx == float(g), lt[lo:lo + EXPERTS_PER_GROUP], leg)
    v1 = jnp.max(leg, axis=0, keepdims=True)
    i1 = jnp.min(jnp.where(leg == v1, row8, float(SUBLANES)), axis=0, keepdims=True)
    leg2 = jnp.where(row8 == i1, -jnp.inf, leg)
    v2 = jnp.max(leg2, axis=0, keepdims=True)
    i2 = jnp.min(jnp.where(leg2 == v2, row8, float(SUBLANES)), axis=0, keepdims=True)
    e21 = jnp.exp(v2 - v1)
    w1 = p_top / (1.0 + e21)
    w2 = w1 * e21
    base = gidx * float(EXPERTS_PER_GROUP)
    out = jnp.where(row8 == 3.0, w2, 0.0)
    for r_idx, val in ((2.0, w1), (1.0, base + i2), (0.0, base + i1)):
        out = jnp.where(row8 == r_idx, val, out)
    route_ref[0, 0] = out


def _outproj(attn, rec, x, ga1, g2, sh2, n2, g_att, wo_a, wo_r, wr_hi, wr_mid, b_r, bb, tm):
    b, t, _ = x.shape
    nb, nt = b // bb, t // tm
    rows = bb * tm
    const2 = lambda shape: pl.BlockSpec(shape, lambda i, j: (0, 0))
    per_b = pl.BlockSpec((bb, 1, D_MODEL), lambda i, j: (i, 0, 0))
    seq = lambda last: pl.BlockSpec((bb, tm, last), lambda i, j: (i, j, 0))
    return pl.pallas_call(
        _outproj_kernel,
        grid=(nb, nt),
        in_specs=[seq(W_ATT), seq(W_REC), seq(D_MODEL), per_b, per_b, per_b, const2((1, D_MODEL)),
                  const2((1, W_ATT)), const2((W_ATT, D_MODEL)), const2((W_REC, D_MODEL)),
                  const2((D_MODEL, ROUTE_COLS)), const2((D_MODEL, ROUTE_COLS)), const2((1, ROUTE_COLS))],
        out_specs=[seq(D_MODEL),
                   pl.BlockSpec((N_PIECES, bb, tm, PIECE_COLS), lambda i, j: (0, i, j, 0)),
                   pl.BlockSpec((1, 1, SUBLANES, rows), lambda i, j: (i, j, 0, 0))],
        out_shape=[jax.ShapeDtypeStruct((b, t, D_MODEL), F32),
                   jax.ShapeDtypeStruct((N_PIECES, b, t, PIECE_COLS), U32),
                   jax.ShapeDtypeStruct((nb, nt, SUBLANES, rows), F32)],
        compiler_params=pltpu.CompilerParams(dimension_semantics=("arbitrary", "arbitrary"),
                                             vmem_limit_bytes=VMEM_LIMIT),
        name="outproj",
    )(attn, rec, x, ga1, g2, sh2, n2, g_att, wo_a, wo_r, wr_hi, wr_mid, b_r)


_SC_AXES = ("core", "subcore")


def _sc_mesh():
    return plsc.VectorSubcoreMesh(core_axis_name=_SC_AXES[0], subcore_axis_name=_SC_AXES[1])


def _sc_scatter_rows(src_hbm, idx_hbm, n, dst_hbm):
    def body(x_vmem, i_vmem):
        pltpu.sync_copy(x_vmem, dst_hbm.at[i_vmem.at[0]])

    pltpu.emit_pipeline(
        body,
        grid=(n // GATHER_WINDOW,),
        in_specs=[pl.BlockSpec((GATHER_WINDOW, PIECE_COLS), lambda i: (i, 0)),
                  pl.BlockSpec((1, GATHER_WINDOW), lambda i: (0, i))],
        out_specs=[],
        core_axis_name=_SC_AXES,
        dimension_semantics=(pltpu.PARALLEL,),
    )(src_hbm, idx_hbm)


def _sc_gather_rows(tab_hbm, idx_hbm, n, dst_hbm):
    def body(i_vmem, o_vmem):
        pltpu.sync_copy(tab_hbm.at[i_vmem.at[0]], o_vmem)

    pltpu.emit_pipeline(
        body,
        grid=(n // GATHER_WINDOW,),
        in_specs=[pl.BlockSpec((1, GATHER_WINDOW), lambda i: (0, i))],
        out_specs=[pl.BlockSpec((GATHER_WINDOW, PIECE_COLS), lambda i: (i, 0))],
        core_axis_name=_SC_AXES,
        dimension_semantics=(pltpu.PARALLEL,),
    )(idx_hbm, dst_hbm)


def _dispatch(tables, dests, dest_pad, n_rows):
    n_pad = dest_pad.shape[1]
    sizes = [t.shape[1] for t in tables]
    n_groups = len(tables)

    @pl.kernel(out_type=jax.ShapeDtypeStruct((N_PIECES, n_rows, PIECE_COLS), U32), mesh=_sc_mesh(), scratch_types=[])
    def scatter(*refs):
        tabs = refs[:n_groups]
        idxs = refs[n_groups:3 * n_groups]
        pad_hbm, o_hbm = refs[3 * n_groups], refs[3 * n_groups + 1]
        for piece in range(N_PIECES):
            dst = o_hbm.at[piece]
            for g in range(n_groups):
                for k in range(2):
                    _sc_scatter_rows(tabs[g].at[piece], idxs[2 * g + k], sizes[g], dst)
            _sc_scatter_rows(tabs[0].at[piece], pad_hbm, n_pad, dst)

    flat_idx = [d for pair in dests for d in pair]
    return scatter(*tables, *flat_idx, dest_pad)


def _collect(yo, dests):
    sizes = [pair[0].shape[1] for pair in dests]
    n_groups = len(dests)
    out_type = [jax.ShapeDtypeStruct((2, N_PIECES, n, PIECE_COLS), U32) for n in sizes]

    @pl.kernel(out_type=out_type, mesh=_sc_mesh(), scratch_types=[])
    def gather(*refs):
        yo_hbm = refs[0]
        idxs = refs[1:1 + 2 * n_groups]
        outs = refs[1 + 2 * n_groups:]
        for piece in range(N_PIECES):
            for g in range(n_groups):
                for k in range(2):
                    _sc_gather_rows(yo_hbm.at[piece], idxs[2 * g + k], sizes[g], outs[g].at[k, piece])

    flat_idx = [d for pair in dests for d in pair]
    return gather(yo, *flat_idx)


def _expert_kernel(blk_e_ref, blk_first_ref, n_used_ref, xs_ref, wg_ref, wu_ref, wd_ref, yo_ref, wg_b, wu_b, wd_b):
    b = pl.program_id(0)

    @pl.when(blk_first_ref[b] == 1)
    def _():
        wg_b[...] = wg_ref[0].astype(BF16)
        wu_b[...] = wu_ref[0].astype(BF16)
        wd_b[...] = wd_ref[0].astype(BF16)

    @pl.when(b < n_used_ref[0])
    def _():
        chunks = [c.astype(BF16) for c in _unpack_pieces([xs_ref[i] for i in range(N_PIECES)])]
        g = None
        u = None
        for i, xc in enumerate(chunks):
            rows_i = slice(i * PIECE_COLS, (i + 1) * PIECE_COLS)
            gi = _dot(xc, wg_b[rows_i, :])
            ui = _dot(xc, wu_b[rows_i, :])
            g = gi if g is None else g + gi
            u = ui if u is None else u + ui
        h = (g * _sigmoid(g)) * u
        for i, piece in enumerate(_pack_pieces(_dot(h.astype(BF16), wd_b[...]))):
            yo_ref[i] = piece

    @pl.when(b >= n_used_ref[0])
    def _():
        yo_ref[...] = jnp.zeros(yo_ref.shape, U32)


def _experts(xs, blk_e, blk_first, n_used, w_gate, w_up, w_down):
    p = xs.shape[1]
    tb = EXPERT_BLOCK
    nblk = p // tb
    row_block = pl.BlockSpec((N_PIECES, tb, PIECE_COLS), lambda i, e, f, n: (0, i, 0))
    grid_spec = pltpu.PrefetchScalarGridSpec(
        num_scalar_prefetch=3,
        grid=(nblk,),
        in_specs=[row_block,
                  pl.BlockSpec((1, D_MODEL, D_EXPERT), lambda i, e, f, n: (e[i], 0, 0)),
                  pl.BlockSpec((1, D_MODEL, D_EXPERT), lambda i, e, f, n: (e[i], 0, 0)),
                  pl.BlockSpec((1, D_EXPERT, D_MODEL), lambda i, e, f, n: (e[i], 0, 0))],
        out_specs=row_block,
        scratch_shapes=[pltpu.VMEM((D_MODEL, D_EXPERT), BF16),
                        pltpu.VMEM((D_MODEL, D_EXPERT), BF16),
                        pltpu.VMEM((D_EXPERT, D_MODEL), BF16)],
    )
    return pl.pallas_call(
        _expert_kernel,
        grid_spec=grid_spec,
        out_shape=jax.ShapeDtypeStruct((N_PIECES, p, PIECE_COLS), U32),
        compiler_params=pltpu.CompilerParams(dimension_semantics=("arbitrary",), vmem_limit_bytes=VMEM_LIMIT),
        name="experts",
    )(blk_e, blk_first, n_used, xs, w_gate, w_up, w_down)


def _combine_kernel(x1_ref, yg_ref, route_ref, ga2_ref, fg_ref, o_ref):
    bb, tm, _ = x1_ref.shape
    rows = bb * tm
    route = jnp.concatenate([route_ref[...], jnp.zeros((LANES - SUBLANES, rows), F32)], axis=0)
    w = route.T
    y = None
    for k in range(2):
        chunks = _unpack_pieces([yg_ref[k, i] for i in range(N_PIECES)])
        yk = w[:, 2 + k:3 + k] * jnp.concatenate(chunks, axis=1)
        y = yk if y is None else y + yk
    out = x1_ref[...] + ga2_ref[...] * y.reshape(bb, tm, D_MODEL)
    o_ref[...] = _rms(out) * fg_ref[...]


def _combine(x1, yg, route, ga2, final_g, bb, tm):
    b, t, _ = x1.shape
    nb, nt = b // bb, t // tm
    rows = bb * tm
    return pl.pallas_call(
        _combine_kernel,
        grid=(nb, nt),
        in_specs=[pl.BlockSpec((bb, tm, D_MODEL), lambda i, j: (i, j, 0)),
                  pl.BlockSpec((2, N_PIECES, rows, PIECE_COLS), lambda i, j: (0, 0, i * nt + j, 0)),
                  pl.BlockSpec((SUBLANES, rows), lambda i, j: (0, i * nt + j)),
                  pl.BlockSpec((bb, 1, D_MODEL), lambda i, j: (i, 0, 0)),
                  pl.BlockSpec((1, D_MODEL), lambda i, j: (0, 0))],
        out_specs=pl.BlockSpec((bb, tm, D_MODEL), lambda i, j: (i, j, 0)),
        out_shape=jax.ShapeDtypeStruct((b, t, D_MODEL), F32),
        compiler_params=pltpu.CompilerParams(dimension_semantics=("arbitrary", "arbitrary"),
                                             vmem_limit_bytes=VMEM_LIMIT),
        name="combine",
    )(x1, yg, route, ga2, final_g)


def _block_diag(w):
    n, k, _ = w.shape
    eye = jnp.eye(n, dtype=w.dtype)
    return (eye[:, None, :, None] * w[:, :, None, :]).reshape(n * k, n * k)


def _tiles(f, tile):
    b, _, length = f.shape
    return f.reshape(b, N_HEADS // 2, 2, length // tile, tile).transpose(0, 1, 3, 2, 4)


def _untile_rows(a, b, t):
    nb, nt, r, rows = a.shape
    bb = b // nb
    tm = t // nt
    return a.reshape(nb, nt, r, bb, tm).transpose(2, 0, 3, 1, 4).reshape(r, b, t)


def _group_front(x, c_mod, layer, conv_past, h0, cache):
    b, t, _ = x.shape
    sh1, sc1, ga1, sh2, sc2, ga2 = [m.reshape(b, 1, D_MODEL) for m in jnp.split(c_mod, 6, axis=-1)]
    q, k, v, lf_steps, rec, h_t, ctail = _inproj(
        x, 1.0 + sc1, sh1, layer["n1"], layer["w_all"], layer["b_f"], layer["conv_w"], layer["conv_b"],
        layer["wa_bd"], layer["b_a"], layer["wx_bd"], layer["b_x"], layer["lam"], layer["g_rec"], conv_past, h0)
    lf_t = _untile_rows(lf_steps, b, t).transpose(1, 0, 2)
    k_leaf = k.reshape(1, b, t, N_HEADS, HEAD_DIM)
    v_leaf = v.reshape(1, b, t, N_HEADS, HEAD_DIM)
    if cache is None:
        attn = _attention(q, k, v, _tiles(_cumsum_time(lf_t), min(t, ATTN_CHUNK)))
        attn, k_leaf, v_leaf = lax.optimization_barrier((attn, k_leaf, v_leaf))
    else:
        k_t, v_t, lf_past_t = cache
        total = k_t.shape[-1] + t
        padded = -(-total // CUMSUM_CHUNK) * CUMSUM_CHUNK
        lf_all = jnp.concatenate([lf_past_t, lf_t, jnp.zeros((b, N_HEADS, padded - total), F32)], axis=2)
        attn = _attention_cached(q, k, v, _cumsum_time(lf_all), k_t, v_t)
    bb, tm = (1, ROW_TILE) if t >= ROW_TILE else (ROW_TILE // t, t)
    x1, u2p, route = _outproj(attn, rec, x, ga1, 1.0 + sc2, sh2, layer["n2"], layer["g_att"],
                              layer["wo_a"], layer["wo_r"], layer["wr_hi"], layer["wr_mid"], layer["b_r"], bb, tm)
    route = _untile_rows(route, b, t).reshape(SUBLANES, b * t)
    leaves = (k_leaf, v_leaf, lf_t.transpose(0, 2, 1)[None], h_t[None], ctail[None, :, SUBLANES - (CONV_W - 1):, :])
    combine_tile = (1, COMBINE_TILE) if t >= COMBINE_TILE else (bb, tm)
    return x1, u2p.reshape(N_PIECES, b * t, PIECE_COLS), route, ga2, combine_tile, leaves


def kernel(x_prompt, x_sample, c_prompt, c_sample, cache_k, cache_v, cache_logf, state_h, state_conv, norm1_g, norm2_g, w_ada, b_ada, w_in, b_f, conv_w, conv_b, w_rg_a, b_rg_a, w_rg_x, b_rg_x, rg_lambda, out_g_att, out_g_rec, w_out, w_route_group, b_route_group, w_route_expert, b_route_expert, w_exp_gate, w_exp_up, w_exp_down, final_g):
    bp, tp, _ = x_prompt.shape
    bs, ts, _ = x_sample.shape
    l = 0
    o3 = 3 * W_ATT
    o4 = o3 + N_HEADS
    w_in_l = w_in[l]
    w_all = jnp.concatenate([w_in_l[:, :o3], w_in_l[:, o4:], w_in_l[:, o3:o4],
                             jnp.zeros((D_MODEL, LANES - N_HEADS), F32)], axis=1).astype(BF16)
    w_r = jnp.zeros((D_MODEL, ROUTE_COLS), F32)
    w_r = w_r.at[:, 0:N_GROUPS].set(w_route_group[l]).at[:, EXPERT_COL0:EXPERT_COL0 + N_EXPERTS].set(w_route_expert[l])
    wr_hi = w_r.astype(BF16)
    wr_mid = (w_r - wr_hi.astype(F32)).astype(BF16)
    b_r = jnp.zeros((1, ROUTE_COLS), F32)
    b_r = b_r.at[0, 0:N_GROUPS].set(b_route_group[l]).at[0, N_GROUPS:SUBLANES].set(NEG_BIG)
    b_r = b_r.at[0, EXPERT_COL0:EXPERT_COL0 + N_EXPERTS].set(b_route_expert[l])
    layer = {
        "n1": norm1_g[l].reshape(1, D_MODEL), "n2": norm2_g[l].reshape(1, D_MODEL),
        "w_all": w_all,
        "b_f": jnp.concatenate([b_f[l], jnp.zeros((LANES - N_HEADS,), F32)]).reshape(1, LANES),
        "conv_w": conv_w[l], "conv_b": conv_b[l].reshape(1, W_REC),
        "wa_bd": _block_diag(w_rg_a[l]).astype(BF16), "b_a": b_rg_a[l].reshape(1, W_REC),
        "wx_bd": _block_diag(w_rg_x[l]).astype(BF16), "b_x": b_rg_x[l].reshape(1, W_REC),
        "lam": rg_lambda[l].reshape(1, W_REC), "g_rec": out_g_rec[l].reshape(1, W_REC),
        "g_att": out_g_att[l].reshape(1, W_ATT),
        "wo_a": w_out[l, :W_ATT].astype(BF16), "wo_r": w_out[l, W_ATT:].astype(BF16),
        "wr_hi": wr_hi, "wr_mid": wr_mid, "b_r": b_r,
    }

    ada = _ada(jnp.concatenate([c_prompt, c_sample], axis=0), w_ada[l], b_ada[l])
    pad_rows = SUBLANES - (CONV_W - 1)
    x1p, u2p_p, route_p, ga2p, tile_p, leaves_p = _group_front(
        x_prompt, ada[:bp], layer, jnp.zeros((bp, SUBLANES, W_REC), F32), jnp.zeros((bp, W_REC), F32), None)
    conv_past_s = jnp.concatenate([jnp.zeros((bs, pad_rows, W_REC), F32), state_conv[l]], axis=1)
    x1s, u2p_s, route_s, ga2s, tile_s, leaves_s = _group_front(
        x_sample, ada[bp:], layer, conv_past_s, state_h[l],
        (cache_k[l].transpose(0, 2, 3, 1), cache_v[l].transpose(0, 2, 3, 1), cache_logf[l].transpose(0, 2, 1)))

    n_p, n_s = bp * tp, bs * ts
    n_tok = n_p + n_s
    route = jnp.concatenate([route_p, route_s], axis=1)
    eid = route[0:2].astype(jnp.int32).reshape(-1)
    n_slot = 2 * n_tok
    tb = EXPERT_BLOCK
    experts = jnp.arange(N_EXPERTS, dtype=jnp.int32)
    onehot = (eid[:, None] == experts[None, :]).astype(jnp.int32)
    csum = jnp.cumsum(onehot, axis=0)
    counts = csum[-1]
    rank = jnp.sum(csum * onehot, axis=1) - 1
    padded = ((counts + tb - 1) // tb) * tb
    pend = jnp.cumsum(padded)
    pstart = pend - padded
    dest = jnp.sum(onehot * pstart[None, :], axis=1) + rank
    nblk = -(-n_slot // tb) + N_EXPERTS
    n_rows = nblk * tb
    blk_row0 = jnp.arange(nblk, dtype=jnp.int32) * tb
    blk_e = jnp.minimum(jnp.sum((pend[None, :] <= blk_row0[:, None]).astype(jnp.int32), axis=1), N_EXPERTS - 1)
    blk_first = jnp.concatenate([jnp.ones((1,), jnp.int32), (blk_e[1:] != blk_e[:-1]).astype(jnp.int32)])
    n_used = (pend[-1:] // tb).astype(jnp.int32)
    pad_e = padded - counts
    cpad = jnp.cumsum(pad_e)
    j = jnp.arange(n_rows - n_slot, dtype=jnp.int32)
    owner = (cpad[None, :] <= j[:, None]).astype(jnp.int32)
    e_j = jnp.minimum(jnp.sum(owner, axis=1), N_EXPERTS - 1)
    sel = (e_j[:, None] == experts[None, :]).astype(jnp.int32)
    in_expert = jnp.sum(sel * (pstart + counts - (cpad - pad_e))[None, :], axis=1) + j
    dest_pad = jnp.where(j < cpad[-1], in_expert, pend[-1] + (j - cpad[-1])).reshape(1, -1)

    dest = dest.reshape(2, n_tok)
    dests = [[dest[k, :n_p].reshape(1, n_p) for k in range(2)], [dest[k, n_p:].reshape(1, n_s) for k in range(2)]]
    xs = _dispatch([u2p_p, u2p_s], dests, dest_pad, n_rows)
    yo = _experts(xs, blk_e, blk_first, n_used, w_exp_gate[l], w_exp_up[l], w_exp_down[l])
    yg_p, yg_s = _collect(yo, dests)

    fg = final_g.reshape(1, D_MODEL)
    y_prompt = _combine(x1p, yg_p, route_p, ga2p, fg, *tile_p)
    y_sample = _combine(x1s, yg_s, route_s, ga2s, fg, *tile_s)
    return (y_prompt, y_sample) + leaves_p + leaves_s
```

```python
import functools
import math

import jax
import jax.numpy as jnp
from jax import lax
from jax.experimental import pallas as pl
from jax.experimental.pallas import tpu as pltpu
from jax.experimental.pallas import tpu_sc as plsc

F32 = jnp.float32
BF16 = jnp.bfloat16
U32 = jnp.uint32

D_MODEL = 1024
N_HEADS = 8
HEAD_DIM = 64
W_ATT = N_HEADS * HEAD_DIM
W_REC = D_MODEL - W_ATT
N_REC_BLOCKS = 8
REC_BLOCK = W_REC // N_REC_BLOCKS
CONV_W = 4
RG_C = 8.0
N_GROUPS = 4
EXPERTS_PER_GROUP = 8
N_EXPERTS = N_GROUPS * EXPERTS_PER_GROUP
D_EXPERT = 256
EPS = 1e-6

LANES = 128
SUBLANES = 8
LOG2E = math.log2(math.e)
Q_SCALE = LOG2E * HEAD_DIM ** -0.5
NEG_BIG = -1e30
VMEM_LIMIT = 48 * 1024 * 1024

SEQ_ROWS = 64
BATCH_ROWS = 8
CUMSUM_CHUNK = 512
CUMSUM_BATCH = 8
ATTN_TILE = 2048
ATTN_CHUNK = 512
ATTN_UNROLL = 2
ROW_TILE = 512
EXPERT_BLOCK = 512
COMBINE_TILE = 1024
GATHER_WINDOW = 128
PIECE_COLS = 256
N_PIECES = D_MODEL // (2 * PIECE_COLS)
ROUTE_COLS = 128
EXPERT_COL0 = 8


def _split3(x):
    hi = x.astype(BF16)
    r1 = x - hi.astype(F32)
    mid = r1.astype(BF16)
    lo = (r1 - mid.astype(F32)).astype(BF16)
    return hi, mid, lo


def _dot(a, b):
    return jnp.dot(a, b, preferred_element_type=F32)


def _rms(x):
    return x * lax.rsqrt(jnp.mean(x * x, axis=-1, keepdims=True) + EPS)


def _sigmoid(x):
    return 0.5 * jnp.tanh(0.5 * x) + 0.5


def _pack_bf16_pairs(y):
    n = y.shape[-1] // 2
    yb = y.astype(BF16).astype(F32)
    lo = pltpu.bitcast(yb[:, :n], U32)
    hi = pltpu.bitcast(yb[:, n:], U32)
    return (lo >> 16) | hi


def _unpack_bf16_pairs(p):
    lo = pltpu.bitcast(p << 16, F32)
    hi = pltpu.bitcast(p & jnp.uint32(0xFFFF0000), F32)
    return lo, hi


def _pack_pieces(y):
    w = 2 * PIECE_COLS
    return [_pack_bf16_pairs(y[:, i * w:(i + 1) * w]) for i in range(N_PIECES)]


def _unpack_pieces(pieces):
    chunks = []
    for p in pieces:
        chunks.extend(_unpack_bf16_pairs(p))
    return chunks


def _ada_kernel(c_ref, w_ref, b_ref, o_ref):
    c = c_ref[...]
    a = c * jax.nn.sigmoid(c)
    a_hi, a_mid, _ = _split3(a)
    w_hi, w_mid, _ = _split3(w_ref[...])
    o_ref[...] = _dot(a_hi, w_hi) + _dot(a_mid, w_hi) + _dot(a_hi, w_mid) + b_ref[...]


def _ada(c, w_ada, b_ada):
    rows = c.shape[0]
    n = w_ada.shape[1]
    tn = 1536
    return pl.pallas_call(
        _ada_kernel,
        grid=(n // tn,),
        in_specs=[pl.BlockSpec((rows, D_MODEL), lambda j: (0, 0)),
                  pl.BlockSpec((D_MODEL, tn), lambda j: (0, j)),
                  pl.BlockSpec((1, tn), lambda j: (0, j))],
        out_specs=pl.BlockSpec((rows, tn), lambda j: (0, j)),
        out_shape=jax.ShapeDtypeStruct((rows, n), F32),
        compiler_params=pltpu.CompilerParams(vmem_limit_bytes=VMEM_LIMIT),
        name="ada",
    )(c, w_ada, b_ada.reshape(1, n))


def _inproj_kernel(x_ref, g1_ref, sh1_ref, n1_ref, w_ref, bf_ref, cw_ref, cb_ref, wa_ref, ba_ref, wx_ref, bx_ref,
                   lam_ref, grec_ref, cpast_ref, h0_ref,
                   q_ref, k_ref, v_ref, lf_ref, rec_ref, ht_ref, ctail_ref,
                   xbuf, ub_scr, z_scr, xc_scr, pr_scr, pi_scr, a_scr, u_scr, hs_scr, gl_scr, h_scr):
    bb, tm, _ = x_ref.shape
    rows = bb * tm
    n_lane_tiles = W_REC // LANES
    half = W_REC // 2
    o_x = 3 * W_ATT

    @pl.when(pl.program_id(1) == 0)
    def _():
        xbuf[:, 0:SUBLANES, :] = cpast_ref[...]
        h_scr[...] = h0_ref[...]

    mod = n1_ref[...] * g1_ref[...]
    for b in range(bb):
        rb = slice(b * tm, (b + 1) * tm)
        ub_scr[rb, :] = (_rms(x_ref[b]) * mod[b] + sh1_ref[b]).astype(BF16)

    z_scr[...] = _dot(ub_scr[...], w_ref[:, o_x:])

    cw = cw_ref[...]
    for b in range(bb):
        rb = slice(b * tm, (b + 1) * tm)
        xb = z_scr[rb, 0:W_REC]
        xbuf[b, SUBLANES:SUBLANES + tm, :] = xb
        xc = cb_ref[...] + cw[3:4, :] * xb
        for j in range(CONV_W - 1):
            off = SUBLANES - (CONV_W - 1) + j
            xc = xc + cw[j:j + 1, :] * xbuf[b, off:off + tm, :]
        tail = xbuf[b, tm:tm + SUBLANES, :]
        ctail_ref[b] = tail
        xbuf[b, 0:SUBLANES, :] = tail
        xc_scr[rb, :] = xc

    xcb = xc_scr[...].astype(BF16)
    for w_gate, pre in ((wa_ref, pr_scr), (wx_ref, pi_scr)):
        pre[:, :half] = _dot(xcb[:, :half], w_gate[:half, :half])
        pre[:, half:] = _dot(xcb[:, half:], w_gate[half:, half:])

    zq = _dot(ub_scr[...], w_ref[:, :o_x])
    q_ref[...] = (zq[:, 0:W_ATT] * Q_SCALE).reshape(bb, tm, W_ATT).astype(BF16)
    k_ref[...] = zq[:, W_ATT:2 * W_ATT].reshape(bb, tm, W_ATT)
    v_ref[...] = zq[:, 2 * W_ATT:3 * W_ATT].reshape(bb, tm, W_ATT)

    zf = z_scr[:, 2 * W_REC:2 * W_REC + LANES] + bf_ref[...]
    lf = jnp.minimum(zf, 0.0) - jnp.log1p(jnp.exp(-jnp.abs(zf)))
    lf_ref[0, 0] = lf.T[0:N_HEADS, :]

    nlam = -lam_ref[...]
    decay = -RG_C * (jnp.maximum(nlam, 0.0) + jnp.log1p(jnp.exp(-jnp.abs(nlam))))
    for b in range(bb):
        rb = slice(b * tm, (b + 1) * tm)
        r = _sigmoid(pr_scr[rb, :] + ba_ref[...])
        i = _sigmoid(pi_scr[rb, :] + bx_ref[...])
        log_a = decay * r
        t = jnp.tanh(log_a)
        neg_expm1 = -2.0 * t / (1.0 - t)
        root = jnp.where(neg_expm1 > 0.0, neg_expm1 * lax.rsqrt(neg_expm1), 0.0)
        pr_scr[rb, :] = jnp.exp(log_a)
        pi_scr[rb, :] = root * (i * xc_scr[rb, :])
        gb = z_scr[rb, W_REC:2 * W_REC]
        gl_scr[rb, :] = 0.5 * gb * (1.0 + jnp.tanh(math.sqrt(2.0 / math.pi) * (gb + 0.044715 * (gb * gb * gb))))

    def lane_tile(ref, c):
        return ref[:, c * LANES:(c + 1) * LANES].reshape(bb, tm, LANES)

    for c in range(n_lane_tiles):
        a_scr[c] = jnp.swapaxes(lane_tile(pr_scr, c), 0, 1)
        u_scr[c] = jnp.swapaxes(lane_tile(pi_scr, c), 0, 1)
    for c in range(n_lane_tiles):
        h = h_scr[:, c * LANES:(c + 1) * LANES]
        for step in range(tm):
            h = a_scr[c, step] * h + u_scr[c, step]
            hs_scr[c, step] = h
        h_scr[:, c * LANES:(c + 1) * LANES] = h
    ht_ref[...] = h_scr[...]
    for c in range(n_lane_tiles):
        pr_scr[:, c * LANES:(c + 1) * LANES] = jnp.swapaxes(hs_scr[c], 0, 1).reshape(rows, LANES)

    for b in range(bb):
        rb = slice(b * tm, (b + 1) * tm)
        y = gl_scr[rb, :] * pr_scr[rb, :]
        rec_ref[b] = (_rms(y) * grec_ref[...]).astype(BF16)


def _inproj(x, g1, sh1, n1, w_all, b_f, conv_w, conv_b, wa_bd, b_a, wx_bd, b_x, lam, g_rec, conv_past, h0):
    b, t, _ = x.shape
    bb, tm = BATCH_ROWS, SEQ_ROWS
    nb, nt = b // bb, t // tm
    rows = bb * tm
    wcols = w_all.shape[1]
    const2 = lambda shape: pl.BlockSpec(shape, lambda i, j: (0, 0))
    per_b = lambda last: pl.BlockSpec((bb, 1, last), lambda i, j: (i, 0, 0))
    seq = lambda last: pl.BlockSpec((bb, tm, last), lambda i, j: (i, j, 0))
    return pl.pallas_call(
        _inproj_kernel,
        grid=(nb, nt),
        in_specs=[seq(D_MODEL), per_b(D_MODEL), per_b(D_MODEL), const2((1, D_MODEL)),
                  const2((D_MODEL, wcols)), const2((1, LANES)),
                  const2((CONV_W, W_REC)), const2((1, W_REC)),
                  const2((W_REC, W_REC)), const2((1, W_REC)), const2((W_REC, W_REC)), const2((1, W_REC)),
                  const2((1, W_REC)), const2((1, W_REC)),
                  pl.BlockSpec((bb, SUBLANES, W_REC), lambda i, j: (i, 0, 0)),
                  pl.BlockSpec((bb, W_REC), lambda i, j: (i, 0))],
        out_specs=[seq(W_ATT), seq(W_ATT), seq(W_ATT),
                   pl.BlockSpec((1, 1, N_HEADS, rows), lambda i, j: (i, j, 0, 0)),
                   seq(W_REC),
                   pl.BlockSpec((bb, W_REC), lambda i, j: (i, 0)),
                   pl.BlockSpec((bb, SUBLANES, W_REC), lambda i, j: (i, 0, 0))],
        out_shape=[jax.ShapeDtypeStruct((b, t, W_ATT), BF16),
                   jax.ShapeDtypeStruct((b, t, W_ATT), F32),
                   jax.ShapeDtypeStruct((b, t, W_ATT), F32),
                   jax.ShapeDtypeStruct((nb, nt, N_HEADS, rows), F32),
                   jax.ShapeDtypeStruct((b, t, W_REC), BF16),
                   jax.ShapeDtypeStruct((b, W_REC), F32),
                   jax.ShapeDtypeStruct((b, SUBLANES, W_REC), F32)],
        scratch_shapes=[pltpu.VMEM((bb, tm + SUBLANES, W_REC), F32),
                        pltpu.VMEM((rows, D_MODEL), BF16),
                        pltpu.VMEM((rows, 2 * W_REC + LANES), F32),
                        pltpu.VMEM((rows, W_REC), F32),
                        pltpu.VMEM((rows, W_REC), F32),
                        pltpu.VMEM((rows, W_REC), F32),
                        pltpu.VMEM((W_REC // LANES, tm, bb, LANES), F32),
                        pltpu.VMEM((W_REC // LANES, tm, bb, LANES), F32),
                        pltpu.VMEM((W_REC // LANES, tm, bb, LANES), F32),
                        pltpu.VMEM((rows, W_REC), F32),
                        pltpu.VMEM((bb, W_REC), F32)],
        compiler_params=pltpu.CompilerParams(dimension_semantics=("arbitrary", "arbitrary"),
                                             vmem_limit_bytes=VMEM_LIMIT),
        name="inproj",
    )(x, g1, sh1, n1, w_all, b_f, conv_w, conv_b, wa_bd, b_a, wx_bd, b_x, lam, g_rec, conv_past, h0)


def _cumsum_kernel(x_ref, o_ref):
    cb, heads, length = x_ref.shape
    n = cb * heads
    c = CUMSUM_CHUNK
    upper = (lax.broadcasted_iota(jnp.int32, (c, c), 0) <= lax.broadcasted_iota(jnp.int32, (c, c), 1)).astype(BF16)
    carry = jnp.zeros((n, 1), F32)
    for j in range(length // c):
        x = x_ref[:, :, j * c:(j + 1) * c].reshape(n, c)
        hi, mid, lo = _split3(x)
        parts = jnp.concatenate([hi.astype(F32), mid.astype(F32), lo.astype(F32)], axis=0)
        sums = _dot(parts.astype(BF16), upper)
        out = sums[0:n] + sums[n:2 * n] + sums[2 * n:3 * n] + carry
        o_ref[:, :, j * c:(j + 1) * c] = out.reshape(cb, heads, c)
        carry = out[:, c - 1:c]


def _cumsum_time(lf_t):
    b, h, length = lf_t.shape
    cb = min(b, CUMSUM_BATCH)
    return pl.pallas_call(
        _cumsum_kernel,
        grid=(b // cb,),
        in_specs=[pl.BlockSpec((cb, h, length), lambda i: (i, 0, 0))],
        out_specs=pl.BlockSpec((cb, h, length), lambda i: (i, 0, 0)),
        out_shape=jax.ShapeDtypeStruct((b, h, length), F32),
        name="cumsum",
    )(lf_t)


def _attn_tile(qh, kt, vt, bias, m_ref, acc_ref, mask):
    s = lax.dot_general(qh, kt, (((1,), (1,)), ((), ())), preferred_element_type=F32) + bias
    if mask is not None:
        s = jnp.where(mask, s, NEG_BIG)
    width = min(LANES, s.shape[1])
    slabs = [s[:, c * width:(c + 1) * width] for c in range(s.shape[1] // width)]
    part = slabs[0]
    for sl in slabs[1:]:
        part = jnp.maximum(part, sl)
    m = m_ref[...]
    m_new = jnp.maximum(m, jnp.broadcast_to(jnp.max(part, axis=1, keepdims=True), m.shape))
    p = jnp.concatenate([jnp.exp2(sl - m_new[:, :width]) for sl in slabs], axis=1)
    acc_ref[...] = jnp.exp2(m - m_new) * acc_ref[...] + _dot(p.astype(BF16), vt)
    m_ref[...] = m_new


def _attn_kernel(q_ref, k_ref, v_ref, f_ref, o_ref, kb, vb, m_scr, acc_scr, *, tq, ck):
    qi = pl.program_id(2)
    chunks_per_tile = tq // ck

    @pl.when(qi == 0)
    def _():
        v_f32 = v_ref[0]
        first = lax.broadcasted_iota(jnp.int32, v_f32.shape, 1) < HEAD_DIM
        kb[...] = k_ref[0].astype(BF16)
        vb[0] = jnp.where(first, v_f32, 1.0).astype(BF16)
        vb[1] = jnp.where(first, 1.0, v_f32).astype(BF16)

    q = q_ref[0]
    first_head = lax.broadcasted_iota(jnp.int32, q.shape, 1) < HEAD_DIM
    qh = (jnp.where(first_head, q, jnp.zeros_like(q)), jnp.where(first_head, jnp.zeros_like(q), q))

    ref_f = f_ref[0, 0, qi * chunks_per_tile][:, 0:1]
    m_scr[...] = jnp.full(m_scr.shape, NEG_BIG, F32)
    acc_scr[...] = jnp.zeros(acc_scr.shape, F32)

    def chunk(j, row0=0, masked=False):
        start = pl.multiple_of(j * ck, ck)
        kt = kb[pl.ds(start, ck), :]
        f_tile = f_ref[0, 0, j]
        mask = None
        if masked:
            shape = (tq - row0, ck)
            mask = lax.broadcasted_iota(jnp.int32, shape, 1) <= lax.broadcasted_iota(jnp.int32, shape, 0)
        for hd in range(2):
            bias = (ref_f[hd:hd + 1, :] - f_tile[hd:hd + 1, :]) * LOG2E
            _attn_tile(qh[hd][row0:, :], kt, vb[hd, pl.ds(start, ck), :], bias,
                       m_scr.at[hd, row0:tq], acc_scr.at[hd, row0:tq], mask)

    n_chunks = qi * chunks_per_tile

    def chunk_body(jj, carry):
        for u in range(ATTN_UNROLL):
            chunk(jj * ATTN_UNROLL + u)
        return carry

    lax.fori_loop(0, n_chunks // ATTN_UNROLL, chunk_body, 0)
    for r in range(ATTN_UNROLL - 1):
        @pl.when(r < n_chunks % ATTN_UNROLL)
        def _():
            chunk((n_chunks // ATTN_UNROLL) * ATTN_UNROLL + r)

    for c in range(chunks_per_tile):
        chunk(n_chunks + c, row0=c * ck, masked=True)

    outs = []
    for hd in range(2):
        acc = acc_scr[hd]
        outs.append(acc / pltpu.roll(acc, HEAD_DIM, axis=1))
    o_ref[0] = jnp.where(first_head, outs[0], outs[1]).astype(BF16)


def _attention(q, k, v, f_tiles):
    b, t, _ = q.shape
    tq = min(t, ATTN_TILE)
    ck = f_tiles.shape[-1]
    q_spec = pl.BlockSpec((1, tq, LANES), lambda i, p, j: (i, j, p))
    full = pl.BlockSpec((1, t, LANES), lambda i, p, j: (i, 0, p))
    return pl.pallas_call(
        functools.partial(_attn_kernel, tq=tq, ck=ck),
        grid=(b, N_HEADS // 2, t // tq),
        in_specs=[q_spec, full, full,
                  pl.BlockSpec((1, 1) + f_tiles.shape[2:], lambda i, p, j: (i, p, 0, 0, 0))],
        out_specs=q_spec,
        out_shape=jax.ShapeDtypeStruct((b, t, W_ATT), BF16),
        scratch_shapes=[pltpu.VMEM((t, LANES), BF16), pltpu.VMEM((2, t, LANES), BF16),
                        pltpu.VMEM((2, tq, LANES), F32), pltpu.VMEM((2, tq, LANES), F32)],
        compiler_params=pltpu.CompilerParams(dimension_semantics=("arbitrary", "arbitrary", "arbitrary"),
                                             vmem_limit_bytes=VMEM_LIMIT),
        name="attn",
    )(q, k, v, f_tiles)


def _attn_cached_kernel(q_ref, kn_ref, vn_ref, f_ref, kt_ref, vt_ref, o_ref, *, past):
    t = q_ref.shape[1]
    q = q_ref[0]
    kn = kn_ref[0].astype(BF16)
    vn = vn_ref[0].astype(BF16)
    f = f_ref[0]
    causal = lax.broadcasted_iota(jnp.int32, (t, t), 1) <= lax.broadcasted_iota(jnp.int32, (t, t), 0)
    last_dims = (((1,), (1,)), ((), ()))
    outs = []
    for h in range(N_HEADS):
        cols = slice(h * HEAD_DIM, (h + 1) * HEAD_DIM)
        qh = q[:, cols]
        ref_f = f[h:h + 1, past:past + 1]
        s_p = _dot(qh, kt_ref[0, h].astype(BF16)) + (ref_f - f[h:h + 1, 0:past]) * LOG2E
        s_n = lax.dot_general(qh, kn[:, cols], last_dims, preferred_element_type=F32)
        s_n = jnp.where(causal, s_n + (ref_f - f[h:h + 1, past:past + t]) * LOG2E, NEG_BIG)
        m = jnp.maximum(jnp.max(s_p, axis=1, keepdims=True), jnp.max(s_n, axis=1, keepdims=True))
        p_p = jnp.exp2(s_p - m)
        p_n = jnp.exp2(s_n - m)
        l = jnp.sum(p_p, axis=1, keepdims=True) + jnp.sum(p_n, axis=1, keepdims=True)
        o = lax.dot_general(p_p.astype(BF16), vt_ref[0, h].astype(BF16), last_dims, preferred_element_type=F32)
        o = o + _dot(p_n.astype(BF16), vn[:, cols])
        outs.append(o / l)
    o_ref[0] = jnp.concatenate(outs, axis=1).astype(BF16)


def _attention_cached(q, k_new, v_new, f_all, k_t, v_t):
    b, t, _ = q.shape
    past = k_t.shape[-1]
    rows = pl.BlockSpec((1, t, W_ATT), lambda i: (i, 0, 0))
    cache = pl.BlockSpec((1, N_HEADS, HEAD_DIM, past), lambda i: (i, 0, 0, 0))
    return pl.pallas_call(
        functools.partial(_attn_cached_kernel, past=past),
        grid=(b,),
        in_specs=[rows, rows, rows, pl.BlockSpec((1,) + f_all.shape[1:], lambda i: (i, 0, 0)), cache, cache],
        out_specs=rows,
        out_shape=jax.ShapeDtypeStruct((b, t, W_ATT), BF16),
        compiler_params=pltpu.CompilerParams(dimension_semantics=("arbitrary",), vmem_limit_bytes=VMEM_LIMIT),
        name="attn_cached",
    )(q, k_new, v_new, f_all, k_t, v_t)


def _outproj_kernel(attn_ref, rec_ref, x_ref, ga1_ref, g2_ref, sh2_ref, n2_ref, gatt_ref, woa_ref, wor_ref,
                    wr_ref, br_ref,
                    x1_ref, u2p_ref, route_ref, cnt_ref, before_scr, cnt_scr):
    bb, tm, _ = x_ref.shape
    rows = bb * tm

    @pl.when((pl.program_id(0) == 0) & (pl.program_id(1) == 0))
    def _():
        shape = (rows, rows)
        before_scr[...] = (lax.broadcasted_iota(jnp.int32, shape, 0)
                           < lax.broadcasted_iota(jnp.int32, shape, 1)).astype(BF16)
        cnt_scr[...] = jnp.zeros(cnt_scr.shape, F32)

    attn = attn_ref[...].astype(F32)
    an = (_rms(attn) * gatt_ref[...]).reshape(rows, W_ATT).astype(BF16)
    mix = _dot(an, woa_ref[...]) + _dot(rec_ref[...].reshape(rows, W_REC), wor_ref[...])
    x1 = x_ref[...] + ga1_ref[...] * mix.reshape(bb, tm, D_MODEL)
    x1_ref[...] = x1
    u2 = (_rms(x1) * (n2_ref[...] * g2_ref[...]) + sh2_ref[...]).reshape(rows, D_MODEL)
    for i, piece in enumerate(_pack_pieces(u2)):
        u2p_ref[i] = piece.reshape(bb, tm, PIECE_COLS)

    logits = _dot(u2.astype(BF16), wr_ref[...]) + br_ref[...]
    lt = logits.T
    row8 = lax.broadcasted_iota(jnp.int32, (SUBLANES, rows), 0).astype(F32)
    lg = lt[0:SUBLANES]
    m_g = jnp.max(lg, axis=0, keepdims=True)
    gidx = jnp.min(jnp.where(lg == m_g, row8, float(SUBLANES)), axis=0, keepdims=True)
    p_top = 1.0 / jnp.sum(jnp.exp(lg - m_g), axis=0, keepdims=True)
    leg = jnp.zeros((EXPERTS_PER_GROUP, rows), F32)
    for g in range(N_GROUPS):
        lo = EXPERT_COL0 + g * EXPERTS_PER_GROUP
        leg = jnp.where(gidx == float(g), lt[lo:lo + EXPERTS_PER_GROUP], leg)
    v1 = jnp.max(leg, axis=0, keepdims=True)
    i1 = jnp.min(jnp.where(leg == v1, row8, float(SUBLANES)), axis=0, keepdims=True)
    leg2 = jnp.where(row8 == i1, -jnp.inf, leg)
    v2 = jnp.max(leg2, axis=0, keepdims=True)
    i2 = jnp.min(jnp.where(leg2 == v2, row8, float(SUBLANES)), axis=0, keepdims=True)
    e21 = jnp.exp(v2 - v1)
    w1 = p_top / (1.0 + e21)
    w2 = w1 * e21
    base = gidx * float(EXPERTS_PER_GROUP)
    e0 = base + i1
    e1 = base + i2

    erow = lax.broadcasted_iota(jnp.int32, (N_EXPERTS, rows), 0).astype(F32)
    hit0 = erow == e0
    hit1 = erow == e1
    hits = jnp.concatenate([jnp.where(hit0, 1.0, 0.0), jnp.where(hit1, 1.0, 0.0)], axis=0)
    earlier = _dot(hits.astype(BF16), before_scr[...])
    seen = cnt_scr[:, 0:1] + earlier[0:N_EXPERTS] + earlier[N_EXPERTS:]
    rank0 = jnp.sum(jnp.where(hit0, seen, 0.0), axis=0, keepdims=True)
    rank1 = jnp.sum(jnp.where(hit1, seen, 0.0), axis=0, keepdims=True)
    step_counts = jnp.sum(hits[0:N_EXPERTS] + hits[N_EXPERTS:], axis=1, keepdims=True)
    cnt_scr[...] = cnt_scr[...] + step_counts
    cnt_ref[...] = cnt_scr[...]

    out = jnp.zeros((SUBLANES, rows), F32)
    for r_idx, val in enumerate((e0, e1, w1, w2, rank0, rank1)):
        out = jnp.where(row8 == float(r_idx), val, out)
    route_ref[0, 0] = out


def _outproj(attn, rec, x, ga1, g2, sh2, n2, g_att, wo_a, wo_r, w_r, b_r, bb, tm):
    b, t, _ = x.shape
    nb, nt = b // bb, t // tm
    rows = bb * tm
    const2 = lambda shape: pl.BlockSpec(shape, lambda i, j: (0, 0))
    per_b = pl.BlockSpec((bb, 1, D_MODEL), lambda i, j: (i, 0, 0))
    seq = lambda last: pl.BlockSpec((bb, tm, last), lambda i, j: (i, j, 0))
    return pl.pallas_call(
        _outproj_kernel,
        grid=(nb, nt),
        in_specs=[seq(W_ATT), seq(W_REC), seq(D_MODEL), per_b, per_b, per_b, const2((1, D_MODEL)),
                  const2((1, W_ATT)), const2((W_ATT, D_MODEL)), const2((W_REC, D_MODEL)),
                  const2((D_MODEL, ROUTE_COLS)), const2((1, ROUTE_COLS))],
        out_specs=[seq(D_MODEL),
                   pl.BlockSpec((N_PIECES, bb, tm, PIECE_COLS), lambda i, j: (0, i, j, 0)),
                   pl.BlockSpec((1, 1, SUBLANES, rows), lambda i, j: (i, j, 0, 0)),
                   const2((N_EXPERTS, LANES))],
        out_shape=[jax.ShapeDtypeStruct((b, t, D_MODEL), F32),
                   jax.ShapeDtypeStruct((N_PIECES, b, t, PIECE_COLS), U32),
                   jax.ShapeDtypeStruct((nb, nt, SUBLANES, rows), F32),
                   jax.ShapeDtypeStruct((N_EXPERTS, LANES), F32)],
        scratch_shapes=[pltpu.VMEM((rows, rows), BF16), pltpu.VMEM((N_EXPERTS, LANES), F32)],
        compiler_params=pltpu.CompilerParams(dimension_semantics=("arbitrary", "arbitrary"),
                                             vmem_limit_bytes=VMEM_LIMIT),
        name="outproj",
    )(attn, rec, x, ga1, g2, sh2, n2, g_att, wo_a, wo_r, w_r, b_r)


_SC_AXES = ("core", "subcore")


def _sc_mesh():
    return plsc.VectorSubcoreMesh(core_axis_name=_SC_AXES[0], subcore_axis_name=_SC_AXES[1])


def _sc_scatter_rows(src_hbm, idx_hbm, n, dst_hbm):
    def body(x_vmem, i_vmem):
        pltpu.sync_copy(x_vmem, dst_hbm.at[i_vmem.at[0]])

    pltpu.emit_pipeline(
        body,
        grid=(n // GATHER_WINDOW,),
        in_specs=[pl.BlockSpec((GATHER_WINDOW, PIECE_COLS), lambda i: (i, 0)),
                  pl.BlockSpec((1, GATHER_WINDOW), lambda i: (0, i))],
        out_specs=[],
        core_axis_name=_SC_AXES,
        dimension_semantics=(pltpu.PARALLEL,),
    )(src_hbm, idx_hbm)


def _sc_gather_rows(tab_hbm, idx_hbm, n, dst_hbm):
    def body(i_vmem, o_vmem):
        pltpu.sync_copy(tab_hbm.at[i_vmem.at[0]], o_vmem)

    pltpu.emit_pipeline(
        body,
        grid=(n // GATHER_WINDOW,),
        in_specs=[pl.BlockSpec((1, GATHER_WINDOW), lambda i: (0, i))],
        out_specs=[pl.BlockSpec((GATHER_WINDOW, PIECE_COLS), lambda i: (i, 0))],
        core_axis_name=_SC_AXES,
        dimension_semantics=(pltpu.PARALLEL,),
    )(idx_hbm, dst_hbm)


def _dispatch(tables, dests, dest_pad, n_rows):
    n_pad = dest_pad.shape[1]
    sizes = [t.shape[1] for t in tables]
    n_groups = len(tables)

    @pl.kernel(out_type=jax.ShapeDtypeStruct((N_PIECES, n_rows, PIECE_COLS), U32), mesh=_sc_mesh(), scratch_types=[])
    def scatter(*refs):
        tabs = refs[:n_groups]
        idxs = refs[n_groups:3 * n_groups]
        pad_hbm, o_hbm = refs[3 * n_groups], refs[3 * n_groups + 1]
        for piece in range(N_PIECES):
            dst = o_hbm.at[piece]
            for g in range(n_groups):
                for k in range(2):
                    _sc_scatter_rows(tabs[g].at[piece], idxs[2 * g + k], sizes[g], dst)
            _sc_scatter_rows(tabs[0].at[piece], pad_hbm, n_pad, dst)

    flat_idx = [d for pair in dests for d in pair]
    return scatter(*tables, *flat_idx, dest_pad)


def _collect(yo, dests):
    sizes = [pair[0].shape[1] for pair in dests]
    n_groups = len(dests)
    out_type = [jax.ShapeDtypeStruct((2, N_PIECES, n, PIECE_COLS), U32) for n in sizes]

    @pl.kernel(out_type=out_type, mesh=_sc_mesh(), scratch_types=[])
    def gather(*refs):
        yo_hbm = refs[0]
        idxs = refs[1:1 + 2 * n_groups]
        outs = refs[1 + 2 * n_groups:]
        for piece in range(N_PIECES):
            for g in range(n_groups):
                for k in range(2):
                    _sc_gather_rows(yo_hbm.at[piece], idxs[2 * g + k], sizes[g], outs[g].at[k, piece])

    flat_idx = [d for pair in dests for d in pair]
    return gather(yo, *flat_idx)


def _expert_kernel(blk_e_ref, blk_first_ref, n_used_ref, xs_ref, wg_ref, wu_ref, wd_ref, yo_ref, wg_b, wu_b, wd_b):
    b = pl.program_id(0)

    @pl.when(blk_first_ref[b] == 1)
    def _():
        wg_b[...] = wg_ref[0].astype(BF16)
        wu_b[...] = wu_ref[0].astype(BF16)
        wd_b[...] = wd_ref[0].astype(BF16)

    @pl.when(b < n_used_ref[0])
    def _():
        chunks = [c.astype(BF16) for c in _unpack_pieces([xs_ref[i] for i in range(N_PIECES)])]
        g = None
        u = None
        for i, xc in enumerate(chunks):
            rows_i = slice(i * PIECE_COLS, (i + 1) * PIECE_COLS)
            gi = _dot(xc, wg_b[rows_i, :])
            ui = _dot(xc, wu_b[rows_i, :])
            g = gi if g is None else g + gi
            u = ui if u is None else u + ui
        h = (g * _sigmoid(g)) * u
        for i, piece in enumerate(_pack_pieces(_dot(h.astype(BF16), wd_b[...]))):
            yo_ref[i] = piece

    @pl.when(b >= n_used_ref[0])
    def _():
        yo_ref[...] = jnp.zeros(yo_ref.shape, U32)


def _experts(xs, blk_e, blk_first, n_used, w_gate, w_up, w_down):
    p = xs.shape[1]
    tb = EXPERT_BLOCK
    nblk = p // tb
    row_block = pl.BlockSpec((N_PIECES, tb, PIECE_COLS), lambda i, e, f, n: (0, i, 0))
    grid_spec = pltpu.PrefetchScalarGridSpec(
        num_scalar_prefetch=3,
        grid=(nblk,),
        in_specs=[row_block,
                  pl.BlockSpec((1, D_MODEL, D_EXPERT), lambda i, e, f, n: (e[i], 0, 0)),
                  pl.BlockSpec((1, D_MODEL, D_EXPERT), lambda i, e, f, n: (e[i], 0, 0)),
                  pl.BlockSpec((1, D_EXPERT, D_MODEL), lambda i, e, f, n: (e[i], 0, 0))],
        out_specs=row_block,
        scratch_shapes=[pltpu.VMEM((D_MODEL, D_EXPERT), BF16),
                        pltpu.VMEM((D_MODEL, D_EXPERT), BF16),
                        pltpu.VMEM((D_EXPERT, D_MODEL), BF16)],
    )
    return pl.pallas_call(
        _expert_kernel,
        grid_spec=grid_spec,
        out_shape=jax.ShapeDtypeStruct((N_PIECES, p, PIECE_COLS), U32),
        compiler_params=pltpu.CompilerParams(dimension_semantics=("arbitrary",), vmem_limit_bytes=VMEM_LIMIT),
        name="experts",
    )(blk_e, blk_first, n_used, xs, w_gate, w_up, w_down)


def _combine_kernel(x1_ref, yg_ref, route_ref, ga2_ref, fg_ref, o_ref):
    bb, tm, _ = x1_ref.shape
    rows = bb * tm
    route = jnp.concatenate([route_ref[...], jnp.zeros((LANES - SUBLANES, rows), F32)], axis=0)
    w = route.T
    y = None
    for k in range(2):
        chunks = _unpack_pieces([yg_ref[k, i] for i in range(N_PIECES)])
        yk = w[:, 2 + k:3 + k] * jnp.concatenate(chunks, axis=1)
        y = yk if y is None else y + yk
    out = x1_ref[...] + ga2_ref[...] * y.reshape(bb, tm, D_MODEL)
    o_ref[...] = _rms(out) * fg_ref[...]


def _combine(x1, yg, route, ga2, final_g, bb, tm):
    b, t, _ = x1.shape
    nb, nt = b // bb, t // tm
    rows = bb * tm
    return pl.pallas_call(
        _combine_kernel,
        grid=(nb, nt),
        in_specs=[pl.BlockSpec((bb, tm, D_MODEL), lambda i, j: (i, j, 0)),
                  pl.BlockSpec((2, N_PIECES, rows, PIECE_COLS), lambda i, j: (0, 0, i * nt + j, 0)),
                  pl.BlockSpec((SUBLANES, rows), lambda i, j: (0, i * nt + j)),
                  pl.BlockSpec((bb, 1, D_MODEL), lambda i, j: (i, 0, 0)),
                  pl.BlockSpec((1, D_MODEL), lambda i, j: (0, 0))],
        out_specs=pl.BlockSpec((bb, tm, D_MODEL), lambda i, j: (i, j, 0)),
        out_shape=jax.ShapeDtypeStruct((b, t, D_MODEL), F32),
        compiler_params=pltpu.CompilerParams(dimension_semantics=("arbitrary", "arbitrary"),
                                             vmem_limit_bytes=VMEM_LIMIT),
        name="combine",
    )(x1, yg, route, ga2, final_g)


def _block_diag(w):
    n, k, _ = w.shape
    eye = jnp.eye(n, dtype=w.dtype)
    return (eye[:, None, :, None] * w[:, :, None, :]).reshape(n * k, n * k)


def _tiles(f, tile):
    b, _, length = f.shape
    return f.reshape(b, N_HEADS // 2, 2, length // tile, tile).transpose(0, 1, 3, 2, 4)


def _untile_rows(a, b, t):
    nb, nt, r, rows = a.shape
    bb = b // nb
    tm = t // nt
    return a.reshape(nb, nt, r, bb, tm).transpose(2, 0, 3, 1, 4).reshape(r, b, t)


def _group_front(x, c_mod, layer, conv_past, h0, cache):
    b, t, _ = x.shape
    sh1, sc1, ga1, sh2, sc2, ga2 = [m.reshape(b, 1, D_MODEL) for m in jnp.split(c_mod, 6, axis=-1)]
    q, k, v, lf_steps, rec, h_t, ctail = _inproj(
        x, 1.0 + sc1, sh1, layer["n1"], layer["w_all"], layer["b_f"], layer["conv_w"], layer["conv_b"],
        layer["wa_bd"], layer["b_a"], layer["wx_bd"], layer["b_x"], layer["lam"], layer["g_rec"], conv_past, h0)
    lf_t = _untile_rows(lf_steps, b, t).transpose(1, 0, 2)
    k_leaf = k.reshape(1, b, t, N_HEADS, HEAD_DIM)
    v_leaf = v.reshape(1, b, t, N_HEADS, HEAD_DIM)
    if cache is None:
        attn = _attention(q, k, v, _tiles(_cumsum_time(lf_t), min(t, ATTN_CHUNK)))
        attn, k_leaf, v_leaf = lax.optimization_barrier((attn, k_leaf, v_leaf))
    else:
        k_t, v_t, lf_past_t = cache
        total = k_t.shape[-1] + t
        padded = -(-total // CUMSUM_CHUNK) * CUMSUM_CHUNK
        lf_all = jnp.concatenate([lf_past_t, lf_t, jnp.zeros((b, N_HEADS, padded - total), F32)], axis=2)
        attn = _attention_cached(q, k, v, _cumsum_time(lf_all), k_t, v_t)
    bb, tm = (1, ROW_TILE) if t >= ROW_TILE else (ROW_TILE // t, t)
    x1, u2p, route, counts = _outproj(attn, rec, x, ga1, 1.0 + sc2, sh2, layer["n2"], layer["g_att"],
                                      layer["wo_a"], layer["wo_r"], layer["w_r"], layer["b_r"], bb, tm)
    route = _untile_rows(route, b, t).reshape(SUBLANES, b * t)
    route = (route, counts[:, 0].astype(jnp.int32))
    leaves = (k_leaf, v_leaf, lf_t.transpose(0, 2, 1)[None], h_t[None], ctail[None, :, SUBLANES - (CONV_W - 1):, :])
    combine_tile = (1, COMBINE_TILE) if t >= COMBINE_TILE else (bb, tm)
    return x1, u2p.reshape(N_PIECES, b * t, PIECE_COLS), route, ga2, combine_tile, leaves


def kernel(x_prompt, x_sample, c_prompt, c_sample, cache_k, cache_v, cache_logf, state_h, state_conv, norm1_g, norm2_g, w_ada, b_ada, w_in, b_f, conv_w, conv_b, w_rg_a, b_rg_a, w_rg_x, b_rg_x, rg_lambda, out_g_att, out_g_rec, w_out, w_route_group, b_route_group, w_route_expert, b_route_expert, w_exp_gate, w_exp_up, w_exp_down, final_g):
    bp, tp, _ = x_prompt.shape
    bs, ts, _ = x_sample.shape
    l = 0
    o3 = 3 * W_ATT
    o4 = o3 + N_HEADS
    w_in_l = w_in[l]
    w_all = jnp.concatenate([w_in_l[:, :o3], w_in_l[:, o4:], w_in_l[:, o3:o4],
                             jnp.zeros((D_MODEL, LANES - N_HEADS), F32)], axis=1).astype(BF16)
    w_r = jnp.zeros((D_MODEL, ROUTE_COLS), F32)
    w_r = w_r.at[:, 0:N_GROUPS].set(w_route_group[l]).at[:, EXPERT_COL0:EXPERT_COL0 + N_EXPERTS].set(w_route_expert[l])
    b_r = jnp.zeros((1, ROUTE_COLS), F32)
    b_r = b_r.at[0, 0:N_GROUPS].set(b_route_group[l]).at[0, N_GROUPS:SUBLANES].set(NEG_BIG)
    b_r = b_r.at[0, EXPERT_COL0:EXPERT_COL0 + N_EXPERTS].set(b_route_expert[l])
    layer = {
        "n1": norm1_g[l].reshape(1, D_MODEL), "n2": norm2_g[l].reshape(1, D_MODEL),
        "w_all": w_all,
        "b_f": jnp.concatenate([b_f[l], jnp.zeros((LANES - N_HEADS,), F32)]).reshape(1, LANES),
        "conv_w": conv_w[l], "conv_b": conv_b[l].reshape(1, W_REC),
        "wa_bd": _block_diag(w_rg_a[l]).astype(BF16), "b_a": b_rg_a[l].reshape(1, W_REC),
        "wx_bd": _block_diag(w_rg_x[l]).astype(BF16), "b_x": b_rg_x[l].reshape(1, W_REC),
        "lam": rg_lambda[l].reshape(1, W_REC), "g_rec": out_g_rec[l].reshape(1, W_REC),
        "g_att": out_g_att[l].reshape(1, W_ATT),
        "wo_a": w_out[l, :W_ATT].astype(BF16), "wo_r": w_out[l, W_ATT:].astype(BF16),
        "w_r": w_r.astype(BF16), "b_r": b_r,
    }

    ada = _ada(jnp.concatenate([c_prompt, c_sample], axis=0), w_ada[l], b_ada[l])
    pad_rows = SUBLANES - (CONV_W - 1)
    x1p, u2p_p, route_p, ga2p, tile_p, leaves_p = _group_front(
        x_prompt, ada[:bp], layer, jnp.zeros((bp, SUBLANES, W_REC), F32), jnp.zeros((bp, W_REC), F32), None)
    conv_past_s = jnp.concatenate([jnp.zeros((bs, pad_rows, W_REC), F32), state_conv[l]], axis=1)
    x1s, u2p_s, route_s, ga2s, tile_s, leaves_s = _group_front(
        x_sample, ada[bp:], layer, conv_past_s, state_h[l],
        (cache_k[l].transpose(0, 2, 3, 1), cache_v[l].transpose(0, 2, 3, 1), cache_logf[l].transpose(0, 2, 1)))

    (route_p, counts_p), (route_s, counts_s) = route_p, route_s
    n_p, n_s = bp * tp, bs * ts
    n_slot = 2 * (n_p + n_s)
    tb = EXPERT_BLOCK
    experts = jnp.arange(N_EXPERTS, dtype=jnp.int32)
    counts = counts_p + counts_s
    padded = ((counts + tb - 1) // tb) * tb
    pend = jnp.cumsum(padded)
    pstart = pend - padded

    def slot_rows(route, first_row):
        eid = route[0:2].astype(jnp.int32)
        onehot = (eid[:, :, None] == experts[None, None, :]).astype(jnp.int32)
        return jnp.sum(onehot * first_row[None, None, :], axis=2) + route[4:6].astype(jnp.int32)

    dests = [[d[k].reshape(1, -1) for k in range(2)]
             for d in (slot_rows(route_p, pstart), slot_rows(route_s, pstart + counts_p))]
    nblk = -(-n_slot // tb) + N_EXPERTS
    n_rows = nblk * tb
    blk_row0 = jnp.arange(nblk, dtype=jnp.int32) * tb
    blk_e = jnp.minimum(jnp.sum((pend[None, :] <= blk_row0[:, None]).astype(jnp.int32), axis=1), N_EXPERTS - 1)
    blk_first = jnp.concatenate([jnp.ones((1,), jnp.int32), (blk_e[1:] != blk_e[:-1]).astype(jnp.int32)])
    n_used = (pend[-1:] // tb).astype(jnp.int32)
    pad_e = padded - counts
    cpad = jnp.cumsum(pad_e)
    j = jnp.arange(n_rows - n_slot, dtype=jnp.int32)
    owner = (cpad[None, :] <= j[:, None]).astype(jnp.int32)
    e_j = jnp.minimum(jnp.sum(owner, axis=1), N_EXPERTS - 1)
    sel = (e_j[:, None] == experts[None, :]).astype(jnp.int32)
    in_expert = jnp.sum(sel * (pstart + counts - (cpad - pad_e))[None, :], axis=1) + j
    dest_pad = jnp.where(j < cpad[-1], in_expert, pend[-1] + (j - cpad[-1])).reshape(1, -1)

    xs = _dispatch([u2p_p, u2p_s], dests, dest_pad, n_rows)
    yo = _experts(xs, blk_e, blk_first, n_used, w_exp_gate[l], w_exp_up[l], w_exp_down[l])
    yg_p, yg_s = _collect(yo, dests)

    fg = final_g.reshape(1, D_MODEL)
    y_prompt = _combine(x1p, yg_p, route_p, ga2p, fg, *tile_p)
    y_sample = _combine(x1s, yg_s, route_s, ga2s, fg, *tile_s)
    return (y_prompt, y_sample) + leaves_p + leaves_s
```

```python
import functools
import math

import jax
import jax.numpy as jnp
from jax import lax
from jax.experimental import pallas as pl
from jax.experimental.pallas import tpu as pltpu
from jax.experimental.pallas import tpu_sc as plsc

F32 = jnp.float32
BF16 = jnp.bfloat16
U32 = jnp.uint32

D_MODEL = 1024
N_HEADS = 8
HEAD_DIM = 64
W_ATT = N_HEADS * HEAD_DIM
W_REC = D_MODEL - W_ATT
N_REC_BLOCKS = 8
REC_BLOCK = W_REC // N_REC_BLOCKS
CONV_W = 4
RG_C = 8.0
N_GROUPS = 4
EXPERTS_PER_GROUP = 8
N_EXPERTS = N_GROUPS * EXPERTS_PER_GROUP
D_EXPERT = 256
EPS = 1e-6

LANES = 128
SUBLANES = 8
LOG2E = math.log2(math.e)
Q_SCALE = LOG2E * HEAD_DIM ** -0.5
NEG_BIG = -1e30
VMEM_LIMIT = 48 * 1024 * 1024

SEQ_ROWS = 64
BATCH_ROWS = 8
CUMSUM_CHUNK = 512
CUMSUM_BATCH = 8
ATTN_TILE = 2048
ATTN_CHUNK = 512
ATTN_UNROLL = 2
ROW_TILE = 512
EXPERT_BLOCK = 512
COMBINE_TILE = 1024
GATHER_WINDOW = 128
PIECE_COLS = 256
N_PIECES = D_MODEL // (2 * PIECE_COLS)
ROUTE_COLS = 128
EXPERT_COL0 = 8


def _split3(x):
    hi = x.astype(BF16)
    r1 = x - hi.astype(F32)
    mid = r1.astype(BF16)
    lo = (r1 - mid.astype(F32)).astype(BF16)
    return hi, mid, lo


def _dot(a, b):
    return jnp.dot(a, b, preferred_element_type=F32)


def _rms(x):
    return x * lax.rsqrt(jnp.mean(x * x, axis=-1, keepdims=True) + EPS)


def _sigmoid(x):
    return 0.5 * jnp.tanh(0.5 * x) + 0.5


def _pack_bf16_pairs(y):
    n = y.shape[-1] // 2
    yb = y.astype(BF16).astype(F32)
    lo = pltpu.bitcast(yb[:, :n], U32)
    hi = pltpu.bitcast(yb[:, n:], U32)
    return (lo >> 16) | hi


def _unpack_bf16_pairs(p):
    lo = pltpu.bitcast(p << 16, F32)
    hi = pltpu.bitcast(p & jnp.uint32(0xFFFF0000), F32)
    return lo, hi


def _pack_pieces(y):
    w = 2 * PIECE_COLS
    return [_pack_bf16_pairs(y[:, i * w:(i + 1) * w]) for i in range(N_PIECES)]


def _unpack_pieces(pieces):
    chunks = []
    for p in pieces:
        chunks.extend(_unpack_bf16_pairs(p))
    return chunks


def _ada_kernel(c_ref, w_ref, b_ref, o_ref):
    c = c_ref[...]
    a = c * jax.nn.sigmoid(c)
    a_hi, a_mid, _ = _split3(a)
    w_hi, w_mid, _ = _split3(w_ref[...])
    o_ref[...] = _dot(a_hi, w_hi) + _dot(a_mid, w_hi) + _dot(a_hi, w_mid) + b_ref[...]


def _ada(c, w_ada, b_ada):
    rows = c.shape[0]
    n = w_ada.shape[1]
    tn = 1536
    return pl.pallas_call(
        _ada_kernel,
        grid=(n // tn,),
        in_specs=[pl.BlockSpec((rows, D_MODEL), lambda j: (0, 0)),
                  pl.BlockSpec((D_MODEL, tn), lambda j: (0, j)),
                  pl.BlockSpec((1, tn), lambda j: (0, j))],
        out_specs=pl.BlockSpec((rows, tn), lambda j: (0, j)),
        out_shape=jax.ShapeDtypeStruct((rows, n), F32),
        compiler_params=pltpu.CompilerParams(vmem_limit_bytes=VMEM_LIMIT),
        name="ada",
    )(c, w_ada, b_ada.reshape(1, n))


def _inproj_kernel(x_ref, g1_ref, sh1_ref, n1_ref, w_ref, bf_ref, cw_ref, cb_ref, wa_ref, ba_ref, wx_ref, bx_ref,
                   lam_ref, grec_ref, cpast_ref, h0_ref,
                   q_ref, k_ref, v_ref, lf_ref, rec_ref, ht_ref, ctail_ref,
                   xbuf, ub_scr, z_scr, xc_scr, pr_scr, pi_scr, a_scr, u_scr, hs_scr, gl_scr, h_scr):
    bb, tm, _ = x_ref.shape
    rows = bb * tm
    n_lane_tiles = W_REC // LANES
    half = W_REC // 2
    o_x = 3 * W_ATT

    @pl.when(pl.program_id(1) == 0)
    def _():
        xbuf[:, 0:SUBLANES, :] = cpast_ref[...]
        h_scr[...] = h0_ref[...]

    mod = n1_ref[...] * g1_ref[...]
    for b in range(bb):
        rb = slice(b * tm, (b + 1) * tm)
        ub_scr[rb, :] = (_rms(x_ref[b]) * mod[b] + sh1_ref[b]).astype(BF16)

    z_scr[...] = _dot(ub_scr[...], w_ref[:, o_x:])

    cw = cw_ref[...]
    for b in range(bb):
        rb = slice(b * tm, (b + 1) * tm)
        xb = z_scr[rb, 0:W_REC]
        xbuf[b, SUBLANES:SUBLANES + tm, :] = xb
        xc = cb_ref[...] + cw[3:4, :] * xb
        for j in range(CONV_W - 1):
            off = SUBLANES - (CONV_W - 1) + j
            xc = xc + cw[j:j + 1, :] * xbuf[b, off:off + tm, :]
        tail = xbuf[b, tm:tm + SUBLANES, :]
        ctail_ref[b] = tail
        xbuf[b, 0:SUBLANES, :] = tail
        xc_scr[rb, :] = xc

    xcb = xc_scr[...].astype(BF16)
    for w_gate, pre in ((wa_ref, pr_scr), (wx_ref, pi_scr)):
        pre[:, :half] = _dot(xcb[:, :half], w_gate[:half, :half])
        pre[:, half:] = _dot(xcb[:, half:], w_gate[half:, half:])

    zq = _dot(ub_scr[...], w_ref[:, :o_x])
    q_ref[...] = (zq[:, 0:W_ATT] * Q_SCALE).reshape(bb, tm, W_ATT).astype(BF16)
    k_ref[...] = zq[:, W_ATT:2 * W_ATT].reshape(bb, tm, W_ATT)
    v_ref[...] = zq[:, 2 * W_ATT:3 * W_ATT].reshape(bb, tm, W_ATT)

    zf = z_scr[:, 2 * W_REC:2 * W_REC + LANES] + bf_ref[...]
    lf = jnp.minimum(zf, 0.0) - jnp.log1p(jnp.exp(-jnp.abs(zf)))
    lf_ref[0, 0] = lf.T[0:N_HEADS, :]

    nlam = -lam_ref[...]
    decay = -RG_C * (jnp.maximum(nlam, 0.0) + jnp.log1p(jnp.exp(-jnp.abs(nlam))))
    for b in range(bb):
        rb = slice(b * tm, (b + 1) * tm)
        r = _sigmoid(pr_scr[rb, :] + ba_ref[...])
        i = _sigmoid(pi_scr[rb, :] + bx_ref[...])
        log_a = decay * r
        t = jnp.tanh(log_a)
        neg_expm1 = -2.0 * t / (1.0 - t)
        root = jnp.where(neg_expm1 > 0.0, neg_expm1 * lax.rsqrt(neg_expm1), 0.0)
        pr_scr[rb, :] = jnp.exp(log_a)
        pi_scr[rb, :] = root * (i * xc_scr[rb, :])
        gb = z_scr[rb, W_REC:2 * W_REC]
        gl_scr[rb, :] = 0.5 * gb * (1.0 + jnp.tanh(math.sqrt(2.0 / math.pi) * (gb + 0.044715 * (gb * gb * gb))))

    def lane_tile(ref, c):
        return ref[:, c * LANES:(c + 1) * LANES].reshape(bb, tm, LANES)

    for c in range(n_lane_tiles):
        a_scr[c] = jnp.swapaxes(lane_tile(pr_scr, c), 0, 1)
        u_scr[c] = jnp.swapaxes(lane_tile(pi_scr, c), 0, 1)
    for c in range(n_lane_tiles):
        h = h_scr[:, c * LANES:(c + 1) * LANES]
        for step in range(tm):
            h = a_scr[c, step] * h + u_scr[c, step]
            hs_scr[c, step] = h
        h_scr[:, c * LANES:(c + 1) * LANES] = h
    ht_ref[...] = h_scr[...]
    for c in range(n_lane_tiles):
        pr_scr[:, c * LANES:(c + 1) * LANES] = jnp.swapaxes(hs_scr[c], 0, 1).reshape(rows, LANES)

    for b in range(bb):
        rb = slice(b * tm, (b + 1) * tm)
        y = gl_scr[rb, :] * pr_scr[rb, :]
        rec_ref[b] = (_rms(y) * grec_ref[...]).astype(BF16)


def _inproj(x, g1, sh1, n1, w_all, b_f, conv_w, conv_b, wa_bd, b_a, wx_bd, b_x, lam, g_rec, conv_past, h0):
    b, t, _ = x.shape
    bb, tm = BATCH_ROWS, SEQ_ROWS
    nb, nt = b // bb, t // tm
    rows = bb * tm
    wcols = w_all.shape[1]
    const2 = lambda shape: pl.BlockSpec(shape, lambda i, j: (0, 0))
    per_b = lambda last: pl.BlockSpec((bb, 1, last), lambda i, j: (i, 0, 0))
    seq = lambda last: pl.BlockSpec((bb, tm, last), lambda i, j: (i, j, 0))
    return pl.pallas_call(
        _inproj_kernel,
        grid=(nb, nt),
        in_specs=[seq(D_MODEL), per_b(D_MODEL), per_b(D_MODEL), const2((1, D_MODEL)),
                  const2((D_MODEL, wcols)), const2((1, LANES)),
                  const2((CONV_W, W_REC)), const2((1, W_REC)),
                  const2((W_REC, W_REC)), const2((1, W_REC)), const2((W_REC, W_REC)), const2((1, W_REC)),
                  const2((1, W_REC)), const2((1, W_REC)),
                  pl.BlockSpec((bb, SUBLANES, W_REC), lambda i, j: (i, 0, 0)),
                  pl.BlockSpec((bb, W_REC), lambda i, j: (i, 0))],
        out_specs=[seq(W_ATT), seq(W_ATT), seq(W_ATT),
                   pl.BlockSpec((1, 1, N_HEADS, rows), lambda i, j: (i, j, 0, 0)),
                   seq(W_REC),
                   pl.BlockSpec((bb, W_REC), lambda i, j: (i, 0)),
                   pl.BlockSpec((bb, SUBLANES, W_REC), lambda i, j: (i, 0, 0))],
        out_shape=[jax.ShapeDtypeStruct((b, t, W_ATT), BF16),
                   jax.ShapeDtypeStruct((b, t, W_ATT), F32),
                   jax.ShapeDtypeStruct((b, t, W_ATT), F32),
                   jax.ShapeDtypeStruct((nb, nt, N_HEADS, rows), F32),
                   jax.ShapeDtypeStruct((b, t, W_REC), BF16),
                   jax.ShapeDtypeStruct((b, W_REC), F32),
                   jax.ShapeDtypeStruct((b, SUBLANES, W_REC), F32)],
        scratch_shapes=[pltpu.VMEM((bb, tm + SUBLANES, W_REC), F32),
                        pltpu.VMEM((rows, D_MODEL), BF16),
                        pltpu.VMEM((rows, 2 * W_REC + LANES), F32),
                        pltpu.VMEM((rows, W_REC), F32),
                        pltpu.VMEM((rows, W_REC), F32),
                        pltpu.VMEM((rows, W_REC), F32),
                        pltpu.VMEM((W_REC // LANES, tm, bb, LANES), F32),
                        pltpu.VMEM((W_REC // LANES, tm, bb, LANES), F32),
                        pltpu.VMEM((W_REC // LANES, tm, bb, LANES), F32),
                        pltpu.VMEM((rows, W_REC), F32),
                        pltpu.VMEM((bb, W_REC), F32)],
        compiler_params=pltpu.CompilerParams(dimension_semantics=("arbitrary", "arbitrary"),
                                             vmem_limit_bytes=VMEM_LIMIT),
        name="inproj",
    )(x, g1, sh1, n1, w_all, b_f, conv_w, conv_b, wa_bd, b_a, wx_bd, b_x, lam, g_rec, conv_past, h0)


def _cumsum_kernel(x_ref, o_ref):
    cb, heads, length = x_ref.shape
    n = cb * heads
    c = CUMSUM_CHUNK
    upper = (lax.broadcasted_iota(jnp.int32, (c, c), 0) <= lax.broadcasted_iota(jnp.int32, (c, c), 1)).astype(BF16)
    carry = jnp.zeros((n, 1), F32)
    for j in range(length // c):
        x = x_ref[:, :, j * c:(j + 1) * c].reshape(n, c)
        hi, mid, lo = _split3(x)
        parts = jnp.concatenate([hi.astype(F32), mid.astype(F32), lo.astype(F32)], axis=0)
        sums = _dot(parts.astype(BF16), upper)
        out = sums[0:n] + sums[n:2 * n] + sums[2 * n:3 * n] + carry
        o_ref[:, :, j * c:(j + 1) * c] = out.reshape(cb, heads, c)
        carry = out[:, c - 1:c]


def _cumsum_time(lf_t):
    b, h, length = lf_t.shape
    cb = min(b, CUMSUM_BATCH)
    return pl.pallas_call(
        _cumsum_kernel,
        grid=(b // cb,),
        in_specs=[pl.BlockSpec((cb, h, length), lambda i: (i, 0, 0))],
        out_specs=pl.BlockSpec((cb, h, length), lambda i: (i, 0, 0)),
        out_shape=jax.ShapeDtypeStruct((b, h, length), F32),
        name="cumsum",
    )(lf_t)


def _attn_tile(qh, kt, vt, bias, m_ref, acc_ref, mask):
    s = lax.dot_general(qh, kt, (((1,), (1,)), ((), ())), preferred_element_type=F32) + bias
    if mask is not None:
        s = jnp.where(mask, s, NEG_BIG)
    width = min(LANES, s.shape[1])
    slabs = [s[:, c * width:(c + 1) * width] for c in range(s.shape[1] // width)]
    part = slabs[0]
    for sl in slabs[1:]:
        part = jnp.maximum(part, sl)
    m = m_ref[...]
    m_new = jnp.maximum(m, jnp.broadcast_to(jnp.max(part, axis=1, keepdims=True), m.shape))
    p = jnp.concatenate([jnp.exp2(sl - m_new[:, :width]) for sl in slabs], axis=1)
    acc_ref[...] = jnp.exp2(m - m_new) * acc_ref[...] + _dot(p.astype(BF16), vt)
    m_ref[...] = m_new


def _attn_kernel(q_ref, k_ref, v_ref, f_ref, o_ref, kb, vb, m_scr, acc_scr, *, tq, ck):
    qi = pl.program_id(2)
    chunks_per_tile = tq // ck

    @pl.when(qi == 0)
    def _():
        v_f32 = v_ref[0]
        first = lax.broadcasted_iota(jnp.int32, v_f32.shape, 1) < HEAD_DIM
        kb[...] = k_ref[0].astype(BF16)
        vb[0] = jnp.where(first, v_f32, 1.0).astype(BF16)
        vb[1] = jnp.where(first, 1.0, v_f32).astype(BF16)

    q = q_ref[0]
    first_head = lax.broadcasted_iota(jnp.int32, q.shape, 1) < HEAD_DIM
    qh = (jnp.where(first_head, q, jnp.zeros_like(q)), jnp.where(first_head, jnp.zeros_like(q), q))

    ref_f = f_ref[0, 0, qi * chunks_per_tile][:, 0:1]
    m_scr[...] = jnp.full(m_scr.shape, NEG_BIG, F32)
    acc_scr[...] = jnp.zeros(acc_scr.shape, F32)

    def chunk(j, row0=0, masked=False):
        start = pl.multiple_of(j * ck, ck)
        kt = kb[pl.ds(start, ck), :]
        f_tile = f_ref[0, 0, j]
        mask = None
        if masked:
            shape = (tq - row0, ck)
            mask = lax.broadcasted_iota(jnp.int32, shape, 1) <= lax.broadcasted_iota(jnp.int32, shape, 0)
        for hd in range(2):
            bias = (ref_f[hd:hd + 1, :] - f_tile[hd:hd + 1, :]) * LOG2E
            _attn_tile(qh[hd][row0:, :], kt, vb[hd, pl.ds(start, ck), :], bias,
                       m_scr.at[hd, row0:tq], acc_scr.at[hd, row0:tq], mask)

    n_chunks = qi * chunks_per_tile

    def chunk_body(jj, carry):
        for u in range(ATTN_UNROLL):
            chunk(jj * ATTN_UNROLL + u)
        return carry

    lax.fori_loop(0, n_chunks // ATTN_UNROLL, chunk_body, 0)
    for r in range(ATTN_UNROLL - 1):
        @pl.when(r < n_chunks % ATTN_UNROLL)
        def _():
            chunk((n_chunks // ATTN_UNROLL) * ATTN_UNROLL + r)

    for c in range(chunks_per_tile):
        chunk(n_chunks + c, row0=c * ck, masked=True)

    outs = []
    for hd in range(2):
        acc = acc_scr[hd]
        outs.append(acc / pltpu.roll(acc, HEAD_DIM, axis=1))
    o_ref[0] = jnp.where(first_head, outs[0], outs[1]).astype(BF16)


def _attention(q, k, v, f_tiles):
    b, t, _ = q.shape
    tq = min(t, ATTN_TILE)
    ck = f_tiles.shape[-1]
    q_spec = pl.BlockSpec((1, tq, LANES), lambda i, p, j: (i, j, p))
    full = pl.BlockSpec((1, t, LANES), lambda i, p, j: (i, 0, p))
    return pl.pallas_call(
        functools.partial(_attn_kernel, tq=tq, ck=ck),
        grid=(b, N_HEADS // 2, t // tq),
        in_specs=[q_spec, full, full,
                  pl.BlockSpec((1, 1) + f_tiles.shape[2:], lambda i, p, j: (i, p, 0, 0, 0))],
        out_specs=q_spec,
        out_shape=jax.ShapeDtypeStruct((b, t, W_ATT), BF16),
        scratch_shapes=[pltpu.VMEM((t, LANES), BF16), pltpu.VMEM((2, t, LANES), BF16),
                        pltpu.VMEM((2, tq, LANES), F32), pltpu.VMEM((2, tq, LANES), F32)],
        compiler_params=pltpu.CompilerParams(dimension_semantics=("arbitrary", "arbitrary", "arbitrary"),
                                             vmem_limit_bytes=VMEM_LIMIT),
        name="attn",
    )(q, k, v, f_tiles)


def _attn_cached_kernel(q_ref, kn_ref, vn_ref, f_ref, kt_ref, vt_ref, o_ref, *, past):
    t = q_ref.shape[1]
    q = q_ref[0]
    kn = kn_ref[0].astype(BF16)
    vn = vn_ref[0].astype(BF16)
    f = f_ref[0]
    causal = lax.broadcasted_iota(jnp.int32, (t, t), 1) <= lax.broadcasted_iota(jnp.int32, (t, t), 0)
    last_dims = (((1,), (1,)), ((), ()))
    outs = []
    for h in range(N_HEADS):
        cols = slice(h * HEAD_DIM, (h + 1) * HEAD_DIM)
        qh = q[:, cols]
        ref_f = f[h:h + 1, past:past + 1]
        s_p = _dot(qh, kt_ref[0, h].astype(BF16)) + (ref_f - f[h:h + 1, 0:past]) * LOG2E
        s_n = lax.dot_general(qh, kn[:, cols], last_dims, preferred_element_type=F32)
        s_n = jnp.where(causal, s_n + (ref_f - f[h:h + 1, past:past + t]) * LOG2E, NEG_BIG)
        m = jnp.maximum(jnp.max(s_p, axis=1, keepdims=True), jnp.max(s_n, axis=1, keepdims=True))
        p_p = jnp.exp2(s_p - m)
        p_n = jnp.exp2(s_n - m)
        l = jnp.sum(p_p, axis=1, keepdims=True) + jnp.sum(p_n, axis=1, keepdims=True)
        o = lax.dot_general(p_p.astype(BF16), vt_ref[0, h].astype(BF16), last_dims, preferred_element_type=F32)
        o = o + _dot(p_n.astype(BF16), vn[:, cols])
        outs.append(o / l)
    o_ref[0] = jnp.concatenate(outs, axis=1).astype(BF16)


def _attention_cached(q, k_new, v_new, f_all, k_t, v_t):
    b, t, _ = q.shape
    past = k_t.shape[-1]
    rows = pl.BlockSpec((1, t, W_ATT), lambda i: (i, 0, 0))
    cache = pl.BlockSpec((1, N_HEADS, HEAD_DIM, past), lambda i: (i, 0, 0, 0))
    return pl.pallas_call(
        functools.partial(_attn_cached_kernel, past=past),
        grid=(b,),
        in_specs=[rows, rows, rows, pl.BlockSpec((1,) + f_all.shape[1:], lambda i: (i, 0, 0)), cache, cache],
        out_specs=rows,
        out_shape=jax.ShapeDtypeStruct((b, t, W_ATT), BF16),
        compiler_params=pltpu.CompilerParams(dimension_semantics=("arbitrary",), vmem_limit_bytes=VMEM_LIMIT),
        name="attn_cached",
    )(q, k_new, v_new, f_all, k_t, v_t)


def _outproj_kernel(attn_ref, rec_ref, x_ref, ga1_ref, g2_ref, sh2_ref, n2_ref, gatt_ref, woa_ref, wor_ref,
                    wr_ref, br_ref,
                    x1_ref, u2p_ref, route_ref, cnt_ref, before_scr, cnt_scr):
    bb, tm, _ = x_ref.shape
    rows = bb * tm

    @pl.when((pl.program_id(0) == 0) & (pl.program_id(1) == 0))
    def _():
        shape = (rows, rows)
        before_scr[...] = (lax.broadcasted_iota(jnp.int32, shape, 0)
                           < lax.broadcasted_iota(jnp.int32, shape, 1)).astype(BF16)
        cnt_scr[...] = jnp.zeros(cnt_scr.shape, F32)

    attn = attn_ref[...].astype(F32)
    an = (_rms(attn) * gatt_ref[...]).reshape(rows, W_ATT).astype(BF16)
    mix = _dot(an, woa_ref[...]) + _dot(rec_ref[...].reshape(rows, W_REC), wor_ref[...])
    x1 = x_ref[...] + ga1_ref[...] * mix.reshape(bb, tm, D_MODEL)
    x1_ref[...] = x1
    u2 = (_rms(x1) * (n2_ref[...] * g2_ref[...]) + sh2_ref[...]).reshape(rows, D_MODEL)
    for i, piece in enumerate(_pack_pieces(u2)):
        u2p_ref[i] = piece.reshape(bb, tm, PIECE_COLS)

    logits = _dot(u2.astype(BF16), wr_ref[...]) + br_ref[...]
    lt = logits.T
    row8 = lax.broadcasted_iota(jnp.int32, (SUBLANES, rows), 0).astype(F32)
    lg = lt[0:SUBLANES]
    m_g = jnp.max(lg, axis=0, keepdims=True)
    gidx = jnp.min(jnp.where(lg == m_g, row8, float(SUBLANES)), axis=0, keepdims=True)
    p_top = 1.0 / jnp.sum(jnp.exp(lg - m_g), axis=0, keepdims=True)
    leg = jnp.zeros((EXPERTS_PER_GROUP, rows), F32)
    for g in range(N_GROUPS):
        lo = EXPERT_COL0 + g * EXPERTS_PER_GROUP
        leg = jnp.where(gidx == float(g), lt[lo:lo + EXPERTS_PER_GROUP], leg)
    v1 = jnp.max(leg, axis=0, keepdims=True)
    i1 = jnp.min(jnp.where(leg == v1, row8, float(SUBLANES)), axis=0, keepdims=True)
    leg2 = jnp.where(row8 == i1, -jnp.inf, leg)
    v2 = jnp.max(leg2, axis=0, keepdims=True)
    i2 = jnp.min(jnp.where(leg2 == v2, row8, float(SUBLANES)), axis=0, keepdims=True)
    e21 = jnp.exp(v2 - v1)
    w1 = p_top / (1.0 + e21)
    w2 = w1 * e21
    base = gidx * float(EXPERTS_PER_GROUP)
    e0 = base + i1
    e1 = base + i2

    erow = lax.broadcasted_iota(jnp.int32, (N_EXPERTS, rows), 0).astype(F32)
    hit0 = erow == e0
    hit1 = erow == e1
    hits = jnp.concatenate([jnp.where(hit0, 1.0, 0.0), jnp.where(hit1, 1.0, 0.0)], axis=0)
    earlier = _dot(hits.astype(BF16), before_scr[...])
    seen = cnt_scr[:, 0:1] + earlier[0:N_EXPERTS] + earlier[N_EXPERTS:]
    rank0 = jnp.sum(jnp.where(hit0, seen, 0.0), axis=0, keepdims=True)
    rank1 = jnp.sum(jnp.where(hit1, seen, 0.0), axis=0, keepdims=True)
    step_counts = jnp.sum(hits[0:N_EXPERTS] + hits[N_EXPERTS:], axis=1, keepdims=True)
    cnt_scr[...] = cnt_scr[...] + step_counts
    cnt_ref[...] = cnt_scr[...]

    out = jnp.zeros((SUBLANES, rows), F32)
    for r_idx, val in enumerate((e0, e1, w1, w2, rank0, rank1)):
        out = jnp.where(row8 == float(r_idx), val, out)
    route_ref[0, 0] = out


def _outproj(attn, rec, x, ga1, g2, sh2, n2, g_att, wo_a, wo_r, w_r, b_r, bb, tm):
    b, t, _ = x.shape
    nb, nt = b // bb, t // tm
    rows = bb * tm
    const2 = lambda shape: pl.BlockSpec(shape, lambda i, j: (0, 0))
    per_b = pl.BlockSpec((bb, 1, D_MODEL), lambda i, j: (i, 0, 0))
    seq = lambda last: pl.BlockSpec((bb, tm, last), lambda i, j: (i, j, 0))
    return pl.pallas_call(
        _outproj_kernel,
        grid=(nb, nt),
        in_specs=[seq(W_ATT), seq(W_REC), seq(D_MODEL), per_b, per_b, per_b, const2((1, D_MODEL)),
                  const2((1, W_ATT)), const2((W_ATT, D_MODEL)), const2((W_REC, D_MODEL)),
                  const2((D_MODEL, ROUTE_COLS)), const2((1, ROUTE_COLS))],
        out_specs=[seq(D_MODEL),
                   pl.BlockSpec((N_PIECES, bb, tm, PIECE_COLS), lambda i, j: (0, i, j, 0)),
                   pl.BlockSpec((1, 1, SUBLANES, rows), lambda i, j: (i, j, 0, 0)),
                   const2((N_EXPERTS, LANES))],
        out_shape=[jax.ShapeDtypeStruct((b, t, D_MODEL), F32),
                   jax.ShapeDtypeStruct((N_PIECES, b, t, PIECE_COLS), U32),
                   jax.ShapeDtypeStruct((nb, nt, SUBLANES, rows), F32),
                   jax.ShapeDtypeStruct((N_EXPERTS, LANES), F32)],
        scratch_shapes=[pltpu.VMEM((rows, rows), BF16), pltpu.VMEM((N_EXPERTS, LANES), F32)],
        compiler_params=pltpu.CompilerParams(dimension_semantics=("arbitrary", "arbitrary"),
                                             vmem_limit_bytes=VMEM_LIMIT),
        name="outproj",
    )(attn, rec, x, ga1, g2, sh2, n2, g_att, wo_a, wo_r, w_r, b_r)


_SC_AXES = ("core", "subcore")


def _sc_mesh():
    return plsc.VectorSubcoreMesh(core_axis_name=_SC_AXES[0], subcore_axis_name=_SC_AXES[1])


def _sc_scatter_rows(src_hbm, idx_hbm, n, dst_hbm):
    def body(x_vmem, i_vmem):
        pltpu.sync_copy(x_vmem, dst_hbm.at[i_vmem.at[0]])

    pltpu.emit_pipeline(
        body,
        grid=(n // GATHER_WINDOW,),
        in_specs=[pl.BlockSpec((GATHER_WINDOW, PIECE_COLS), lambda i: (i, 0)),
                  pl.BlockSpec((1, GATHER_WINDOW), lambda i: (0, i))],
        out_specs=[],
        core_axis_name=_SC_AXES,
        dimension_semantics=(pltpu.PARALLEL,),
    )(src_hbm, idx_hbm)


def _sc_gather_rows(tab_hbm, idx_hbm, n, dst_hbm):
    def body(i_vmem, o_vmem):
        pltpu.sync_copy(tab_hbm.at[i_vmem.at[0]], o_vmem)

    pltpu.emit_pipeline(
        body,
        grid=(n // GATHER_WINDOW,),
        in_specs=[pl.BlockSpec((1, GATHER_WINDOW), lambda i: (0, i))],
        out_specs=[pl.BlockSpec((GATHER_WINDOW, PIECE_COLS), lambda i: (i, 0))],
        core_axis_name=_SC_AXES,
        dimension_semantics=(pltpu.PARALLEL,),
    )(idx_hbm, dst_hbm)


def _dispatch(tables, dests, dest_pad, n_rows):
    n_pad = dest_pad.shape[1]
    sizes = [t.shape[1] for t in tables]
    n_groups = len(tables)

    @pl.kernel(out_type=jax.ShapeDtypeStruct((N_PIECES, n_rows, PIECE_COLS), U32), mesh=_sc_mesh(), scratch_types=[])
    def scatter(*refs):
        tabs = refs[:n_groups]
        idxs = refs[n_groups:3 * n_groups]
        pad_hbm, o_hbm = refs[3 * n_groups], refs[3 * n_groups + 1]
        for piece in range(N_PIECES):
            dst = o_hbm.at[piece]
            for g in range(n_groups):
                for k in range(2):
                    _sc_scatter_rows(tabs[g].at[piece], idxs[2 * g + k], sizes[g], dst)
            _sc_scatter_rows(tabs[0].at[piece], pad_hbm, n_pad, dst)

    flat_idx = [d for pair in dests for d in pair]
    return scatter(*tables, *flat_idx, dest_pad)


def _collect(yo, dests):
    sizes = [pair[0].shape[1] for pair in dests]
    n_groups = len(dests)
    out_type = [jax.ShapeDtypeStruct((2, N_PIECES, n, PIECE_COLS), U32) for n in sizes]

    @pl.kernel(out_type=out_type, mesh=_sc_mesh(), scratch_types=[])
    def gather(*refs):
        yo_hbm = refs[0]
        idxs = refs[1:1 + 2 * n_groups]
        outs = refs[1 + 2 * n_groups:]
        for piece in range(N_PIECES):
            for g in range(n_groups):
                for k in range(2):
                    _sc_gather_rows(yo_hbm.at[piece], idxs[2 * g + k], sizes[g], outs[g].at[k, piece])

    flat_idx = [d for pair in dests for d in pair]
    return gather(yo, *flat_idx)


def _expert_kernel(blk_e_ref, run_start_ref, next_e_ref, has_next_ref, slot_ref, n_used_ref,
                   xs_ref, wg_hbm, wu_hbm, wd_hbm, yo_ref, wg_f, wu_f, wd_f, sems, wg_b, wu_b, wd_b):
    b = pl.program_id(0)

    def weight_copies(e, s):
        return [pltpu.make_async_copy(src.at[e], dst.at[s], sems.at[s, i])
                for i, (src, dst) in enumerate(((wg_hbm, wg_f), (wu_hbm, wu_f), (wd_hbm, wd_f)))]

    @pl.when(run_start_ref[b] == 1)
    def _():
        s = slot_ref[b]

        @pl.when(b == 0)
        def _():
            for c in weight_copies(blk_e_ref[b], s):
                c.start()

        for c in weight_copies(blk_e_ref[b], s):
            c.wait()
        wg_b[...] = wg_f[s].astype(BF16)
        wu_b[...] = wu_f[s].astype(BF16)
        wd_b[...] = wd_f[s].astype(BF16)

        @pl.when(has_next_ref[b] == 1)
        def _():
            for c in weight_copies(next_e_ref[b], 1 - s):
                c.start()

    @pl.when(b < n_used_ref[0])
    def _():
        chunks = [c.astype(BF16) for c in _unpack_pieces([xs_ref[i] for i in range(N_PIECES)])]
        g = None
        u = None
        for i, xc in enumerate(chunks):
            rows_i = slice(i * PIECE_COLS, (i + 1) * PIECE_COLS)
            gi = _dot(xc, wg_b[rows_i, :])
            ui = _dot(xc, wu_b[rows_i, :])
            g = gi if g is None else g + gi
            u = ui if u is None else u + ui
        h = (g * _sigmoid(g)) * u
        for i, piece in enumerate(_pack_pieces(_dot(h.astype(BF16), wd_b[...]))):
            yo_ref[i] = piece

    @pl.when(b >= n_used_ref[0])
    def _():
        yo_ref[...] = jnp.zeros(yo_ref.shape, U32)


def _experts(xs, blk_e, n_used, w_gate, w_up, w_down):
    p = xs.shape[1]
    tb = EXPERT_BLOCK
    nblk = p // tb
    idx = jnp.arange(nblk, dtype=jnp.int32)
    changed = jnp.concatenate([jnp.ones((1,), jnp.bool_), blk_e[1:] != blk_e[:-1]])
    run_start = (changed & (idx < n_used[0])).astype(jnp.int32)
    next_start = lax.cummin(jnp.where(run_start == 1, idx, nblk)[::-1])[::-1]
    next_start = jnp.concatenate([next_start[1:], jnp.full((1,), nblk, jnp.int32)])
    has_next = (next_start < nblk).astype(jnp.int32)
    next_e = blk_e[jnp.minimum(next_start, nblk - 1)]
    slot = (jnp.cumsum(run_start) - 1) % 2
    row_block = pl.BlockSpec((N_PIECES, tb, PIECE_COLS), lambda i, *_: (0, i, 0))
    whole = pl.BlockSpec(memory_space=pl.ANY)
    grid_spec = pltpu.PrefetchScalarGridSpec(
        num_scalar_prefetch=6,
        grid=(nblk,),
        in_specs=[row_block, whole, whole, whole],
        out_specs=row_block,
        scratch_shapes=[pltpu.VMEM((2, D_MODEL, D_EXPERT), F32),
                        pltpu.VMEM((2, D_MODEL, D_EXPERT), F32),
                        pltpu.VMEM((2, D_EXPERT, D_MODEL), F32),
                        pltpu.SemaphoreType.DMA((2, 3)),
                        pltpu.VMEM((D_MODEL, D_EXPERT), BF16),
                        pltpu.VMEM((D_MODEL, D_EXPERT), BF16),
                        pltpu.VMEM((D_EXPERT, D_MODEL), BF16)],
    )
    return pl.pallas_call(
        _expert_kernel,
        grid_spec=grid_spec,
        out_shape=jax.ShapeDtypeStruct((N_PIECES, p, PIECE_COLS), U32),
        compiler_params=pltpu.CompilerParams(dimension_semantics=("arbitrary",), vmem_limit_bytes=VMEM_LIMIT),
        name="experts",
    )(blk_e, run_start, next_e, has_next, slot.astype(jnp.int32), n_used, xs, w_gate, w_up, w_down)


def _combine_kernel(x1_ref, yg_ref, route_ref, ga2_ref, fg_ref, o_ref):
    bb, tm, _ = x1_ref.shape
    rows = bb * tm
    route = jnp.concatenate([route_ref[...], jnp.zeros((LANES - SUBLANES, rows), F32)], axis=0)
    w = route.T
    y = None
    for k in range(2):
        chunks = _unpack_pieces([yg_ref[k, i] for i in range(N_PIECES)])
        yk = w[:, 2 + k:3 + k] * jnp.concatenate(chunks, axis=1)
        y = yk if y is None else y + yk
    out = x1_ref[...] + ga2_ref[...] * y.reshape(bb, tm, D_MODEL)
    o_ref[...] = _rms(out) * fg_ref[...]


def _combine(x1, yg, route, ga2, final_g, bb, tm):
    b, t, _ = x1.shape
    nb, nt = b // bb, t // tm
    rows = bb * tm
    return pl.pallas_call(
        _combine_kernel,
        grid=(nb, nt),
        in_specs=[pl.BlockSpec((bb, tm, D_MODEL), lambda i, j: (i, j, 0)),
                  pl.BlockSpec((2, N_PIECES, rows, PIECE_COLS), lambda i, j: (0, 0, i * nt + j, 0)),
                  pl.BlockSpec((SUBLANES, rows), lambda i, j: (0, i * nt + j)),
                  pl.BlockSpec((bb, 1, D_MODEL), lambda i, j: (i, 0, 0)),
                  pl.BlockSpec((1, D_MODEL), lambda i, j: (0, 0))],
        out_specs=pl.BlockSpec((bb, tm, D_MODEL), lambda i, j: (i, j, 0)),
        out_shape=jax.ShapeDtypeStruct((b, t, D_MODEL), F32),
        compiler_params=pltpu.CompilerParams(dimension_semantics=("arbitrary", "arbitrary"),
                                             vmem_limit_bytes=VMEM_LIMIT),
        name="combine",
    )(x1, yg, route, ga2, final_g)


def _block_diag(w):
    n, k, _ = w.shape
    eye = jnp.eye(n, dtype=w.dtype)
    return (eye[:, None, :, None] * w[:, :, None, :]).reshape(n * k, n * k)


def _tiles(f, tile):
    b, _, length = f.shape
    return f.reshape(b, N_HEADS // 2, 2, length // tile, tile).transpose(0, 1, 3, 2, 4)


def _untile_rows(a, b, t):
    nb, nt, r, rows = a.shape
    bb = b // nb
    tm = t // nt
    return a.reshape(nb, nt, r, bb, tm).transpose(2, 0, 3, 1, 4).reshape(r, b, t)


def _group_front(x, c_mod, layer, conv_past, h0, cache):
    b, t, _ = x.shape
    sh1, sc1, ga1, sh2, sc2, ga2 = [m.reshape(b, 1, D_MODEL) for m in jnp.split(c_mod, 6, axis=-1)]
    q, k, v, lf_steps, rec, h_t, ctail = _inproj(
        x, 1.0 + sc1, sh1, layer["n1"], layer["w_all"], layer["b_f"], layer["conv_w"], layer["conv_b"],
        layer["wa_bd"], layer["b_a"], layer["wx_bd"], layer["b_x"], layer["lam"], layer["g_rec"], conv_past, h0)
    lf_t = _untile_rows(lf_steps, b, t).transpose(1, 0, 2)
    k_leaf = k.reshape(1, b, t, N_HEADS, HEAD_DIM)
    v_leaf = v.reshape(1, b, t, N_HEADS, HEAD_DIM)
    if cache is None:
        attn = _attention(q, k, v, _tiles(_cumsum_time(lf_t), min(t, ATTN_CHUNK)))
        attn, k_leaf, v_leaf = lax.optimization_barrier((attn, k_leaf, v_leaf))
    else:
        k_t, v_t, lf_past_t = cache
        total = k_t.shape[-1] + t
        padded = -(-total // CUMSUM_CHUNK) * CUMSUM_CHUNK
        lf_all = jnp.concatenate([lf_past_t, lf_t, jnp.zeros((b, N_HEADS, padded - total), F32)], axis=2)
        attn = _attention_cached(q, k, v, _cumsum_time(lf_all), k_t, v_t)
    bb, tm = (1, ROW_TILE) if t >= ROW_TILE else (ROW_TILE // t, t)
    x1, u2p, route, counts = _outproj(attn, rec, x, ga1, 1.0 + sc2, sh2, layer["n2"], layer["g_att"],
                                      layer["wo_a"], layer["wo_r"], layer["w_r"], layer["b_r"], bb, tm)
    route = _untile_rows(route, b, t).reshape(SUBLANES, b * t)
    route = (route, counts[:, 0].astype(jnp.int32))
    leaves = (k_leaf, v_leaf, lf_t.transpose(0, 2, 1)[None], h_t[None], ctail[None, :, SUBLANES - (CONV_W - 1):, :])
    combine_tile = (1, COMBINE_TILE) if t >= COMBINE_TILE else (bb, tm)
    return x1, u2p.reshape(N_PIECES, b * t, PIECE_COLS), route, ga2, combine_tile, leaves


def kernel(x_prompt, x_sample, c_prompt, c_sample, cache_k, cache_v, cache_logf, state_h, state_conv, norm1_g, norm2_g, w_ada, b_ada, w_in, b_f, conv_w, conv_b, w_rg_a, b_rg_a, w_rg_x, b_rg_x, rg_lambda, out_g_att, out_g_rec, w_out, w_route_group, b_route_group, w_route_expert, b_route_expert, w_exp_gate, w_exp_up, w_exp_down, final_g):
    bp, tp, _ = x_prompt.shape
    bs, ts, _ = x_sample.shape
    l = 0
    o3 = 3 * W_ATT
    o4 = o3 + N_HEADS
    w_in_l = w_in[l]
    w_all = jnp.concatenate([w_in_l[:, :o3], w_in_l[:, o4:], w_in_l[:, o3:o4],
                             jnp.zeros((D_MODEL, LANES - N_HEADS), F32)], axis=1).astype(BF16)
    w_r = jnp.zeros((D_MODEL, ROUTE_COLS), F32)
    w_r = w_r.at[:, 0:N_GROUPS].set(w_route_group[l]).at[:, EXPERT_COL0:EXPERT_COL0 + N_EXPERTS].set(w_route_expert[l])
    b_r = jnp.zeros((1, ROUTE_COLS), F32)
    b_r = b_r.at[0, 0:N_GROUPS].set(b_route_group[l]).at[0, N_GROUPS:SUBLANES].set(NEG_BIG)
    b_r = b_r.at[0, EXPERT_COL0:EXPERT_COL0 + N_EXPERTS].set(b_route_expert[l])
    layer = {
        "n1": norm1_g[l].reshape(1, D_MODEL), "n2": norm2_g[l].reshape(1, D_MODEL),
        "w_all": w_all,
        "b_f": jnp.concatenate([b_f[l], jnp.zeros((LANES - N_HEADS,), F32)]).reshape(1, LANES),
        "conv_w": conv_w[l], "conv_b": conv_b[l].reshape(1, W_REC),
        "wa_bd": _block_diag(w_rg_a[l]).astype(BF16), "b_a": b_rg_a[l].reshape(1, W_REC),
        "wx_bd": _block_diag(w_rg_x[l]).astype(BF16), "b_x": b_rg_x[l].reshape(1, W_REC),
        "lam": rg_lambda[l].reshape(1, W_REC), "g_rec": out_g_rec[l].reshape(1, W_REC),
        "g_att": out_g_att[l].reshape(1, W_ATT),
        "wo_a": w_out[l, :W_ATT].astype(BF16), "wo_r": w_out[l, W_ATT:].astype(BF16),
        "w_r": w_r.astype(BF16), "b_r": b_r,
    }

    ada = _ada(jnp.concatenate([c_prompt, c_sample], axis=0), w_ada[l], b_ada[l])
    pad_rows = SUBLANES - (CONV_W - 1)
    x1p, u2p_p, route_p, ga2p, tile_p, leaves_p = _group_front(
        x_prompt, ada[:bp], layer, jnp.zeros((bp, SUBLANES, W_REC), F32), jnp.zeros((bp, W_REC), F32), None)
    conv_past_s = jnp.concatenate([jnp.zeros((bs, pad_rows, W_REC), F32), state_conv[l]], axis=1)
    x1s, u2p_s, route_s, ga2s, tile_s, leaves_s = _group_front(
        x_sample, ada[bp:], layer, conv_past_s, state_h[l],
        (cache_k[l].transpose(0, 2, 3, 1), cache_v[l].transpose(0, 2, 3, 1), cache_logf[l].transpose(0, 2, 1)))

    (route_p, counts_p), (route_s, counts_s) = route_p, route_s
    n_p, n_s = bp * tp, bs * ts
    n_slot = 2 * (n_p + n_s)
    tb = EXPERT_BLOCK
    experts = jnp.arange(N_EXPERTS, dtype=jnp.int32)
    counts = counts_p + counts_s
    padded = ((counts + tb - 1) // tb) * tb
    pend = jnp.cumsum(padded)
    pstart = pend - padded

    def slot_rows(route, first_row):
        eid = route[0:2].astype(jnp.int32)
        onehot = (eid[:, :, None] == experts[None, None, :]).astype(jnp.int32)
        return jnp.sum(onehot * first_row[None, None, :], axis=2) + route[4:6].astype(jnp.int32)

    dests = [[d[k].reshape(1, -1) for k in range(2)]
             for d in (slot_rows(route_p, pstart), slot_rows(route_s, pstart + counts_p))]
    nblk = -(-n_slot // tb) + N_EXPERTS
    n_rows = nblk * tb
    blk_row0 = jnp.arange(nblk, dtype=jnp.int32) * tb
    blk_e = jnp.minimum(jnp.sum((pend[None, :] <= blk_row0[:, None]).astype(jnp.int32), axis=1), N_EXPERTS - 1)
    n_used = (pend[-1:] // tb).astype(jnp.int32)
    pad_e = padded - counts
    cpad = jnp.cumsum(pad_e)
    j = jnp.arange(n_rows - n_slot, dtype=jnp.int32)
    owner = (cpad[None, :] <= j[:, None]).astype(jnp.int32)
    e_j = jnp.minimum(jnp.sum(owner, axis=1), N_EXPERTS - 1)
    sel = (e_j[:, None] == experts[None, :]).astype(jnp.int32)
    in_expert = jnp.sum(sel * (pstart + counts - (cpad - pad_e))[None, :], axis=1) + j
    dest_pad = jnp.where(j < cpad[-1], in_expert, pend[-1] + (j - cpad[-1])).reshape(1, -1)

    xs = _dispatch([u2p_p, u2p_s], dests, dest_pad, n_rows)
    yo = _experts(xs, blk_e, n_used, w_exp_gate[l], w_exp_up[l], w_exp_down[l])
    yg_p, yg_s = _collect(yo, dests)

    fg = final_g.reshape(1, D_MODEL)
    y_prompt = _combine(x1p, yg_p, route_p, ga2p, fg, *tile_p)
    y_sample = _combine(x1s, yg_s, route_s, ga2s, fg, *tile_s)
    return (y_prompt, y_sample) + leaves_p + leaves_s
```

```python
import functools
import math

import jax
import jax.numpy as jnp
from jax import lax
from jax.experimental import pallas as pl
from jax.experimental.pallas import tpu as pltpu
from jax.experimental.pallas import tpu_sc as plsc

F32 = jnp.float32
BF16 = jnp.bfloat16
U32 = jnp.uint32

D_MODEL = 1024
N_HEADS = 8
HEAD_DIM = 64
W_ATT = N_HEADS * HEAD_DIM
W_REC = D_MODEL - W_ATT
N_REC_BLOCKS = 8
REC_BLOCK = W_REC // N_REC_BLOCKS
CONV_W = 4
RG_C = 8.0
N_GROUPS = 4
EXPERTS_PER_GROUP = 8
N_EXPERTS = N_GROUPS * EXPERTS_PER_GROUP
D_EXPERT = 256
EPS = 1e-6

LANES = 128
SUBLANES = 8
LOG2E = math.log2(math.e)
Q_SCALE = LOG2E * HEAD_DIM ** -0.5
NEG_BIG = -1e30
VMEM_LIMIT = 48 * 1024 * 1024

SEQ_ROWS = 64
BATCH_ROWS = 8
CUMSUM_CHUNK = 512
CUMSUM_BATCH = 8
ATTN_TILE = 2048
ATTN_CHUNK = 512
ATTN_UNROLL = 2
ROW_TILE = 512
EXPERT_BLOCK = 512
COMBINE_TILE = 1024
GATHER_WINDOW = 128
PIECE_COLS = 256
N_PIECES = D_MODEL // (2 * PIECE_COLS)
ROUTE_COLS = 128
EXPERT_COL0 = 8


def _split3(x):
    hi = x.astype(BF16)
    r1 = x - hi.astype(F32)
    mid = r1.astype(BF16)
    lo = (r1 - mid.astype(F32)).astype(BF16)
    return hi, mid, lo


def _dot(a, b):
    return jnp.dot(a, b, preferred_element_type=F32)


def _rms(x):
    return x * lax.rsqrt(jnp.mean(x * x, axis=-1, keepdims=True) + EPS)


def _sigmoid(x):
    return 0.5 * jnp.tanh(0.5 * x) + 0.5


def _pack_bf16_pairs(y):
    n = y.shape[-1] // 2
    yb = y.astype(BF16).astype(F32)
    lo = pltpu.bitcast(yb[:, :n], U32)
    hi = pltpu.bitcast(yb[:, n:], U32)
    return (lo >> 16) | hi


def _unpack_bf16_pairs(p):
    lo = pltpu.bitcast(p << 16, F32)
    hi = pltpu.bitcast(p & jnp.uint32(0xFFFF0000), F32)
    return lo, hi


def _pack_pieces(y):
    w = 2 * PIECE_COLS
    return [_pack_bf16_pairs(y[:, i * w:(i + 1) * w]) for i in range(N_PIECES)]


def _unpack_pieces(pieces):
    chunks = []
    for p in pieces:
        chunks.extend(_unpack_bf16_pairs(p))
    return chunks


def _ada_kernel(c_ref, w_ref, b_ref, o_ref):
    c = c_ref[...]
    a = c * jax.nn.sigmoid(c)
    a_hi, a_mid, _ = _split3(a)
    w_hi, w_mid, _ = _split3(w_ref[...])
    o_ref[...] = _dot(a_hi, w_hi) + _dot(a_mid, w_hi) + _dot(a_hi, w_mid) + b_ref[...]


def _ada(c, w_ada, b_ada):
    rows = c.shape[0]
    n = w_ada.shape[1]
    tn = 1536
    return pl.pallas_call(
        _ada_kernel,
        grid=(n // tn,),
        in_specs=[pl.BlockSpec((rows, D_MODEL), lambda j: (0, 0)),
                  pl.BlockSpec((D_MODEL, tn), lambda j: (0, j)),
                  pl.BlockSpec((1, tn), lambda j: (0, j))],
        out_specs=pl.BlockSpec((rows, tn), lambda j: (0, j)),
        out_shape=jax.ShapeDtypeStruct((rows, n), F32),
        compiler_params=pltpu.CompilerParams(vmem_limit_bytes=VMEM_LIMIT),
        name="ada",
    )(c, w_ada, b_ada.reshape(1, n))


def _inproj_kernel(x_ref, g1_ref, sh1_ref, n1_ref, w_ref, bf_ref, cw_ref, cb_ref, wa_ref, ba_ref, wx_ref, bx_ref,
                   lam_ref, grec_ref, cpast_ref, h0_ref,
                   q_ref, k_ref, v_ref, lf_ref, rec_ref, ht_ref, ctail_ref,
                   xbuf, ub_scr, z_scr, xc_scr, pr_scr, pi_scr, a_scr, u_scr, hs_scr, gl_scr, h_scr):
    bb, tm, _ = x_ref.shape
    rows = bb * tm
    n_lane_tiles = W_REC // LANES
    half = W_REC // 2
    o_x = 3 * W_ATT

    @pl.when(pl.program_id(1) == 0)
    def _():
        xbuf[:, 0:SUBLANES, :] = cpast_ref[...]
        h_scr[...] = h0_ref[...]

    mod = n1_ref[...] * g1_ref[...]
    for b in range(bb):
        rb = slice(b * tm, (b + 1) * tm)
        ub_scr[rb, :] = (_rms(x_ref[b]) * mod[b] + sh1_ref[b]).astype(BF16)

    z_scr[...] = _dot(ub_scr[...], w_ref[:, o_x:])

    cw = cw_ref[...]
    for b in range(bb):
        rb = slice(b * tm, (b + 1) * tm)
        xb = z_scr[rb, 0:W_REC]
        xbuf[b, SUBLANES:SUBLANES + tm, :] = xb
        xc = cb_ref[...] + cw[3:4, :] * xb
        for j in range(CONV_W - 1):
            off = SUBLANES - (CONV_W - 1) + j
            xc = xc + cw[j:j + 1, :] * xbuf[b, off:off + tm, :]
        tail = xbuf[b, tm:tm + SUBLANES, :]
        ctail_ref[b] = tail
        xbuf[b, 0:SUBLANES, :] = tail
        xc_scr[rb, :] = xc

    xcb = xc_scr[...].astype(BF16)
    for w_gate, pre in ((wa_ref, pr_scr), (wx_ref, pi_scr)):
        pre[:, :half] = _dot(xcb[:, :half], w_gate[:half, :half])
        pre[:, half:] = _dot(xcb[:, half:], w_gate[half:, half:])

    zq = _dot(ub_scr[...], w_ref[:, :o_x])
    q_ref[...] = (zq[:, 0:W_ATT] * Q_SCALE).reshape(bb, tm, W_ATT).astype(BF16)
    k_ref[...] = zq[:, W_ATT:2 * W_ATT].reshape(bb, tm, W_ATT)
    v_ref[...] = zq[:, 2 * W_ATT:3 * W_ATT].reshape(bb, tm, W_ATT)

    zf = z_scr[:, 2 * W_REC:2 * W_REC + LANES] + bf_ref[...]
    lf = jnp.minimum(zf, 0.0) - jnp.log1p(jnp.exp(-jnp.abs(zf)))
    lf_ref[0, 0] = lf.T[0:N_HEADS, :]

    nlam = -lam_ref[...]
    decay = -RG_C * (jnp.maximum(nlam, 0.0) + jnp.log1p(jnp.exp(-jnp.abs(nlam))))
    for b in range(bb):
        rb = slice(b * tm, (b + 1) * tm)
        r = _sigmoid(pr_scr[rb, :] + ba_ref[...])
        i = _sigmoid(pi_scr[rb, :] + bx_ref[...])
        log_a = decay * r
        t = jnp.tanh(log_a)
        neg_expm1 = -2.0 * t / (1.0 - t)
        root = jnp.where(neg_expm1 > 0.0, neg_expm1 * lax.rsqrt(neg_expm1), 0.0)
        pr_scr[rb, :] = jnp.exp(log_a)
        pi_scr[rb, :] = root * (i * xc_scr[rb, :])
        gb = z_scr[rb, W_REC:2 * W_REC]
        gl_scr[rb, :] = 0.5 * gb * (1.0 + jnp.tanh(math.sqrt(2.0 / math.pi) * (gb + 0.044715 * (gb * gb * gb))))

    def lane_tile(ref, c):
        return ref[:, c * LANES:(c + 1) * LANES].reshape(bb, tm, LANES)

    for c in range(n_lane_tiles):
        a_scr[c] = jnp.swapaxes(lane_tile(pr_scr, c), 0, 1)
        u_scr[c] = jnp.swapaxes(lane_tile(pi_scr, c), 0, 1)
    for c in range(n_lane_tiles):
        h = h_scr[:, c * LANES:(c + 1) * LANES]
        for step in range(tm):
            h = a_scr[c, step] * h + u_scr[c, step]
            hs_scr[c, step] = h
        h_scr[:, c * LANES:(c + 1) * LANES] = h
    ht_ref[...] = h_scr[...]
    for c in range(n_lane_tiles):
        pr_scr[:, c * LANES:(c + 1) * LANES] = jnp.swapaxes(hs_scr[c], 0, 1).reshape(rows, LANES)

    for b in range(bb):
        rb = slice(b * tm, (b + 1) * tm)
        y = gl_scr[rb, :] * pr_scr[rb, :]
        rec_ref[b] = (_rms(y) * grec_ref[...]).astype(BF16)


def _inproj(x, g1, sh1, n1, w_all, b_f, conv_w, conv_b, wa_bd, b_a, wx_bd, b_x, lam, g_rec, conv_past, h0):
    b, t, _ = x.shape
    bb, tm = BATCH_ROWS, SEQ_ROWS
    nb, nt = b // bb, t // tm
    rows = bb * tm
    wcols = w_all.shape[1]
    const2 = lambda shape: pl.BlockSpec(shape, lambda i, j: (0, 0))
    per_b = lambda last: pl.BlockSpec((bb, 1, last), lambda i, j: (i, 0, 0))
    seq = lambda last: pl.BlockSpec((bb, tm, last), lambda i, j: (i, j, 0))
    return pl.pallas_call(
        _inproj_kernel,
        grid=(nb, nt),
        in_specs=[seq(D_MODEL), per_b(D_MODEL), per_b(D_MODEL), const2((1, D_MODEL)),
                  const2((D_MODEL, wcols)), const2((1, LANES)),
                  const2((CONV_W, W_REC)), const2((1, W_REC)),
                  const2((W_REC, W_REC)), const2((1, W_REC)), const2((W_REC, W_REC)), const2((1, W_REC)),
                  const2((1, W_REC)), const2((1, W_REC)),
                  pl.BlockSpec((bb, SUBLANES, W_REC), lambda i, j: (i, 0, 0)),
                  pl.BlockSpec((bb, W_REC), lambda i, j: (i, 0))],
        out_specs=[seq(W_ATT), seq(W_ATT), seq(W_ATT),
                   pl.BlockSpec((1, 1, N_HEADS, rows), lambda i, j: (i, j, 0, 0)),
                   seq(W_REC),
                   pl.BlockSpec((bb, W_REC), lambda i, j: (i, 0)),
                   pl.BlockSpec((bb, SUBLANES, W_REC), lambda i, j: (i, 0, 0))],
        out_shape=[jax.ShapeDtypeStruct((b, t, W_ATT), BF16),
                   jax.ShapeDtypeStruct((b, t, W_ATT), F32),
                   jax.ShapeDtypeStruct((b, t, W_ATT), F32),
                   jax.ShapeDtypeStruct((nb, nt, N_HEADS, rows), F32),
                   jax.ShapeDtypeStruct((b, t, W_REC), BF16),
                   jax.ShapeDtypeStruct((b, W_REC), F32),
                   jax.ShapeDtypeStruct((b, SUBLANES, W_REC), F32)],
        scratch_shapes=[pltpu.VMEM((bb, tm + SUBLANES, W_REC), F32),
                        pltpu.VMEM((rows, D_MODEL), BF16),
                        pltpu.VMEM((rows, 2 * W_REC + LANES), F32),
                        pltpu.VMEM((rows, W_REC), F32),
                        pltpu.VMEM((rows, W_REC), F32),
                        pltpu.VMEM((rows, W_REC), F32),
                        pltpu.VMEM((W_REC // LANES, tm, bb, LANES), F32),
                        pltpu.VMEM((W_REC // LANES, tm, bb, LANES), F32),
                        pltpu.VMEM((W_REC // LANES, tm, bb, LANES), F32),
                        pltpu.VMEM((rows, W_REC), F32),
                        pltpu.VMEM((bb, W_REC), F32)],
        compiler_params=pltpu.CompilerParams(dimension_semantics=("arbitrary", "arbitrary"),
                                             vmem_limit_bytes=VMEM_LIMIT),
        name="inproj",
    )(x, g1, sh1, n1, w_all, b_f, conv_w, conv_b, wa_bd, b_a, wx_bd, b_x, lam, g_rec, conv_past, h0)


def _cumsum_kernel(x_ref, o_ref):
    cb, heads, length = x_ref.shape
    n = cb * heads
    c = CUMSUM_CHUNK
    upper = (lax.broadcasted_iota(jnp.int32, (c, c), 0) <= lax.broadcasted_iota(jnp.int32, (c, c), 1)).astype(BF16)
    carry = jnp.zeros((n, 1), F32)
    for j in range(length // c):
        x = x_ref[:, :, j * c:(j + 1) * c].reshape(n, c)
        hi, mid, lo = _split3(x)
        parts = jnp.concatenate([hi.astype(F32), mid.astype(F32), lo.astype(F32)], axis=0)
        sums = _dot(parts.astype(BF16), upper)
        out = sums[0:n] + sums[n:2 * n] + sums[2 * n:3 * n] + carry
        o_ref[:, :, j * c:(j + 1) * c] = out.reshape(cb, heads, c)
        carry = out[:, c - 1:c]


def _cumsum_time(lf_t):
    b, h, length = lf_t.shape
    cb = min(b, CUMSUM_BATCH)
    return pl.pallas_call(
        _cumsum_kernel,
        grid=(b // cb,),
        in_specs=[pl.BlockSpec((cb, h, length), lambda i: (i, 0, 0))],
        out_specs=pl.BlockSpec((cb, h, length), lambda i: (i, 0, 0)),
        out_shape=jax.ShapeDtypeStruct((b, h, length), F32),
        name="cumsum",
    )(lf_t)


def _attn_tile(qhs, kt, vts, biases, m_refs, acc_refs, mask):
    heads = range(len(qhs))
    dims = (((1,), (1,)), ((), ()))
    s = [lax.dot_general(qhs[h], kt, dims, preferred_element_type=F32) + biases[h] for h in heads]
    if mask is not None:
        s = [jnp.where(mask, s[h], NEG_BIG) for h in heads]
    n_slabs = s[0].shape[1] // LANES
    slabs = [[s[h][:, c * LANES:(c + 1) * LANES] for c in range(n_slabs)] for h in heads]
    m = [m_refs[h][...] for h in heads]
    m_new = []
    for h in heads:
        part = slabs[h][0]
        for sl in slabs[h][1:]:
            part = jnp.maximum(part, sl)
        m_new.append(jnp.maximum(m[h], jnp.broadcast_to(jnp.max(part, axis=1, keepdims=True), m[h].shape)))
    p = [jnp.concatenate([jnp.exp2(sl - m_new[h]) for sl in slabs[h]], axis=1).astype(BF16) for h in heads]
    pv = [_dot(p[h], vts[h]) for h in heads]
    for h in heads:
        acc_refs[h][...] = jnp.exp2(m[h] - m_new[h]) * acc_refs[h][...] + pv[h]
        m_refs[h][...] = m_new[h]


def _attn_kernel(q_ref, k_ref, v_ref, f_ref, o_ref, kb, vb, m_scr, acc_scr, *, tq, ck):
    qi = pl.program_id(2)
    chunks_per_tile = tq // ck

    @pl.when(qi == 0)
    def _():
        v_f32 = v_ref[0]
        first = lax.broadcasted_iota(jnp.int32, v_f32.shape, 1) < HEAD_DIM
        kb[...] = k_ref[0].astype(BF16)
        vb[0] = jnp.where(first, v_f32, 1.0).astype(BF16)
        vb[1] = jnp.where(first, 1.0, v_f32).astype(BF16)

    q = q_ref[0]
    first_head = lax.broadcasted_iota(jnp.int32, q.shape, 1) < HEAD_DIM
    qh = (jnp.where(first_head, q, jnp.zeros_like(q)), jnp.where(first_head, jnp.zeros_like(q), q))

    ref_f = f_ref[0, 0, qi * chunks_per_tile][:, 0:1]
    m_scr[...] = jnp.full(m_scr.shape, NEG_BIG, F32)
    acc_scr[...] = jnp.zeros(acc_scr.shape, F32)

    def chunk(j, row0=0, masked=False):
        start = pl.multiple_of(j * ck, ck)
        kt = kb[pl.ds(start, ck), :]
        f_tile = f_ref[0, 0, j]
        mask = None
        if masked:
            shape = (tq - row0, ck)
            mask = lax.broadcasted_iota(jnp.int32, shape, 1) <= lax.broadcasted_iota(jnp.int32, shape, 0)
        bias = (ref_f - f_tile) * LOG2E
        _attn_tile([qh[hd][row0:, :] for hd in range(2)], kt, [vb[hd, pl.ds(start, ck), :] for hd in range(2)],
                   [bias[hd:hd + 1, :] for hd in range(2)],
                   [m_scr.at[hd, row0:tq] for hd in range(2)], [acc_scr.at[hd, row0:tq] for hd in range(2)], mask)

    n_chunks = qi * chunks_per_tile

    def chunk_body(jj, carry):
        for u in range(ATTN_UNROLL):
            chunk(jj * ATTN_UNROLL + u)
        return carry

    lax.fori_loop(0, n_chunks // ATTN_UNROLL, chunk_body, 0)
    for r in range(ATTN_UNROLL - 1):
        @pl.when(r < n_chunks % ATTN_UNROLL)
        def _():
            chunk((n_chunks // ATTN_UNROLL) * ATTN_UNROLL + r)

    for c in range(chunks_per_tile):
        chunk(n_chunks + c, row0=c * ck, masked=True)

    outs = []
    for hd in range(2):
        acc = acc_scr[hd]
        outs.append(acc / pltpu.roll(acc, HEAD_DIM, axis=1))
    o_ref[0] = jnp.where(first_head, outs[0], outs[1]).astype(BF16)


def _attention(q, k, v, f_tiles):
    b, t, _ = q.shape
    tq = min(t, ATTN_TILE)
    ck = f_tiles.shape[-1]
    q_spec = pl.BlockSpec((1, tq, LANES), lambda i, p, j: (i, j, p))
    full = pl.BlockSpec((1, t, LANES), lambda i, p, j: (i, 0, p))
    return pl.pallas_call(
        functools.partial(_attn_kernel, tq=tq, ck=ck),
        grid=(b, N_HEADS // 2, t // tq),
        in_specs=[q_spec, full, full,
                  pl.BlockSpec((1, 1) + f_tiles.shape[2:], lambda i, p, j: (i, p, 0, 0, 0))],
        out_specs=q_spec,
        out_shape=jax.ShapeDtypeStruct((b, t, W_ATT), BF16),
        scratch_shapes=[pltpu.VMEM((t, LANES), BF16), pltpu.VMEM((2, t, LANES), BF16),
                        pltpu.VMEM((2, tq, LANES), F32), pltpu.VMEM((2, tq, LANES), F32)],
        compiler_params=pltpu.CompilerParams(dimension_semantics=("arbitrary", "arbitrary", "arbitrary"),
                                             vmem_limit_bytes=VMEM_LIMIT),
        name="attn",
    )(q, k, v, f_tiles)


def _attn_cached_kernel(q_ref, kn_ref, vn_ref, f_ref, kt_ref, vt_ref, o_ref, *, past):
    t = q_ref.shape[1]
    q = q_ref[0]
    kn = kn_ref[0].astype(BF16)
    vn = vn_ref[0].astype(BF16)
    f = f_ref[0]
    causal = lax.broadcasted_iota(jnp.int32, (t, t), 1) <= lax.broadcasted_iota(jnp.int32, (t, t), 0)
    last_dims = (((1,), (1,)), ((), ()))
    heads = range(N_HEADS)
    cols = [slice(h * HEAD_DIM, (h + 1) * HEAD_DIM) for h in heads]
    bias = (f[:, past:past + 1] - f) * LOG2E
    s_p = [_dot(q[:, cols[h]], kt_ref[0, h].astype(BF16)) + bias[h:h + 1, 0:past] for h in heads]
    s_n = [jnp.where(causal, lax.dot_general(q[:, cols[h]], kn[:, cols[h]], last_dims, preferred_element_type=F32)
                     + bias[h:h + 1, past:past + t], NEG_BIG) for h in heads]
    m = [jnp.maximum(jnp.max(s_p[h], axis=1, keepdims=True), jnp.max(s_n[h], axis=1, keepdims=True)) for h in heads]
    p_p = [jnp.exp2(s_p[h] - m[h]) for h in heads]
    p_n = [jnp.exp2(s_n[h] - m[h]) for h in heads]
    l = [jnp.sum(p_p[h], axis=1, keepdims=True) + jnp.sum(p_n[h], axis=1, keepdims=True) for h in heads]
    o = [lax.dot_general(p_p[h].astype(BF16), vt_ref[0, h].astype(BF16), last_dims, preferred_element_type=F32)
         + _dot(p_n[h].astype(BF16), vn[:, cols[h]]) for h in heads]
    o_ref[0] = jnp.concatenate([o[h] / l[h] for h in heads], axis=1).astype(BF16)


def _attention_cached(q, k_new, v_new, f_all, k_t, v_t):
    b, t, _ = q.shape
    past = k_t.shape[-1]
    rows = pl.BlockSpec((1, t, W_ATT), lambda i: (i, 0, 0))
    cache = pl.BlockSpec((1, N_HEADS, HEAD_DIM, past), lambda i: (i, 0, 0, 0))
    return pl.pallas_call(
        functools.partial(_attn_cached_kernel, past=past),
        grid=(b,),
        in_specs=[rows, rows, rows, pl.BlockSpec((1,) + f_all.shape[1:], lambda i: (i, 0, 0)), cache, cache],
        out_specs=rows,
        out_shape=jax.ShapeDtypeStruct((b, t, W_ATT), BF16),
        compiler_params=pltpu.CompilerParams(dimension_semantics=("arbitrary",), vmem_limit_bytes=VMEM_LIMIT),
        name="attn_cached",
    )(q, k_new, v_new, f_all, k_t, v_t)


def _outproj_kernel(attn_ref, rec_ref, x_ref, ga1_ref, g2_ref, sh2_ref, n2_ref, gatt_ref, woa_ref, wor_ref,
                    wr_ref, br_ref,
                    x1_ref, u2p_ref, route_ref, cnt_ref, before_scr, cnt_scr):
    bb, tm, _ = x_ref.shape
    rows = bb * tm

    @pl.when((pl.program_id(0) == 0) & (pl.program_id(1) == 0))
    def _():
        shape = (rows, rows)
        before_scr[...] = (lax.broadcasted_iota(jnp.int32, shape, 0)
                           < lax.broadcasted_iota(jnp.int32, shape, 1)).astype(BF16)
        cnt_scr[...] = jnp.zeros(cnt_scr.shape, F32)

    attn = attn_ref[...].astype(F32)
    an = (_rms(attn) * gatt_ref[...]).reshape(rows, W_ATT).astype(BF16)
    mix = _dot(an, woa_ref[...]) + _dot(rec_ref[...].reshape(rows, W_REC), wor_ref[...])
    x1 = x_ref[...] + ga1_ref[...] * mix.reshape(bb, tm, D_MODEL)
    x1_ref[...] = x1
    u2 = (_rms(x1) * (n2_ref[...] * g2_ref[...]) + sh2_ref[...]).reshape(rows, D_MODEL)
    for i, piece in enumerate(_pack_pieces(u2)):
        u2p_ref[i] = piece.reshape(bb, tm, PIECE_COLS)

    logits = _dot(u2.astype(BF16), wr_ref[...]) + br_ref[...]
    lt = logits.T
    row8 = lax.broadcasted_iota(jnp.int32, (SUBLANES, rows), 0).astype(F32)
    lg = lt[0:SUBLANES]
    m_g = jnp.max(lg, axis=0, keepdims=True)
    gidx = jnp.min(jnp.where(lg == m_g, row8, float(SUBLANES)), axis=0, keepdims=True)
    p_top = 1.0 / jnp.sum(jnp.exp(lg - m_g), axis=0, keepdims=True)
    leg = jnp.zeros((EXPERTS_PER_GROUP, rows), F32)
    for g in range(N_GROUPS):
        lo = EXPERT_COL0 + g * EXPERTS_PER_GROUP
        leg = jnp.where(gidx == float(g), lt[lo:lo + EXPERTS_PER_GROUP], leg)
    v1 = jnp.max(leg, axis=0, keepdims=True)
    i1 = jnp.min(jnp.where(leg == v1, row8, float(SUBLANES)), axis=0, keepdims=True)
    leg2 = jnp.where(row8 == i1, -jnp.inf, leg)
    v2 = jnp.max(leg2, axis=0, keepdims=True)
    i2 = jnp.min(jnp.where(leg2 == v2, row8, float(SUBLANES)), axis=0, keepdims=True)
    e21 = jnp.exp(v2 - v1)
    w1 = p_top / (1.0 + e21)
    w2 = w1 * e21
    base = gidx * float(EXPERTS_PER_GROUP)
    e0 = base + i1
    e1 = base + i2

    erow = lax.broadcasted_iota(jnp.int32, (N_EXPERTS, rows), 0).astype(F32)
    hit0 = erow == e0
    hit1 = erow == e1
    hits = jnp.concatenate([jnp.where(hit0, 1.0, 0.0), jnp.where(hit1, 1.0, 0.0)], axis=0)
    earlier = _dot(hits.astype(BF16), before_scr[...])
    seen = cnt_scr[:, 0:1] + earlier[0:N_EXPERTS] + earlier[N_EXPERTS:]
    rank0 = jnp.sum(jnp.where(hit0, seen, 0.0), axis=0, keepdims=True)
    rank1 = jnp.sum(jnp.where(hit1, seen, 0.0), axis=0, keepdims=True)
    step_counts = jnp.sum(hits[0:N_EXPERTS] + hits[N_EXPERTS:], axis=1, keepdims=True)
    cnt_scr[...] = cnt_scr[...] + step_counts
    cnt_ref[...] = cnt_scr[...]

    out = jnp.zeros((SUBLANES, rows), F32)
    for r_idx, val in enumerate((e0, e1, w1, w2, rank0, rank1)):
        out = jnp.where(row8 == float(r_idx), val, out)
    route_ref[0, 0] = out


def _outproj(attn, rec, x, ga1, g2, sh2, n2, g_att, wo_a, wo_r, w_r, b_r, bb, tm):
    b, t, _ = x.shape
    nb, nt = b // bb, t // tm
    rows = bb * tm
    const2 = lambda shape: pl.BlockSpec(shape, lambda i, j: (0, 0))
    per_b = pl.BlockSpec((bb, 1, D_MODEL), lambda i, j: (i, 0, 0))
    seq = lambda last: pl.BlockSpec((bb, tm, last), lambda i, j: (i, j, 0))
    return pl.pallas_call(
        _outproj_kernel,
        grid=(nb, nt),
        in_specs=[seq(W_ATT), seq(W_REC), seq(D_MODEL), per_b, per_b, per_b, const2((1, D_MODEL)),
                  const2((1, W_ATT)), const2((W_ATT, D_MODEL)), const2((W_REC, D_MODEL)),
                  const2((D_MODEL, ROUTE_COLS)), const2((1, ROUTE_COLS))],
        out_specs=[seq(D_MODEL),
                   pl.BlockSpec((N_PIECES, bb, tm, PIECE_COLS), lambda i, j: (0, i, j, 0)),
                   pl.BlockSpec((1, 1, SUBLANES, rows), lambda i, j: (i, j, 0, 0)),
                   const2((N_EXPERTS, LANES))],
        out_shape=[jax.ShapeDtypeStruct((b, t, D_MODEL), F32),
                   jax.ShapeDtypeStruct((N_PIECES, b, t, PIECE_COLS), U32),
                   jax.ShapeDtypeStruct((nb, nt, SUBLANES, rows), F32),
                   jax.ShapeDtypeStruct((N_EXPERTS, LANES), F32)],
        scratch_shapes=[pltpu.VMEM((rows, rows), BF16), pltpu.VMEM((N_EXPERTS, LANES), F32)],
        compiler_params=pltpu.CompilerParams(dimension_semantics=("arbitrary", "arbitrary"),
                                             vmem_limit_bytes=VMEM_LIMIT),
        name="outproj",
    )(attn, rec, x, ga1, g2, sh2, n2, g_att, wo_a, wo_r, w_r, b_r)


_SC_AXES = ("core", "subcore")


def _sc_mesh():
    return plsc.VectorSubcoreMesh(core_axis_name=_SC_AXES[0], subcore_axis_name=_SC_AXES[1])


def _sc_scatter_rows(src_hbm, idx_hbms, n, dst_hbm, sems):
    def body(x_vmem, *i_vmems):
        copies = [pltpu.make_async_copy(x_vmem, dst_hbm.at[i_vmem.at[0]], sems.at[k])
                  for k, i_vmem in enumerate(i_vmems)]
        for c in copies:
            c.start()
        for c in copies:
            c.wait()

    pltpu.emit_pipeline(
        body,
        grid=(n // GATHER_WINDOW,),
        in_specs=[pl.BlockSpec((GATHER_WINDOW, PIECE_COLS), lambda i: (i, 0))]
        + [pl.BlockSpec((1, GATHER_WINDOW), lambda i: (0, i)) for _ in idx_hbms],
        out_specs=[],
        core_axis_name=_SC_AXES,
        dimension_semantics=(pltpu.PARALLEL,),
    )(src_hbm, *idx_hbms)


def _sc_gather_rows(tab_hbm, idx_hbm, n, dst_hbm):
    def body(i_vmem, o_vmem):
        pltpu.sync_copy(tab_hbm.at[i_vmem.at[0]], o_vmem)

    pltpu.emit_pipeline(
        body,
        grid=(n // GATHER_WINDOW,),
        in_specs=[pl.BlockSpec((1, GATHER_WINDOW), lambda i: (0, i))],
        out_specs=[pl.BlockSpec((GATHER_WINDOW, PIECE_COLS), lambda i: (i, 0))],
        core_axis_name=_SC_AXES,
        dimension_semantics=(pltpu.PARALLEL,),
    )(idx_hbm, dst_hbm)


def _dispatch(tables, dests, dest_pad, n_rows):
    n_pad = dest_pad.shape[1]
    sizes = [t.shape[1] for t in tables]
    n_groups = len(tables)

    @pl.kernel(out_type=jax.ShapeDtypeStruct((N_PIECES, n_rows, PIECE_COLS), U32), mesh=_sc_mesh(),
               scratch_types=[pltpu.SemaphoreType.DMA((2,))])
    def scatter(*refs):
        tabs = refs[:n_groups]
        idxs = refs[n_groups:3 * n_groups]
        pad_hbm, o_hbm, sems = refs[3 * n_groups:]
        for piece in range(N_PIECES):
            dst = o_hbm.at[piece]
            for g in range(n_groups):
                _sc_scatter_rows(tabs[g].at[piece], idxs[2 * g:2 * g + 2], sizes[g], dst, sems)
            _sc_scatter_rows(tabs[0].at[piece], [pad_hbm], n_pad, dst, sems)

    flat_idx = [d for pair in dests for d in pair]
    return scatter(*tables, *flat_idx, dest_pad)


def _collect(yo, dests):
    sizes = [pair[0].shape[1] for pair in dests]
    n_groups = len(dests)
    out_type = [jax.ShapeDtypeStruct((2, N_PIECES, n, PIECE_COLS), U32) for n in sizes]

    @pl.kernel(out_type=out_type, mesh=_sc_mesh(), scratch_types=[])
    def gather(*refs):
        yo_hbm = refs[0]
        idxs = refs[1:1 + 2 * n_groups]
        outs = refs[1 + 2 * n_groups:]
        for piece in range(N_PIECES):
            for g in range(n_groups):
                for k in range(2):
                    _sc_gather_rows(yo_hbm.at[piece], idxs[2 * g + k], sizes[g], outs[g].at[k, piece])

    flat_idx = [d for pair in dests for d in pair]
    return gather(yo, *flat_idx)


def _expert_kernel(blk_e_ref, run_start_ref, next_e_ref, has_next_ref, slot_ref, n_used_ref,
                   xs_ref, wg_hbm, wu_hbm, wd_hbm, yo_ref, wg_f, wu_f, wd_f, sems, wg_b, wu_b, wd_b):
    b = pl.program_id(0)

    def weight_copies(e, s):
        return [pltpu.make_async_copy(src.at[e], dst.at[s], sems.at[s, i])
                for i, (src, dst) in enumerate(((wg_hbm, wg_f), (wu_hbm, wu_f), (wd_hbm, wd_f)))]

    @pl.when(run_start_ref[b] == 1)
    def _():
        s = slot_ref[b]

        @pl.when(b == 0)
        def _():
            for c in weight_copies(blk_e_ref[b], s):
                c.start()

        for c in weight_copies(blk_e_ref[b], s):
            c.wait()
        wg_b[...] = wg_f[s].astype(BF16)
        wu_b[...] = wu_f[s].astype(BF16)
        wd_b[...] = wd_f[s].astype(BF16)

        @pl.when(has_next_ref[b] == 1)
        def _():
            for c in weight_copies(next_e_ref[b], 1 - s):
                c.start()

    @pl.when(b < n_used_ref[0])
    def _():
        chunks = [c.astype(BF16) for c in _unpack_pieces([xs_ref[i] for i in range(N_PIECES)])]
        g = None
        u = None
        for i, xc in enumerate(chunks):
            rows_i = slice(i * PIECE_COLS, (i + 1) * PIECE_COLS)
            gi = _dot(xc, wg_b[rows_i, :])
            ui = _dot(xc, wu_b[rows_i, :])
            g = gi if g is None else g + gi
            u = ui if u is None else u + ui
        h = (g * _sigmoid(g)) * u
        for i, piece in enumerate(_pack_pieces(_dot(h.astype(BF16), wd_b[...]))):
            yo_ref[i] = piece

    @pl.when(b >= n_used_ref[0])
    def _():
        yo_ref[...] = jnp.zeros(yo_ref.shape, U32)


def _experts(xs, blk_e, n_used, w_gate, w_up, w_down):
    p = xs.shape[1]
    tb = EXPERT_BLOCK
    nblk = p // tb
    idx = jnp.arange(nblk, dtype=jnp.int32)
    changed = jnp.concatenate([jnp.ones((1,), jnp.bool_), blk_e[1:] != blk_e[:-1]])
    run_start = (changed & (idx < n_used[0])).astype(jnp.int32)
    next_start = lax.cummin(jnp.where(run_start == 1, idx, nblk)[::-1])[::-1]
    next_start = jnp.concatenate([next_start[1:], jnp.full((1,), nblk, jnp.int32)])
    has_next = (next_start < nblk).astype(jnp.int32)
    next_e = blk_e[jnp.minimum(next_start, nblk - 1)]
    slot = (jnp.cumsum(run_start) - 1) % 2
    row_block = pl.BlockSpec((N_PIECES, tb, PIECE_COLS), lambda i, *_: (0, i, 0))
    whole = pl.BlockSpec(memory_space=pl.ANY)
    grid_spec = pltpu.PrefetchScalarGridSpec(
        num_scalar_prefetch=6,
        grid=(nblk,),
        in_specs=[row_block, whole, whole, whole],
        out_specs=row_block,
        scratch_shapes=[pltpu.VMEM((2, D_MODEL, D_EXPERT), F32),
                        pltpu.VMEM((2, D_MODEL, D_EXPERT), F32),
                        pltpu.VMEM((2, D_EXPERT, D_MODEL), F32),
                        pltpu.SemaphoreType.DMA((2, 3)),
                        pltpu.VMEM((D_MODEL, D_EXPERT), BF16),
                        pltpu.VMEM((D_MODEL, D_EXPERT), BF16),
                        pltpu.VMEM((D_EXPERT, D_MODEL), BF16)],
    )
    return pl.pallas_call(
        _expert_kernel,
        grid_spec=grid_spec,
        out_shape=jax.ShapeDtypeStruct((N_PIECES, p, PIECE_COLS), U32),
        compiler_params=pltpu.CompilerParams(dimension_semantics=("arbitrary",), vmem_limit_bytes=VMEM_LIMIT),
        name="experts",
    )(blk_e, run_start, next_e, has_next, slot.astype(jnp.int32), n_used, xs, w_gate, w_up, w_down)


def _combine_kernel(x1_ref, yg_ref, route_ref, ga2_ref, fg_ref, o_ref):
    bb, tm, _ = x1_ref.shape
    rows = bb * tm
    route = jnp.concatenate([route_ref[...], jnp.zeros((LANES - SUBLANES, rows), F32)], axis=0)
    w = route.T
    y = None
    for k in range(2):
        chunks = _unpack_pieces([yg_ref[k, i] for i in range(N_PIECES)])
        yk = w[:, 2 + k:3 + k] * jnp.concatenate(chunks, axis=1)
        y = yk if y is None else y + yk
    out = x1_ref[...] + ga2_ref[...] * y.reshape(bb, tm, D_MODEL)
    o_ref[...] = _rms(out) * fg_ref[...]


def _combine(x1, yg, route, ga2, final_g, bb, tm):
    b, t, _ = x1.shape
    nb, nt = b // bb, t // tm
    rows = bb * tm
    return pl.pallas_call(
        _combine_kernel,
        grid=(nb, nt),
        in_specs=[pl.BlockSpec((bb, tm, D_MODEL), lambda i, j: (i, j, 0)),
                  pl.BlockSpec((2, N_PIECES, rows, PIECE_COLS), lambda i, j: (0, 0, i * nt + j, 0)),
                  pl.BlockSpec((SUBLANES, rows), lambda i, j: (0, i * nt + j)),
                  pl.BlockSpec((bb, 1, D_MODEL), lambda i, j: (i, 0, 0)),
                  pl.BlockSpec((1, D_MODEL), lambda i, j: (0, 0))],
        out_specs=pl.BlockSpec((bb, tm, D_MODEL), lambda i, j: (i, j, 0)),
        out_shape=jax.ShapeDtypeStruct((b, t, D_MODEL), F32),
        compiler_params=pltpu.CompilerParams(dimension_semantics=("arbitrary", "arbitrary"),
                                             vmem_limit_bytes=VMEM_LIMIT),
        name="combine",
    )(x1, yg, route, ga2, final_g)


def _block_diag(w):
    n, k, _ = w.shape
    eye = jnp.eye(n, dtype=w.dtype)
    return (eye[:, None, :, None] * w[:, :, None, :]).reshape(n * k, n * k)


def _tiles(f, tile):
    b, _, length = f.shape
    return f.reshape(b, N_HEADS // 2, 2, length // tile, tile).transpose(0, 1, 3, 2, 4)


def _untile_rows(a, b, t):
    nb, nt, r, rows = a.shape
    bb = b // nb
    tm = t // nt
    return a.reshape(nb, nt, r, bb, tm).transpose(2, 0, 3, 1, 4).reshape(r, b, t)


def _group_front(x, c_mod, layer, conv_past, h0, cache):
    b, t, _ = x.shape
    sh1, sc1, ga1, sh2, sc2, ga2 = [m.reshape(b, 1, D_MODEL) for m in jnp.split(c_mod, 6, axis=-1)]
    q, k, v, lf_steps, rec, h_t, ctail = _inproj(
        x, 1.0 + sc1, sh1, layer["n1"], layer["w_all"], layer["b_f"], layer["conv_w"], layer["conv_b"],
        layer["wa_bd"], layer["b_a"], layer["wx_bd"], layer["b_x"], layer["lam"], layer["g_rec"], conv_past, h0)
    lf_t = _untile_rows(lf_steps, b, t).transpose(1, 0, 2)
    k_leaf = k.reshape(1, b, t, N_HEADS, HEAD_DIM)
    v_leaf = v.reshape(1, b, t, N_HEADS, HEAD_DIM)
    if cache is None:
        attn = _attention(q, k, v, _tiles(_cumsum_time(lf_t), min(t, ATTN_CHUNK)))
        attn, k_leaf, v_leaf = lax.optimization_barrier((attn, k_leaf, v_leaf))
    else:
        k_t, v_t, lf_past_t = cache
        total = k_t.shape[-1] + t
        padded = -(-total // CUMSUM_CHUNK) * CUMSUM_CHUNK
        lf_all = jnp.concatenate([lf_past_t, lf_t, jnp.zeros((b, N_HEADS, padded - total), F32)], axis=2)
        attn = _attention_cached(q, k, v, _cumsum_time(lf_all), k_t, v_t)
    bb, tm = (1, ROW_TILE) if t >= ROW_TILE else (ROW_TILE // t, t)
    x1, u2p, route, counts = _outproj(attn, rec, x, ga1, 1.0 + sc2, sh2, layer["n2"], layer["g_att"],
                                      layer["wo_a"], layer["wo_r"], layer["w_r"], layer["b_r"], bb, tm)
    route = _untile_rows(route, b, t).reshape(SUBLANES, b * t)
    route = (route, counts[:, 0].astype(jnp.int32))
    leaves = (k_leaf, v_leaf, lf_t.transpose(0, 2, 1)[None], h_t[None], ctail[None, :, SUBLANES - (CONV_W - 1):, :])
    combine_tile = (1, COMBINE_TILE) if t >= COMBINE_TILE else (bb, tm)
    return x1, u2p.reshape(N_PIECES, b * t, PIECE_COLS), route, ga2, combine_tile, leaves


def kernel(x_prompt, x_sample, c_prompt, c_sample, cache_k, cache_v, cache_logf, state_h, state_conv, norm1_g, norm2_g, w_ada, b_ada, w_in, b_f, conv_w, conv_b, w_rg_a, b_rg_a, w_rg_x, b_rg_x, rg_lambda, out_g_att, out_g_rec, w_out, w_route_group, b_route_group, w_route_expert, b_route_expert, w_exp_gate, w_exp_up, w_exp_down, final_g):
    bp, tp, _ = x_prompt.shape
    bs, ts, _ = x_sample.shape
    l = 0
    o3 = 3 * W_ATT
    o4 = o3 + N_HEADS
    w_in_l = w_in[l]
    w_all = jnp.concatenate([w_in_l[:, :o3], w_in_l[:, o4:], w_in_l[:, o3:o4],
                             jnp.zeros((D_MODEL, LANES - N_HEADS), F32)], axis=1).astype(BF16)
    w_r = jnp.zeros((D_MODEL, ROUTE_COLS), F32)
    w_r = w_r.at[:, 0:N_GROUPS].set(w_route_group[l]).at[:, EXPERT_COL0:EXPERT_COL0 + N_EXPERTS].set(w_route_expert[l])
    b_r = jnp.zeros((1, ROUTE_COLS), F32)
    b_r = b_r.at[0, 0:N_GROUPS].set(b_route_group[l]).at[0, N_GROUPS:SUBLANES].set(NEG_BIG)
    b_r = b_r.at[0, EXPERT_COL0:EXPERT_COL0 + N_EXPERTS].set(b_route_expert[l])
    layer = {
        "n1": norm1_g[l].reshape(1, D_MODEL), "n2": norm2_g[l].reshape(1, D_MODEL),
        "w_all": w_all,
        "b_f": jnp.concatenate([b_f[l], jnp.zeros((LANES - N_HEADS,), F32)]).reshape(1, LANES),
        "conv_w": conv_w[l], "conv_b": conv_b[l].reshape(1, W_REC),
        "wa_bd": _block_diag(w_rg_a[l]).astype(BF16), "b_a": b_rg_a[l].reshape(1, W_REC),
        "wx_bd": _block_diag(w_rg_x[l]).astype(BF16), "b_x": b_rg_x[l].reshape(1, W_REC),
        "lam": rg_lambda[l].reshape(1, W_REC), "g_rec": out_g_rec[l].reshape(1, W_REC),
        "g_att": out_g_att[l].reshape(1, W_ATT),
        "wo_a": w_out[l, :W_ATT].astype(BF16), "wo_r": w_out[l, W_ATT:].astype(BF16),
        "w_r": w_r.astype(BF16), "b_r": b_r,
    }

    ada = _ada(jnp.concatenate([c_prompt, c_sample], axis=0), w_ada[l], b_ada[l])
    pad_rows = SUBLANES - (CONV_W - 1)
    x1p, u2p_p, route_p, ga2p, tile_p, leaves_p = _group_front(
        x_prompt, ada[:bp], layer, jnp.zeros((bp, SUBLANES, W_REC), F32), jnp.zeros((bp, W_REC), F32), None)
    conv_past_s = jnp.concatenate([jnp.zeros((bs, pad_rows, W_REC), F32), state_conv[l]], axis=1)
    x1s, u2p_s, route_s, ga2s, tile_s, leaves_s = _group_front(
        x_sample, ada[bp:], layer, conv_past_s, state_h[l],
        (cache_k[l].transpose(0, 2, 3, 1), cache_v[l].transpose(0, 2, 3, 1), cache_logf[l].transpose(0, 2, 1)))

    (route_p, counts_p), (route_s, counts_s) = route_p, route_s
    n_p, n_s = bp * tp, bs * ts
    n_slot = 2 * (n_p + n_s)
    tb = EXPERT_BLOCK
    experts = jnp.arange(N_EXPERTS, dtype=jnp.int32)
    counts = counts_p + counts_s
    padded = ((counts + tb - 1) // tb) * tb
    pend = jnp.cumsum(padded)
    pstart = pend - padded

    def slot_rows(route, first_row):
        eid = route[0:2].astype(jnp.int32)
        onehot = (eid[:, :, None] == experts[None, None, :]).astype(jnp.int32)
        return jnp.sum(onehot * first_row[None, None, :], axis=2) + route[4:6].astype(jnp.int32)

    dests = [[d[k].reshape(1, -1) for k in range(2)]
             for d in (slot_rows(route_p, pstart), slot_rows(route_s, pstart + counts_p))]
    nblk = -(-n_slot // tb) + N_EXPERTS
    n_rows = nblk * tb
    blk_row0 = jnp.arange(nblk, dtype=jnp.int32) * tb
    blk_e = jnp.minimum(jnp.sum((pend[None, :] <= blk_row0[:, None]).astype(jnp.int32), axis=1), N_EXPERTS - 1)
    n_used = (pend[-1:] // tb).astype(jnp.int32)
    pad_e = padded - counts
    cpad = jnp.cumsum(pad_e)
    j = jnp.arange(n_rows - n_slot, dtype=jnp.int32)
    owner = (cpad[None, :] <= j[:, None]).astype(jnp.int32)
    e_j = jnp.minimum(jnp.sum(owner, axis=1), N_EXPERTS - 1)
    sel = (e_j[:, None] == experts[None, :]).astype(jnp.int32)
    in_expert = jnp.sum(sel * (pstart + counts - (cpad - pad_e))[None, :], axis=1) + j
    dest_pad = jnp.where(j < cpad[-1], in_expert, pend[-1] + (j - cpad[-1])).reshape(1, -1)

    xs = _dispatch([u2p_p, u2p_s], dests, dest_pad, n_rows)
    yo = _experts(xs, blk_e, n_used, w_exp_gate[l], w_exp_up[l], w_exp_down[l])
    yg_p, yg_s = _collect(yo, dests)

    fg = final_g.reshape(1, D_MODEL)
    y_prompt = _combine(x1p, yg_p, route_p, ga2p, fg, *tile_p)
    y_sample = _combine(x1s, yg_s, route_s, ga2s, fg, *tile_s)
    return (y_prompt, y_sample) + leaves_p + leaves_s
```
